```python
import math
import jax
import jax.numpy as jnp
from jax import lax
import numpy as np

D_MODEL = 1024
BATCH = 1
SEQ = 16384
DEPTH = 2
DEC_BATCH = 32
DEC_SEQ = 4
PAST_LEN = 16384
PAGE_SIZE = 128

HEAD_DIM = 64
A_PATTERNS = ((128, 1), (512, 4), (2048, 16))
A_GROUP_HEADS = 2
A_HEADS = A_GROUP_HEADS * len(A_PATTERNS)
A_WIDTH = A_HEADS * HEAD_DIM
B_HEADS = 5
B_KEY_DIM = 64
B_VAL_DIM = 128
B_QK_WIDTH = B_HEADS * B_KEY_DIM
B_V_WIDTH = B_HEADS * B_VAL_DIM
B_CHUNK = 128
ROPE_BASE = 10000.0
C_GROUP_SIZE = 16
C_GROUPS = 32
C_WIDTH = C_GROUPS * C_GROUP_SIZE
C_STATE = 64
C_DT_MIN = 0.001
C_DT_MAX = 0.1
C_PARAM_NAMES = ('c_lambda_re', 'c_lambda_im', 'c_log_dt', 'c_b_re', 'c_b_im', 'c_c_re', 'c_c_im', 'c_d', 'c_w_glu', 'c_b_glu')
D_HEADS = 8
D_WIDTH = D_HEADS * HEAD_DIM
IDX_HEADS = 4
IDX_DIM = 64
IDX_TOPK = 256
IDX_SCALE = (IDX_HEADS * IDX_DIM) ** -0.5
D_QBLOCK = 128
T5_BUCKETS = 32
T5_MAX_DIST = 2048
T5_HEADS = max(A_HEADS, D_HEADS)
MOE_GROUPS = 4
MOE_GROUP_EXPERTS = 4
MOE_EXPERTS = MOE_GROUPS * MOE_GROUP_EXPERTS
MOE_TOPK = 2
D_EXPERT = 512
PLE_DIM = 256
N_AB = (DEPTH + 1) // 2
N_CD = DEPTH // 2
AB_IN_WIDTH = 3 * A_WIDTH + 2 * B_QK_WIDTH + 2 * B_V_WIDTH
AB_OUT_WIDTH = A_WIDTH + B_V_WIDTH
CD_IN_WIDTH = C_WIDTH + 3 * D_WIDTH + IDX_HEADS * IDX_DIM + IDX_HEADS + IDX_DIM
CD_OUT_WIDTH = C_WIDTH + D_WIDTH
EPS = 1e-6

kernel_name = 'hybrid_dilated_retention_s5_dsa_step'


def split_cols(x, widths):
    parts, start = [], 0
    for w in widths:
        parts.append(x[..., start:start + w])
        start += w
    return parts


def rmsnorm(x, g):
    xf = x.astype(jnp.float32)
    y = xf * lax.rsqrt(jnp.mean(xf * xf, axis=-1, keepdims=True) + EPS)
    return (y * g.astype(jnp.float32)).astype(x.dtype)


def t5_bucket(dist):
    max_exact = T5_BUCKETS // 2
    d = jnp.maximum(dist, 0)
    scaled = jnp.log(jnp.maximum(d, 1).astype(jnp.float32) / max_exact) / math.log(T5_MAX_DIST / max_exact)
    large = jnp.minimum(max_exact + (scaled * (T5_BUCKETS - max_exact)).astype(jnp.int32), T5_BUCKETS - 1)
    return jnp.where(d < max_exact, d, large)


def rope(x, pos):
    half = x.shape[-1] // 2
    freq = ROPE_BASE ** (-jnp.arange(half, dtype=jnp.float32) / half)
    ang = pos.astype(jnp.float32)[:, None] * freq[None, :]
    cos, sin = jnp.cos(ang)[:, None, :], jnp.sin(ang)[:, None, :]
    xf = x.astype(jnp.float32)
    x1, x2 = xf[..., :half], xf[..., half:]
    return jnp.concatenate([x1 * cos - x2 * sin, x1 * sin + x2 * cos], axis=-1).astype(x.dtype)


def dilated_attn_prompt(q, k, v, dil, nkey, bias):
    bsz, L, H, E = q.shape
    blk = nkey
    span = dil * blk
    lp = -(-L // span) * span
    nb = lp // span

    def to_blocks(t):
        t = jnp.pad(t, ((0, 0), (0, lp - L), (0, 0), (0, 0)))
        t = t.reshape(bsz, lp // dil, dil, H, E).transpose(0, 2, 1, 3, 4)
        return t.reshape(bsz, dil, nb, blk, H, E)

    def with_prev(t):
        prev = jnp.pad(t, ((0, 0), (0, 0), (1, 0), (0, 0), (0, 0), (0, 0)))[:, :, :-1]
        return jnp.concatenate([prev, t], axis=3)

    qb = to_blocks(q)
    kw = with_prev(to_blocks(k))
    vw = with_prev(to_blocks(v))
    qi = jnp.arange(blk)[:, None]
    kj = jnp.arange(2 * blk)[None, :]
    m = blk + qi - kj
    band = (m >= 0) & (m <= nkey)
    has_prev = (jnp.arange(nb)[:, None, None] > 0) | (kj >= blk)[None]
    mask = band[None] & has_prev
    bias_qk = jnp.moveaxis(bias[jnp.clip(m, 0, nkey)], -1, 0)
    s = jnp.einsum('brnqhe,brnkhe->brnhqk', qb, kw).astype(jnp.float32) * (E ** -0.5) + bias_qk
    s = jnp.where(mask[None, None, :, None], s, -jnp.inf)
    lse = jax.nn.logsumexp(s, axis=-1)
    p = jnp.exp(s - lse[..., None])
    o = jnp.einsum('brnhqk,brnkhe->brnqhe', p.astype(vw.dtype), vw)
    o = o.reshape(bsz, dil, lp // dil, H, E).transpose(0, 2, 1, 3, 4).reshape(bsz, lp, H, E)[:, :L]
    lse = jnp.moveaxis(lse, 3, 4).reshape(bsz, dil, lp // dil, H).transpose(0, 2, 1, 3).reshape(bsz, lp, H)[:, :L]
    return o, lse


def dilated_attn_step(q, k_all, v_all, dil, nkey, bias, n_buf):
    S, E = q.shape[1], q.shape[-1]
    rows = n_buf + jnp.arange(S)[:, None] - jnp.arange(nkey + 1)[None, :] * dil
    valid = rows >= 0
    rows_c = jnp.maximum(rows, 0)
    kg = k_all[:, rows_c]
    vg = v_all[:, rows_c]
    s = jnp.einsum('bqhe,bqkhe->bqkh', q, kg).astype(jnp.float32) * (E ** -0.5) + bias[None, None]
    s = jnp.where(valid[None, :, :, None], s, -jnp.inf)
    lse = jax.nn.logsumexp(s, axis=2)
    p = jnp.exp(s - lse[:, :, None])
    o = jnp.einsum('bqkh,bqkhe->bqhe', p.astype(vg.dtype), vg)
    return o, lse


def retention_log_gamma():
    return jnp.log(1.0 - 2.0 ** (-5.0 - jnp.arange(B_HEADS, dtype=jnp.float32)))


def retention(q, k, v, s0, lg, chunk):
    bsz, L, H, dk = q.shape
    dv = v.shape[-1]
    n = L // chunk
    qc = q.reshape(bsz, n, chunk, H, dk).astype(jnp.float32) * (dk ** -0.5)
    kc = k.reshape(bsz, n, chunk, H, dk).astype(jnp.float32)
    vc = v.reshape(bsz, n, chunk, H, dv).astype(jnp.float32)
    i = jnp.arange(chunk, dtype=jnp.float32)
    diff = i[:, None] - i[None, :]
    decay = jnp.where(diff >= 0, jnp.exp(jnp.maximum(diff, 0.0)[None] * lg[:, None, None]), 0.0)
    intra = jnp.einsum('bnihd,bnjhd->bnhij', qc, kc) * decay
    o_intra = jnp.einsum('bnhij,bnjhe->bnihe', intra, vc)
    w_end = jnp.exp((chunk - 1.0 - i)[None, :] * lg[:, None])
    u = jnp.einsum('bnjhd,hj,bnjhe->bnhde', kc, w_end, vc)
    g_chunk = jnp.exp(chunk * lg)[None, :, None, None]

    def step(s, u_n):
        return g_chunk * s + u_n, s

    s_last, s_starts = lax.scan(step, s0, jnp.moveaxis(u, 1, 0))
    s_starts = jnp.moveaxis(s_starts, 0, 1)
    w_in = jnp.exp((i + 1.0)[None, :] * lg[:, None])
    o_cross = jnp.einsum('bnihd,hi,bnhde->bnihe', qc, w_in, s_starts)
    return (o_intra + o_cross).reshape(bsz, L, H, dv), s_last


def complex_affine_combine(e1, e2):
    a1r, a1i, b1r, b1i = e1
    a2r, a2i, b2r, b2i = e2
    return (a1r * a2r - a1i * a2i, a1r * a2i + a1i * a2r,
            a2r * b1r - a2i * b1i + b2r, a2r * b1i + a2i * b1r + b2i)


def s5_layer(u, lam_re, lam_im, log_dt, b_re, b_im, c_re, c_im, d_skip, w_glu, b_glu, x0):
    bsz, L, _ = u.shape
    uf = u.astype(jnp.float32)
    ug = uf.reshape(bsz, L, C_GROUPS, C_GROUP_SIZE)
    dt = jnp.exp(log_dt.astype(jnp.float32))[:, None]
    lr, li = lam_re.astype(jnp.float32), lam_im.astype(jnp.float32)
    mag = jnp.exp(lr * dt)
    ab_re, ab_im = mag * jnp.cos(li * dt), mag * jnp.sin(li * dt)
    den = lr * lr + li * li
    zr = ((ab_re - 1.0) * lr + ab_im * li) / den
    zi = (ab_im * lr - (ab_re - 1.0) * li) / den
    br, bi = b_re.astype(jnp.float32), b_im.astype(jnp.float32)
    bb_re = zr[..., None] * br - zi[..., None] * bi
    bb_im = zr[..., None] * bi + zi[..., None] * br
    bu_re = jnp.einsum('gnc,blgc->blgn', bb_re, ug)
    bu_im = jnp.einsum('gnc,blgc->blgn', bb_im, ug)
    x0r, x0i = x0[..., 0].astype(jnp.float32), x0[..., 1].astype(jnp.float32)
    bu_re = bu_re.at[:, 0].add(ab_re * x0r - ab_im * x0i)
    bu_im = bu_im.at[:, 0].add(ab_re * x0i + ab_im * x0r)
    a_re = jnp.broadcast_to(ab_re, bu_re.shape)
    a_im = jnp.broadcast_to(ab_im, bu_im.shape)
    _, _, xr, xi = lax.associative_scan(complex_affine_combine, (a_re, a_im, bu_re, bu_im), axis=1)
    y = jnp.einsum('gcn,blgn->blgc', c_re.astype(jnp.float32), xr) - jnp.einsum('gcn,blgn->blgc', c_im.astype(jnp.float32), xi)
    y = y.reshape(bsz, L, C_WIDTH) + d_skip.astype(jnp.float32) * uf
    z = jax.nn.gelu(y)
    out = z * jax.nn.sigmoid(z @ w_glu.astype(jnp.float32) + b_glu.astype(jnp.float32))
    return out, jnp.stack([xr[:, -1], xi[:, -1]], axis=-1)


def index_scores(iq, iw, ik):
    s = jnp.einsum('bqhd,bsd->bqhs', iq, ik).astype(jnp.float32)
    return jnp.einsum('bqhs,bqh->bqs', jax.nn.relu(s), iw.astype(jnp.float32)) * IDX_SCALE


def select_keys(score, qpos, n_sel):
    kpos = jnp.arange(score.shape[-1])
    admissible = kpos[None, :] <= qpos[:, None]
    _, idx = lax.top_k(jnp.where(admissible[None], score, -jnp.inf), n_sel)
    valid = idx <= qpos[None, :, None]
    return idx, valid


def sparse_attend(q, kg, vg, dist, valid, t5_table):
    H, E = q.shape[-2], q.shape[-1]
    bias = t5_table[t5_bucket(dist)][..., :H].astype(jnp.float32)
    s = jnp.einsum('bqhe,bqkhe->bqkh', q, kg).astype(jnp.float32) * (E ** -0.5) + bias
    s = jnp.where(valid[..., None], s, -jnp.inf)
    p = jax.nn.softmax(s, axis=2)
    return jnp.einsum('bqkh,bqkhe->bqhe', p.astype(vg.dtype), vg)


def dsa_prompt(q, k, v, iq, iw, ik, t5_table):
    bsz, L, H, E = q.shape
    n_sel = min(IDX_TOPK, L // 4)
    blk = min(D_QBLOCK, L)
    nb = L // blk
    take_rows = jax.vmap(lambda rows, idx: rows[idx])

    def blocks(a):
        return jnp.moveaxis(a.reshape((bsz, nb, blk) + a.shape[2:]), 1, 0)

    def one_block(args):
        qb, iqb, iwb, qpos = args
        idx, valid = select_keys(index_scores(iqb, iwb, ik), qpos, n_sel)
        return sparse_attend(qb, take_rows(k, idx), take_rows(v, idx), qpos[None, :, None] - idx, valid, t5_table)

    out = lax.map(one_block, (blocks(q), blocks(iq), blocks(iw), jnp.arange(L).reshape(nb, blk)))
    return jnp.moveaxis(out, 0, 1).reshape(bsz, L, H, E)


def dsa_step(q, kv_new, iq, iw, ik_new, cache_kv, cache_kidx, page_table, layer, t5_table):
    nseq, S = q.shape[0], q.shape[1]
    page = cache_kv.shape[2]
    n_past = page_table.shape[1] * page
    n_sel = min(IDX_TOPK, (n_past + S) // 4)
    past_ik = cache_kidx[layer, page_table].reshape(nseq, n_past, IDX_DIM)
    ik_all = jnp.concatenate([past_ik.astype(ik_new.dtype), ik_new], axis=1)
    qpos = n_past + jnp.arange(S)
    idx, valid = select_keys(index_scores(iq, iw, ik_all), qpos, n_sel)
    in_past = idx < n_past
    pidx = jnp.minimum(idx, n_past - 1)
    phys = jax.vmap(lambda pt, i: pt[i])(page_table, pidx // page)
    rows_past = cache_kv[layer, phys, pidx % page]
    rows_new = jax.vmap(lambda r, i: r[i])(kv_new, jnp.clip(idx - n_past, 0, S - 1))
    rows = jnp.where(in_past[..., None, None, None], rows_past.astype(kv_new.dtype), rows_new)
    return sparse_attend(q, rows[..., 0, :, :], rows[..., 1, :, :], qpos[None, :, None] - idx, valid, t5_table)


def hier_moe(x, w_group, b_group, w_expert, b_expert, w_gate, w_up, w_down):
    shp = x.shape
    xt = x.reshape(-1, shp[-1])
    g_prob = jax.nn.softmax((xt @ w_group).astype(jnp.float32) + b_group.astype(jnp.float32), axis=-1)
    g_w, g_idx = lax.top_k(g_prob, 1)
    e_logits = jnp.einsum('td,dge->tge', xt, w_expert).astype(jnp.float32) + b_expert.astype(jnp.float32)
    e_logits = jnp.take_along_axis(e_logits, g_idx[:, :, None], axis=1)[:, 0]
    e_w, e_idx = lax.top_k(jax.nn.softmax(e_logits, axis=-1), MOE_TOPK)
    e_w = e_w / jnp.sum(e_w, axis=-1, keepdims=True)
    expert_id = g_idx * MOE_GROUP_EXPERTS + e_idx
    combine = jnp.einsum('tk,tke->te', g_w * e_w, jax.nn.one_hot(expert_id, MOE_EXPERTS, dtype=jnp.float32))
    out = jnp.zeros(xt.shape, jnp.float32)
    for e in range(MOE_EXPERTS):
        hdn = jax.nn.silu(xt @ w_gate[e]) * (xt @ w_up[e])
        out = out + combine[:, e:e + 1] * (hdn @ w_down[e])
    return out.reshape(shp).astype(x.dtype)


def mixer_ab(xn, w_in, w_out, gn_gain, t5_table, pos0, a_past, b_state):
    bsz, L, _ = xn.shape
    proj = xn @ w_in
    aq, ak, av, bq, bk, bv, bg = split_cols(proj, [A_WIDTH] * 3 + [B_QK_WIDTH] * 2 + [B_V_WIDTH] * 2)
    outs, lses, new_bufs = [], [], []
    for g, (win, dil) in enumerate(A_PATTERNS):
        cols = slice(g * A_GROUP_HEADS * HEAD_DIM, (g + 1) * A_GROUP_HEADS * HEAD_DIM)
        q, k, v = (t[..., cols].reshape(bsz, L, A_GROUP_HEADS, HEAD_DIM) for t in (aq, ak, av))
        nkey = win // dil
        bias = t5_table[t5_bucket(jnp.arange(nkey + 1) * dil)][:, g * A_GROUP_HEADS:(g + 1) * A_GROUP_HEADS].astype(jnp.float32)
        kv = jnp.stack([k, v], axis=2)
        if a_past is None:
            o, lse = dilated_attn_prompt(q, k, v, dil, nkey, bias)
            new_bufs.append(kv[:, L - min(win, L):])
        else:
            buf = a_past[g]
            kv_all = jnp.concatenate([buf.astype(kv.dtype), kv], axis=1)
            o, lse = dilated_attn_step(q, kv_all[:, :, 0], kv_all[:, :, 1], dil, nkey, bias, buf.shape[1])
            new_bufs.append(kv_all[:, L:])
        outs.append(o)
        lses.append(lse)
    alpha = jax.nn.softmax(jnp.stack(lses, axis=0), axis=0)
    a_out = jnp.concatenate([o.astype(jnp.float32) * alpha[g][..., None] for g, o in enumerate(outs)], axis=2)
    a_out = a_out.reshape(bsz, L, A_WIDTH)
    pos = pos0 + jnp.arange(L)
    q = rope(bq.reshape(bsz, L, B_HEADS, B_KEY_DIM), pos)
    k = rope(bk.reshape(bsz, L, B_HEADS, B_KEY_DIM), pos)
    v = bv.reshape(bsz, L, B_HEADS, B_VAL_DIM)
    chunk = B_CHUNK if L % B_CHUNK == 0 else L
    o, s_new = retention(q, k, v, b_state, retention_log_gamma(), chunk)
    mu = jnp.mean(o, axis=-1, keepdims=True)
    var = jnp.mean((o - mu) ** 2, axis=-1, keepdims=True)
    o = (o - mu) * lax.rsqrt(var + EPS) * gn_gain.astype(jnp.float32).reshape(B_HEADS, B_VAL_DIM)
    b_out = jax.nn.silu(bg.astype(jnp.float32)) * o.reshape(bsz, L, B_V_WIDTH)
    mixed = jnp.concatenate([a_out, b_out], axis=-1).astype(xn.dtype)
    return mixed @ w_out, new_bufs, s_new


def mixer_cd(xn, w_in, w_out, c_prm, t5_table, c_state, d_cache):
    bsz, L, _ = xn.shape
    proj = xn @ w_in
    cu, dq, dk, dv, iq, iw, ik = split_cols(proj, [C_WIDTH, D_WIDTH, D_WIDTH, D_WIDTH, IDX_HEADS * IDX_DIM, IDX_HEADS, IDX_DIM])
    c_out, c_new = s5_layer(cu, *c_prm, c_state)
    q, k, v = (t.reshape(bsz, L, D_HEADS, HEAD_DIM) for t in (dq, dk, dv))
    iq = iq.reshape(bsz, L, IDX_HEADS, IDX_DIM)
    kv = jnp.stack([k, v], axis=2)
    if d_cache is None:
        d_out = dsa_prompt(q, k, v, iq, iw, ik, t5_table)
    else:
        cache_kv, cache_kidx, page_table, layer = d_cache
        d_out = dsa_step(q, kv, iq, iw, ik, cache_kv, cache_kidx, page_table, layer, t5_table)
    mixed = jnp.concatenate([c_out, d_out.reshape(bsz, L, D_WIDTH).astype(jnp.float32)], axis=-1).astype(xn.dtype)
    return mixed @ w_out, c_new, kv, ik


def run_trunk(x, p, pos0, prm, past):
    bsz = x.shape[0]
    h = x
    a_new = [[] for _ in A_PATTERNS]
    b_new, c_new, kv_new, kidx_new = [], [], [], []
    for i in range(DEPTH):
        j = i // 2
        xn = rmsnorm(h, prm['norm_mix'][i])
        if i % 2 == 0:
            if past is None:
                a_past = None
                b_state = jnp.zeros((bsz, B_HEADS, B_KEY_DIM, B_VAL_DIM), jnp.float32)
            else:
                a_past = [buf[j] for buf in past['a']]
                b_state = past['b'][j].astype(jnp.float32)
            mix, bufs, sb = mixer_ab(xn, prm['ab_w_in'][j], prm['ab_w_out'][j], prm['ab_gn_gain'][j],
                                     prm['t5_table'], pos0, a_past, b_state)
            for g in range(len(A_PATTERNS)):
                a_new[g].append(bufs[g])
            b_new.append(sb)
        else:
            if past is None:
                c_state = jnp.zeros((bsz, C_GROUPS, C_STATE, 2), jnp.float32)
                d_cache = None
            else:
                c_state = past['c'][j]
                d_cache = (past['d_kv'], past['d_kidx'], past['page_table'], j)
            c_prm = tuple(prm[name][j] for name in C_PARAM_NAMES)
            mix, sc, kv_rows, kidx_rows = mixer_cd(xn, prm['cd_w_in'][j], prm['cd_w_out'][j], c_prm,
                                                   prm['t5_table'], c_state, d_cache)
            c_new.append(sc)
            kv_new.append(kv_rows)
            kidx_new.append(kidx_rows)
        h = h + mix
        h = h + hier_moe(rmsnorm(h, prm['norm_ffn'][i]), prm['moe_w_group'][i], prm['moe_b_group'][i],
                         prm['moe_w_expert'][i], prm['moe_b_expert'][i], prm['moe_w_gate'][i],
                         prm['moe_w_up'][i], prm['moe_w_down'][i])
        gate = jax.nn.sigmoid(rmsnorm(h, prm['norm_ple'][i]) @ prm['ple_w_gate'][i])
        h = h + gate * (p[i] @ prm['ple_w_proj'][i])
    y = rmsnorm(h, prm['norm_final'])
    return (y, tuple(jnp.stack(t, axis=0) for t in a_new), jnp.stack(b_new, axis=0), jnp.stack(c_new, axis=0),
            jnp.stack(kv_new, axis=0), jnp.stack(kidx_new, axis=0))


def setup_inputs(seed: int = 0) -> dict:
    key = jax.random.key(seed)
    keys = iter(jax.random.split(key, 64))
    f32 = jnp.float32

    def normal(shape, scale):
        return jax.random.normal(next(keys), shape, f32) * scale

    def gain(shape):
        return 1.0 + normal(shape, 0.05)

    n_pages = PAST_LEN // PAGE_SIZE
    n_used = DEC_BATCH * n_pages
    n_pool = n_used + max(1, n_used // 4)
    inp = {}
    inp['x_prompt'] = normal((BATCH, SEQ, D_MODEL), 1.0)
    inp['x_sample'] = normal((DEC_BATCH, DEC_SEQ, D_MODEL), 1.0)
    inp['p_prompt'] = normal((DEPTH, BATCH, SEQ, PLE_DIM), 1.0)
    inp['p_sample'] = normal((DEPTH, DEC_BATCH, DEC_SEQ, PLE_DIM), 1.0)
    for g, (win, dil) in enumerate(A_PATTERNS):
        inp['cache_a_kv' + str(g)] = normal((N_AB, DEC_BATCH, min(win, PAST_LEN), 2, A_GROUP_HEADS, HEAD_DIM), 1.0)
    inp['state_b'] = normal((N_AB, DEC_BATCH, B_HEADS, B_KEY_DIM, B_VAL_DIM), 2.0)
    inp['state_c'] = normal((N_CD, DEC_BATCH, C_GROUPS, C_STATE, 2), 0.1)
    inp['cache_d_kv'] = normal((N_CD, n_pool, PAGE_SIZE, 2, D_HEADS, HEAD_DIM), 1.0)
    inp['cache_d_kidx'] = normal((N_CD, n_pool, PAGE_SIZE, IDX_DIM), 1.0)
    inp['page_table'] = jax.random.permutation(next(keys), n_pool)[:n_used].reshape(DEC_BATCH, n_pages).astype(jnp.int32)
    inp['t5_table'] = normal((T5_BUCKETS, T5_HEADS), 0.5)
    inp['norm_mix'] = gain((DEPTH, D_MODEL))
    inp['norm_ffn'] = gain((DEPTH, D_MODEL))
    inp['norm_ple'] = gain((DEPTH, D_MODEL))
    inp['norm_final'] = gain((D_MODEL,))
    inp['ab_w_in'] = normal((N_AB, D_MODEL, AB_IN_WIDTH), D_MODEL ** -0.5)
    inp['ab_w_out'] = normal((N_AB, AB_OUT_WIDTH, D_MODEL), AB_OUT_WIDTH ** -0.5)
    inp['ab_gn_gain'] = gain((N_AB, B_V_WIDTH))
    inp['cd_w_in'] = normal((N_CD, D_MODEL, CD_IN_WIDTH), D_MODEL ** -0.5)
    inp['cd_w_out'] = normal((N_CD, CD_OUT_WIDTH, D_MODEL), CD_OUT_WIDTH ** -0.5)
    inp['c_lambda_re'] = -0.5 * jnp.exp(normal((N_CD, C_GROUPS, C_STATE), 0.05))
    inp['c_lambda_im'] = math.pi * jnp.arange(C_STATE, dtype=f32) + normal((N_CD, C_GROUPS, C_STATE), 0.01)
    inp['c_log_dt'] = math.log(C_DT_MIN) + jax.random.uniform(next(keys), (N_CD, C_GROUPS), f32) * (math.log(C_DT_MAX) - math.log(C_DT_MIN))
    inp['c_b_re'] = normal((N_CD, C_GROUPS, C_STATE, C_GROUP_SIZE), (2 * C_GROUP_SIZE) ** -0.5)
    inp['c_b_im'] = normal((N_CD, C_GROUPS, C_STATE, C_GROUP_SIZE), (2 * C_GROUP_SIZE) ** -0.5)
    inp['c_c_re'] = normal((N_CD, C_GROUPS, C_GROUP_SIZE, C_STATE), C_STATE ** -0.5)
    inp['c_c_im'] = normal((N_CD, C_GROUPS, C_GROUP_SIZE, C_STATE), C_STATE ** -0.5)
    inp['c_d'] = normal((N_CD, C_WIDTH), 1.0)
    inp['c_w_glu'] = normal((N_CD, C_WIDTH, C_WIDTH), C_WIDTH ** -0.5)
    inp['c_b_glu'] = normal((N_CD, C_WIDTH), 0.01)
    inp['moe_w_group'] = normal((DEPTH, D_MODEL, MOE_GROUPS), D_MODEL ** -0.5)
    inp['moe_b_group'] = normal((DEPTH, MOE_GROUPS), 0.01)
    inp['moe_w_expert'] = normal((DEPTH, D_MODEL, MOE_GROUPS, MOE_GROUP_EXPERTS), D_MODEL ** -0.5)
    inp['moe_b_expert'] = normal((DEPTH, MOE_GROUPS, MOE_GROUP_EXPERTS), 0.01)
    inp['moe_w_gate'] = normal((DEPTH, MOE_EXPERTS, D_MODEL, D_EXPERT), D_MODEL ** -0.5)
    inp['moe_w_up'] = normal((DEPTH, MOE_EXPERTS, D_MODEL, D_EXPERT), D_MODEL ** -0.5)
    inp['moe_w_down'] = normal((DEPTH, MOE_EXPERTS, D_EXPERT, D_MODEL), D_EXPERT ** -0.5)
    inp['ple_w_gate'] = normal((DEPTH, D_MODEL, D_MODEL), D_MODEL ** -0.5)
    inp['ple_w_proj'] = normal((DEPTH, PLE_DIM, D_MODEL), PLE_DIM ** -0.5)
    return inp


def reference(x_prompt, x_sample, p_prompt, p_sample, cache_a_kv0, cache_a_kv1, cache_a_kv2, state_b, state_c,
              cache_d_kv, cache_d_kidx, page_table, t5_table, norm_mix, norm_ffn, norm_ple, norm_final,
              ab_w_in, ab_w_out, ab_gn_gain, cd_w_in, cd_w_out, c_lambda_re, c_lambda_im, c_log_dt,
              c_b_re, c_b_im, c_c_re, c_c_im, c_d, c_w_glu, c_b_glu, moe_w_group, moe_b_group,
              moe_w_expert, moe_b_expert, moe_w_gate, moe_w_up, moe_w_down, ple_w_gate, ple_w_proj):
    prm = dict(t5_table=t5_table, norm_mix=norm_mix, norm_ffn=norm_ffn, norm_ple=norm_ple, norm_final=norm_final,
               ab_w_in=ab_w_in, ab_w_out=ab_w_out, ab_gn_gain=ab_gn_gain, cd_w_in=cd_w_in, cd_w_out=cd_w_out,
               c_lambda_re=c_lambda_re, c_lambda_im=c_lambda_im, c_log_dt=c_log_dt, c_b_re=c_b_re, c_b_im=c_b_im,
               c_c_re=c_c_re, c_c_im=c_c_im, c_d=c_d, c_w_glu=c_w_glu, c_b_glu=c_b_glu,
               moe_w_group=moe_w_group, moe_b_group=moe_b_group, moe_w_expert=moe_w_expert,
               moe_b_expert=moe_b_expert, moe_w_gate=moe_w_gate, moe_w_up=moe_w_up, moe_w_down=moe_w_down,
               ple_w_gate=ple_w_gate, ple_w_proj=ple_w_proj)
    y_prompt, a_p, sb_p, sc_p, dkv_p, dki_p = run_trunk(x_prompt, p_prompt, 0, prm, None)
    a0_p, a1_p, a2_p = a_p
    past = dict(a=(cache_a_kv0, cache_a_kv1, cache_a_kv2), b=state_b, c=state_c,
                d_kv=cache_d_kv, d_kidx=cache_d_kidx, page_table=page_table)
    n_past = page_table.shape[1] * cache_d_kv.shape[2]
    y_sample, a_s, sb_s, sc_s, dkv_s, dki_s = run_trunk(x_sample, p_sample, n_past, prm, past)
    a0_s, a1_s, a2_s = a_s
    return (y_prompt, y_sample, a0_p, a0_s, a1_p, a1_s, a2_p, a2_s, sb_p, sb_s, sc_p, sc_s, dkv_p, dkv_s, dki_p, dki_s)
```

```python
import functools
import math

import numpy as np
import jax
import jax.numpy as jnp
from jax import lax
from jax.experimental import pallas as pl
from jax.experimental.pallas import tpu as pltpu

F32 = jnp.float32
BF16 = jnp.bfloat16
I32 = jnp.int32

EPS = 1e-6
HEAD_DIM = 64
A_PATTERNS = ((128, 1), (512, 4), (2048, 16))
A_GROUP_HEADS = 2
A_WIDTH = 384
B_HEADS = 5
B_KEY_DIM = 64
B_VAL_DIM = 128
B_QK_WIDTH = 320
B_QK_PAD = 384
B_V_WIDTH = 640
B_CHUNK = 128
ROPE_BASE = 10000.0
C_GROUPS = 32
C_GROUP_SIZE = 16
C_WIDTH = 512
C_STATE = 64
C_FLAT = C_GROUPS * C_STATE
D_HEADS = 8
D_WIDTH = 512
IDX_HEADS = 4
IDX_DIM = 64
IDX_TOPK = 256
IDX_SCALE = (IDX_HEADS * IDX_DIM) ** -0.5
T5_BUCKETS = 32
T5_MAX_DIST = 2048
MOE_GROUPS = 4
MOE_GROUP_EXPERTS = 4
MOE_EXPERTS = 16
D_EXPERT = 512
LANES = 128
INT_MIN = -(2 ** 31)
NEG_INF = float("-inf")

VMEM_LIMIT_BYTES = 52 * 1024 * 1024

DSA_TQ = 128
DSA_TK = 512
DSA_DCAP = T5_MAX_DIST + DSA_TK
DSA_TAB = DSA_DCAP + DSA_TK + 2 * LANES


def _cparams(*sem):
    return pltpu.CompilerParams(dimension_semantics=sem, vmem_limit_bytes=VMEM_LIMIT_BYTES)


def _rms(x, g):
    return x * lax.rsqrt(jnp.mean(x * x, axis=-1, keepdims=True) + EPS) * g


def _dot(a, b):
    return jnp.dot(a, b, preferred_element_type=F32)


def _dot_nt(a, b):
    return lax.dot_general(a, b, (((1,), (1,)), ((), ())), preferred_element_type=F32)


def _dot_tn(a, b):
    return lax.dot_general(a, b, (((0,), (0,)), ((), ())), preferred_element_type=F32)


def _sigmoid(x):
    return 1.0 / (1.0 + jnp.exp(-x))


def _t5_bucket(dist):
    max_exact = T5_BUCKETS // 2
    d = jnp.maximum(dist, 0)
    scaled = jnp.log(jnp.maximum(d, 1).astype(F32) / max_exact) / math.log(T5_MAX_DIST / max_exact)
    large = jnp.minimum(max_exact + (scaled * (T5_BUCKETS - max_exact)).astype(I32), T5_BUCKETS - 1)
    return jnp.where(d < max_exact, d, large)


def _norm_matmul_kernel(x_ref, g_ref, w_ref, *out_refs, segs):
    xn = _rms(x_ref[...], g_ref[...]).astype(BF16)
    for o_ref, (start, width) in zip(out_refs, segs):
        o_ref[...] = _dot(xn, w_ref[:, start:start + width]).astype(o_ref.dtype)


def norm_matmul(x, gain, w, segs, tm):
    t, d = x.shape
    n = w.shape[1]
    kern = functools.partial(_norm_matmul_kernel, segs=tuple((s, wd) for s, wd, _ in segs))
    return pl.pallas_call(
        kern, grid=(t // tm,),
        in_specs=[pl.BlockSpec((tm, d), lambda i: (i, 0)),
                  pl.BlockSpec((1, d), lambda i: (0, 0)),
                  pl.BlockSpec((d, n), lambda i: (0, 0))],
        out_specs=[pl.BlockSpec((tm, wd), lambda i: (i, 0)) for _, wd, _ in segs],
        out_shape=[jax.ShapeDtypeStruct((t, wd), dt) for _, wd, dt in segs],
        compiler_params=_cparams("parallel"), name="norm_matmul")(x, gain.reshape(1, d), w)


def _dil_prompt_kernel(q_ref, kc_ref, kp_ref, vc_ref, vp_ref, bias_ref, o_ref, lse_ref):
    n = pl.program_id(1)
    blk = q_ref.shape[0]
    q = q_ref[...]
    kw = jnp.concatenate([kp_ref[...], kc_ref[...]], axis=0)
    vw = jnp.concatenate([vp_ref[...], vc_ref[...]], axis=0)
    qi = lax.broadcasted_iota(I32, (blk, 2 * blk), 0)
    kj = lax.broadcasted_iota(I32, (blk, 2 * blk), 1)
    m = blk + qi - kj
    first_key = jnp.where(n > 0, 0, blk)
    mask = (m >= 0) & (m <= blk) & (kj >= first_key)
    outs, lses = [], []
    for h in range(A_GROUP_HEADS):
        cs = slice(h * HEAD_DIM, (h + 1) * HEAD_DIM)
        s = _dot_nt(q[:, cs].astype(BF16), kw[:, cs].astype(BF16)) * (HEAD_DIM ** -0.5) + bias_ref[h]
        s = jnp.where(mask, s, NEG_INF)
        mx = jnp.max(s, axis=1, keepdims=True)
        p = jnp.exp(s - mx)
        l = jnp.sum(p, axis=1, keepdims=True)
        outs.append(_dot((p / l).astype(BF16), vw[:, cs].astype(BF16)))
        lses.append(jnp.broadcast_to(mx + jnp.log(l), (blk, HEAD_DIM)))
    o_ref[...] = jnp.concatenate(outs, axis=1)
    lse_ref[...] = jnp.concatenate(lses, axis=1)


def dil_prompt(aqkv, bias_qk, g, dil):
    seq = aqkv.shape[0]
    blk = bias_qk.shape[1]
    nb = seq // (blk * dil)
    ncol = aqkv.shape[1] // LANES
    a = aqkv.reshape(seq // dil, dil * aqkv.shape[1])
    ng = len(A_PATTERNS)

    def cur(off):
        return pl.BlockSpec((blk, LANES), lambda r, n: (n, r * ncol + off + g))

    def prev(off):
        return pl.BlockSpec((blk, LANES), lambda r, n: (jnp.maximum(n - 1, 0), r * ncol + off + g))

    o, lse = pl.pallas_call(
        _dil_prompt_kernel, grid=(dil, nb),
        in_specs=[cur(0), cur(ng), prev(ng), cur(2 * ng), prev(2 * ng),
                  pl.BlockSpec(bias_qk.shape, lambda r, n: (0, 0, 0))],
        out_specs=[pl.BlockSpec((blk, LANES), lambda r, n: (n, r))] * 2,
        out_shape=[jax.ShapeDtypeStruct((seq // dil, dil * LANES), F32)] * 2,
        compiler_params=_cparams("parallel", "parallel"), name="dil_prompt")(a, a, a, a, a, bias_qk)
    return o.reshape(seq, LANES), lse.reshape(seq, LANES)


def _dil_step_kernel(q_ref, buf_ref, new_ref, bias_ref, o_ref, lse_ref):
    q = q_ref[0]
    rows = q.shape[0]
    head = lax.broadcasted_iota(I32, q.shape, 1) // HEAD_DIM
    qbd = jnp.concatenate([jnp.where(head == h, q, 0.0) for h in range(A_GROUP_HEADS)], axis=0).astype(BF16)
    kv = jnp.concatenate([buf_ref[0, 0], new_ref[0]], axis=0)
    k = kv[:, :LANES].astype(BF16)
    v = kv[:, LANES:].astype(BF16)
    s = _dot_nt(qbd, k) * (HEAD_DIM ** -0.5) + bias_ref[...]
    mx = jnp.max(s, axis=1, keepdims=True)
    p = jnp.exp(s - mx)
    l = jnp.sum(p, axis=1, keepdims=True)
    o = _dot((p / l).astype(BF16), v)
    lse = jnp.broadcast_to(mx + jnp.log(l), o.shape)
    o_ref[0] = jnp.where(head == 0, o[:rows], o[rows:])
    lse_ref[0] = jnp.where(head == 0, lse[:rows], lse[rows:])


def dil_step(q, buf, new, bias, layer):
    nseq, rows, _ = q.shape
    win = buf.shape[2]
    return pl.pallas_call(
        _dil_step_kernel, grid=(nseq,),
        in_specs=[pl.BlockSpec((1, rows, LANES), lambda n: (n, 0, 0)),
                  pl.BlockSpec((1, 1, win, 2 * LANES), lambda n: (layer, n, 0, 0)),
                  pl.BlockSpec((1,) + new.shape[1:], lambda n: (n, 0, 0)),
                  pl.BlockSpec(bias.shape, lambda n: (0, 0))],
        out_specs=[pl.BlockSpec((1, rows, LANES), lambda n: (n, 0, 0))] * 2,
        out_shape=[jax.ShapeDtypeStruct((nseq, rows, LANES), F32)] * 2,
        compiler_params=_cparams("parallel"), name="dil_step")(q, buf, new, bias)


def _retention_kernel(bqk_ref, cos_ref, sin_ref, bv_ref, bg_ref, decay_ref, win_ref, wend_ref, gch_ref, gain_ref,
                      s0_ref, out_ref, s_ref):
    c = pl.program_id(1)

    @pl.when(c == 0)
    def _():
        s_ref[...] = s0_ref[...]

    bqk = bqk_ref[...]
    cosv, sinv = cos_ref[...], sin_ref[...]
    w = B_QK_PAD
    q = (bqk[:, 0:w] * cosv + bqk[:, w:2 * w] * sinv) * (B_KEY_DIM ** -0.5)
    k = bqk[:, 2 * w:3 * w] * cosv + bqk[:, 3 * w:4 * w] * sinv
    q_in = q * win_ref[...]
    k_end = k * wend_ref[...]
    bv, bg, gain = bv_ref[...], bg_ref[...], gain_ref[...]
    outs = []
    for h in range(B_HEADS):
        ks = slice(h * B_KEY_DIM, (h + 1) * B_KEY_DIM)
        vs = slice(h * B_VAL_DIM, (h + 1) * B_VAL_DIM)
        vh = bv[:, vs].astype(BF16)
        state = s_ref[0, h]
        intra = _dot_nt(q[:, ks].astype(BF16), k[:, ks].astype(BF16)) * decay_ref[h]
        o = _dot(intra.astype(BF16), vh) + _dot(q_in[:, ks].astype(BF16), state.astype(BF16))
        s_ref[0, h] = gch_ref[h] * state + _dot_tn(k_end[:, ks].astype(BF16), vh)
        mu = jnp.mean(o, axis=1, keepdims=True)
        var = jnp.mean((o - mu) ** 2, axis=1, keepdims=True)
        on = (o - mu) * lax.rsqrt(var + EPS) * gain[:, vs]
        g = bg[:, vs]
        outs.append(g * _sigmoid(g) * on)
    out_ref[...] = jnp.concatenate(outs, axis=1)


def retention(bqk, cosv, sinv, bv, bg, tabs, gain, s0, chunk, nchunk):
    nseq = s0.shape[0]
    rows = bqk.shape[0]
    decay, w_in, w_end, g_chunk = tabs
    row = lambda b, c: (b * nchunk + c, 0)
    pos = lambda b, c: (c, 0)
    fix2 = lambda b, c: (0, 0)
    fix3 = lambda b, c: (0, 0, 0)
    return pl.pallas_call(
        _retention_kernel, grid=(nseq, nchunk),
        in_specs=[pl.BlockSpec((chunk, 4 * B_QK_PAD), row),
                  pl.BlockSpec((chunk, B_QK_PAD), pos), pl.BlockSpec((chunk, B_QK_PAD), pos),
                  pl.BlockSpec((chunk, B_V_WIDTH), row), pl.BlockSpec((chunk, B_V_WIDTH), row),
                  pl.BlockSpec(decay.shape, fix3), pl.BlockSpec(w_in.shape, fix2), pl.BlockSpec(w_end.shape, fix2),
                  pl.BlockSpec(g_chunk.shape, fix3), pl.BlockSpec((1, B_V_WIDTH), fix2),
                  pl.BlockSpec((1,) + s0.shape[1:], lambda b, c: (b, 0, 0, 0))],
        out_specs=[pl.BlockSpec((chunk, B_V_WIDTH), row),
                   pl.BlockSpec((1,) + s0.shape[1:], lambda b, c: (b, 0, 0, 0))],
        out_shape=[jax.ShapeDtypeStruct((rows, B_V_WIDTH), F32), jax.ShapeDtypeStruct(s0.shape, F32)],
        compiler_params=_cparams("parallel", "arbitrary"), name="retention")(
            bqk, cosv, sinv, bv, bg, decay, w_in, w_end, g_chunk, gain.reshape(1, B_V_WIDTH), s0)


def _retention_tables(chunk, pad):
    lg = np.log(1.0 - 2.0 ** (-5.0 - np.arange(B_HEADS, dtype=np.float32))).astype(np.float32)
    lg = jnp.asarray(lg)
    i = jnp.arange(pad, dtype=F32)
    diff = i[:, None] - i[None, :]
    decay = jnp.where(diff >= 0, jnp.exp(jnp.maximum(diff, 0.0)[None] * lg[:, None, None]), 0.0)
    w_end = jnp.exp((chunk - 1.0 - i)[None, :] * lg[:, None])
    w_in = jnp.exp((i + 1.0)[None, :] * lg[:, None])
    g_chunk = jnp.exp(chunk * lg)

    def widen(t):
        t = jnp.repeat(t.T, B_KEY_DIM, axis=1)
        return jnp.pad(t, ((0, 0), (0, B_QK_PAD - B_QK_WIDTH)))

    return decay, widen(w_in), widen(w_end), jnp.broadcast_to(g_chunk[:, None, None], (B_HEADS, 1, B_VAL_DIM))


def _rope_tables(pos):
    half = B_KEY_DIM // 2
    freq = ROPE_BASE ** (-jnp.arange(half, dtype=F32) / half)
    ang = pos.astype(F32)[:, None] * freq[None, :]
    cos, sin = jnp.cos(ang), jnp.sin(ang)
    cos_h = jnp.concatenate([cos, cos], axis=1)
    sin_h = jnp.concatenate([-sin, sin], axis=1)
    padw = ((0, 0), (0, B_QK_PAD - B_QK_WIDTH))
    return jnp.pad(jnp.tile(cos_h, (1, B_HEADS)), padw), jnp.pad(jnp.tile(sin_h, (1, B_HEADS)), padw)


def _ab_out_kernel(h_ref, o0_ref, o1_ref, o2_ref, l0_ref, l1_ref, l2_ref, b_ref, w_ref, out_ref):
    ls = [l0_ref[...], l1_ref[...], l2_ref[...]]
    os_ = [o0_ref[...], o1_ref[...], o2_ref[...]]
    mx = jnp.maximum(jnp.maximum(ls[0], ls[1]), ls[2])
    es = [jnp.exp(l - mx) for l in ls]
    den = es[0] + es[1] + es[2]
    acc = h_ref[...]
    for g in range(3):
        acc = acc + _dot((os_[g] * (es[g] / den)).astype(BF16), w_ref[g * LANES:(g + 1) * LANES, :])
    acc = acc + _dot(b_ref[...].astype(BF16), w_ref[A_WIDTH:, :])
    out_ref[...] = acc


def ab_out(h, os_, ls, b_out, w, tm):
    t, d = h.shape
    row = lambda i: (i, 0)
    small = pl.BlockSpec((tm, LANES), row)
    return pl.pallas_call(
        _ab_out_kernel, grid=(t // tm,),
        in_specs=[pl.BlockSpec((tm, d), row)] + [small] * 6 +
                 [pl.BlockSpec((tm, B_V_WIDTH), row), pl.BlockSpec(w.shape, lambda i: (0, 0))],
        out_specs=pl.BlockSpec((tm, d), row),
        out_shape=jax.ShapeDtypeStruct((t, d), F32),
        compiler_params=_cparams("parallel"), name="ab_out")(h, *os_, *ls, b_out, w)


def _cd_out_kernel(h_ref, c_ref, d_ref, w_ref, out_ref):
    acc = h_ref[...] + _dot(c_ref[...].astype(BF16), w_ref[:C_WIDTH, :])
    out_ref[...] = acc + _dot(d_ref[...].astype(BF16), w_ref[C_WIDTH:, :])


def cd_out(h, c_out, d_out, w, tm):
    t, d = h.shape
    row = lambda i: (i, 0)
    return pl.pallas_call(
        _cd_out_kernel, grid=(t // tm,),
        in_specs=[pl.BlockSpec((tm, d), row), pl.BlockSpec((tm, C_WIDTH), row), pl.BlockSpec((tm, D_WIDTH), row),
                  pl.BlockSpec(w.shape, lambda i: (0, 0))],
        out_specs=pl.BlockSpec((tm, d), row),
        out_shape=jax.ShapeDtypeStruct((t, d), F32),
        compiler_params=_cparams("parallel"), name="cd_out")(h, c_out, d_out, w)


def _route(logits):
    lane = lax.broadcasted_iota(I32, logits.shape, 1).astype(F32)
    gmask = lane < MOE_GROUPS
    gl = jnp.where(gmask, logits, NEG_INF)
    gmax = jnp.max(gl, axis=1, keepdims=True)
    gsum = jnp.sum(jnp.where(gmask, jnp.exp(gl - gmax), 0.0), axis=1, keepdims=True)
    g_w = 1.0 / gsum
    g_idx = jnp.min(jnp.where(gl == gmax, lane, LANES), axis=1, keepdims=True)
    lo = MOE_GROUPS + MOE_GROUP_EXPERTS * g_idx
    emask = (lane >= lo) & (lane < lo + MOE_GROUP_EXPERTS)
    el = jnp.where(emask, logits, NEG_INF)
    emax = jnp.max(el, axis=1, keepdims=True)
    eexp = jnp.where(emask, jnp.exp(el - emax), 0.0)
    p = eexp / jnp.sum(eexp, axis=1, keepdims=True)
    pm = jnp.where(emask, p, -1.0)
    p1 = jnp.max(pm, axis=1, keepdims=True)
    i1 = jnp.min(jnp.where(pm == p1, lane, LANES), axis=1, keepdims=True)
    pm2 = jnp.where(lane == i1, -1.0, pm)
    p2 = jnp.max(pm2, axis=1, keepdims=True)
    i2 = jnp.min(jnp.where(pm2 == p2, lane, LANES), axis=1, keepdims=True)
    den = p1 + p2
    return jnp.where(lane == i1, g_w * (p1 / den), jnp.where(lane == i2, g_w * (p2 / den), 0.0))


def _moe_kernel(h_ref, g_ref, wr_ref, br_ref, wg_ref, wu_ref, wd_ref, out_ref, xn_s, comb_s, acc_s):
    e = pl.program_id(1)

    @pl.when(e == 0)
    def _():
        xn = _rms(h_ref[...], g_ref[...])
        xn_s[...] = xn.astype(BF16)
        logits = jnp.dot(xn, wr_ref[...], preferred_element_type=F32, precision=lax.Precision.HIGHEST)
        comb_s[...] = _route(logits + br_ref[...])
        acc_s[...] = jnp.zeros_like(acc_s)

    xb = xn_s[...]
    hg = _dot(xb, wg_ref[0])
    hu = _dot(xb, wu_ref[0])
    hdn = (hg * _sigmoid(hg)) * hu
    y = _dot(hdn.astype(BF16), wd_ref[0])
    comb = comb_s[...]
    lane = lax.broadcasted_iota(I32, comb.shape, 1)
    cw = jnp.sum(jnp.where(lane == MOE_GROUPS + e, comb, 0.0), axis=1, keepdims=True)
    acc_s[...] += cw * y

    @pl.when(e == MOE_EXPERTS - 1)
    def _():
        out_ref[...] = h_ref[...] + acc_s[...]


def moe(h, gain, wr, br, wg, wu, wd, tm):
    t, d = h.shape
    row = lambda i, e: (i, 0)
    fix = lambda i, e: (0, 0)
    ex = lambda i, e: (e, 0, 0)
    return pl.pallas_call(
        _moe_kernel, grid=(t // tm, MOE_EXPERTS),
        in_specs=[pl.BlockSpec((tm, d), row), pl.BlockSpec((1, d), fix),
                  pl.BlockSpec((d, LANES), fix), pl.BlockSpec((1, LANES), fix),
                  pl.BlockSpec((1, d, D_EXPERT), ex), pl.BlockSpec((1, d, D_EXPERT), ex),
                  pl.BlockSpec((1, D_EXPERT, d), ex)],
        out_specs=pl.BlockSpec((tm, d), row),
        out_shape=jax.ShapeDtypeStruct((t, d), F32),
        scratch_shapes=[pltpu.VMEM((tm, d), BF16), pltpu.VMEM((tm, LANES), F32), pltpu.VMEM((tm, d), F32)],
        compiler_params=_cparams("parallel", "arbitrary"), name="moe")(
            h, gain.reshape(1, d), wr, br, wg, wu, wd)


def _ple_kernel(h_ref, p_ref, g_ref, wg_ref, wp_ref, *rest, final):
    h = h_ref[...]
    gate = _sigmoid(_dot(_rms(h, g_ref[...]).astype(BF16), wg_ref[...]))
    hn = h + gate * _dot(p_ref[...].astype(BF16), wp_ref[...])
    if final:
        gf_ref, out_ref = rest
        out_ref[...] = _rms(hn, gf_ref[...])
    else:
        rest[0][...] = hn


def ple(h, p, gain, wg, wp, gain_final, tm):
    t, d = h.shape
    row = lambda i: (i, 0)
    fix = lambda i: (0, 0)
    final = gain_final is not None
    in_specs = [pl.BlockSpec((tm, d), row), pl.BlockSpec((tm, p.shape[1]), row), pl.BlockSpec((1, d), fix),
                pl.BlockSpec(wg.shape, fix), pl.BlockSpec(wp.shape, fix)]
    args = [h, p, gain.reshape(1, d), wg, wp]
    if final:
        in_specs.append(pl.BlockSpec((1, d), fix))
        args.append(gain_final.reshape(1, d))
    return pl.pallas_call(
        functools.partial(_ple_kernel, final=final), grid=(t // tm,),
        in_specs=in_specs, out_specs=pl.BlockSpec((tm, d), row),
        out_shape=jax.ShapeDtypeStruct((t, d), F32),
        compiler_params=_cparams("parallel"), name="ple")(*args)


def _s5_kernel(u_ref, a_ref, bre_ref, bim_ref, cre_ref, cim_ref, d_ref, wglu_ref, bglu_ref, x0r_ref, x0i_ref,
               out_ref, xr_ref, xi_ref, bur_s, bui_s, *, nbatch, steps):
    c = pl.program_id(0)

    @pl.when(c == 0)
    def _():
        xr_ref[...] = x0r_ref[...]
        xi_ref[...] = x0i_ref[...]

    u = u_ref[...]
    ub = u.astype(BF16)
    bur_s[...] = _dot(ub, bre_ref[...])
    bui_s[...] = _dot(ub, bim_ref[...])
    ar = a_ref[0:1, :]
    ai = a_ref[1:2, :]

    def step(t, carry):
        xr, xi = carry
        r0 = pl.multiple_of(t * nbatch, nbatch)
        nr = ar * xr - ai * xi + bur_s[pl.ds(r0, nbatch), :]
        ni = ar * xi + ai * xr + bui_s[pl.ds(r0, nbatch), :]
        bur_s[pl.ds(r0, nbatch), :] = nr
        bui_s[pl.ds(r0, nbatch), :] = ni
        return nr, ni

    xr, xi = lax.fori_loop(0, steps, step, (xr_ref[...], xi_ref[...]), unroll=min(steps, 8))
    xr_ref[...] = xr
    xi_ref[...] = xi
    y = _dot(bur_s[...].astype(BF16), cre_ref[...]) - _dot(bui_s[...].astype(BF16), cim_ref[...]) + d_ref[...] * u
    z = 0.5 * y * (1.0 + jnp.tanh(math.sqrt(2.0 / math.pi) * (y + 0.044715 * (y * y * y))))
    out_ref[...] = z * _sigmoid(_dot(z.astype(BF16), wglu_ref[...]) + bglu_ref[...])


def s5(u, prm, x0r, x0i, nbatch, steps):
    rows = u.shape[0]
    a, bre, bim, cre, cim, dskip, wglu, bglu = prm
    nchunk = rows // (nbatch * steps)
    tr = nbatch * steps
    fix = lambda c: (0, 0)
    full = lambda arr: pl.BlockSpec(arr.shape, fix)
    return pl.pallas_call(
        functools.partial(_s5_kernel, nbatch=nbatch, steps=steps), grid=(nchunk,),
        in_specs=[pl.BlockSpec((tr, C_WIDTH), lambda c: (c, 0))] +
                 [full(t) for t in (a, bre, bim, cre, cim, dskip, wglu, bglu, x0r, x0i)],
        out_specs=[pl.BlockSpec((tr, C_WIDTH), lambda c: (c, 0)), full(x0r), full(x0i)],
        out_shape=[jax.ShapeDtypeStruct((rows, C_WIDTH), F32), jax.ShapeDtypeStruct(x0r.shape, F32),
                   jax.ShapeDtypeStruct(x0i.shape, F32)],
        scratch_shapes=[pltpu.VMEM((tr, C_FLAT), F32), pltpu.VMEM((tr, C_FLAT), F32)],
        compiler_params=_cparams("arbitrary"), name="s5")(u, a, bre, bim, cre, cim, dskip, wglu, bglu, x0r, x0i)


def _s5_prep(lam_re, lam_im, log_dt, b_re, b_im, c_re, c_im, d_skip, w_glu, b_glu):
    dt = jnp.exp(log_dt)[:, None]
    mag = jnp.exp(lam_re * dt)
    ab_re, ab_im = mag * jnp.cos(lam_im * dt), mag * jnp.sin(lam_im * dt)
    den = lam_re * lam_re + lam_im * lam_im
    zr = ((ab_re - 1.0) * lam_re + ab_im * lam_im) / den
    zi = (ab_im * lam_re - (ab_re - 1.0) * lam_im) / den
    bb_re = zr[..., None] * b_re - zi[..., None] * b_im
    bb_im = zr[..., None] * b_im + zi[..., None] * b_re
    eye = jnp.eye(C_GROUPS, dtype=F32)
    b_dense = lambda bb: jnp.einsum('gnc,gh->gchn', bb, eye).reshape(C_WIDTH, C_FLAT).astype(BF16)
    c_dense = lambda cc: jnp.einsum('gcn,gh->gnhc', cc, eye).reshape(C_FLAT, C_WIDTH).astype(BF16)
    a = jnp.stack([ab_re.reshape(C_FLAT), ab_im.reshape(C_FLAT)], axis=0)
    return (a, b_dense(bb_re), b_dense(bb_im), c_dense(c_re), c_dense(c_im), d_skip.reshape(1, C_WIDTH),
            w_glu.astype(BF16), b_glu.reshape(1, C_WIDTH))


def _sortable(x):
    bits = lax.bitcast_convert_type(x, I32)
    return bits ^ ((bits >> 31) & jnp.int32(0x7FFFFFFF))


def _kth_largest(keys_ref, ntiles, tw, n_sel):
    rows = keys_ref.shape[0]

    def count_ge(cand):
        def body(t, cv):
            off = pl.multiple_of(t * tw, LANES)
            kk = keys_ref[:, pl.ds(off, tw)]
            for j in range(tw // LANES):
                cv = cv + jnp.where(kk[:, j * LANES:(j + 1) * LANES] >= cand, 1.0, 0.0)
            return cv
        cv = lax.fori_loop(0, ntiles, body, jnp.zeros((rows, LANES), F32))
        return jnp.sum(cv, axis=1, keepdims=True)

    def bit(i, thr):
        cand = thr + lax.shift_left(jnp.int32(1), 31 - i)
        return jnp.where(count_ge(cand) >= n_sel, cand, thr)

    thr = lax.fori_loop(0, 32, bit, jnp.full((rows, 1), INT_MIN, I32))
    rem = jnp.where(thr == INT_MIN, 0.0, n_sel - count_ge(thr + 1))
    return thr, rem


def _select(key, thr, rem, cnt, tri):
    eq = key == thr
    pc = _dot(jnp.where(eq, 1.0, 0.0).astype(BF16), tri)
    sel = (key > thr) | (eq & (cnt + pc <= rem))
    return sel, cnt + pc[:, -1:]


def _dsa_prompt_kernel(q_ref, iq_ref, misc_ref, ikt_ref, kt_ref, v_ref, tbr_ref, tri_ref, o_ref,
                       keys_s, thr_s, rem_s, cnt_s, m_s, l_s, acc_s, *, n_sel):
    tq, tk = DSA_TQ, DSA_TK
    qb = pl.program_id(0)
    kt = pl.program_id(1)
    kl = (qb * tq + tq - 1) // tk

    @pl.when(kt == 0)
    def _():
        iq = iq_ref[...]
        iw = misc_ref[:, IDX_DIM:IDX_DIM + IDX_HEADS]
        qpos = qb * tq + lax.broadcasted_iota(I32, (tq, tk), 0)

        def scores(t, carry):
            off = pl.multiple_of(t * tk, tk)
            ikt = ikt_ref[:, pl.ds(off, tk)]
            sc = jnp.zeros((tq, tk), F32)
            for h in range(IDX_HEADS):
                s = _dot(iq[:, h * IDX_DIM:(h + 1) * IDX_DIM], ikt)
                sc = sc + jnp.maximum(s, 0.0) * iw[:, h:h + 1]
            kpos = off + lax.broadcasted_iota(I32, (tq, tk), 1)
            keys_s[:, pl.ds(off, tk)] = jnp.where(kpos <= qpos, _sortable(sc * IDX_SCALE), INT_MIN)
            return carry

        lax.fori_loop(0, kl + 1, scores, 0)
        thr, rem = _kth_largest(keys_s, kl + 1, tk, n_sel)
        thr_s[...] = thr
        rem_s[...] = rem
        cnt_s[...] = jnp.zeros_like(cnt_s)
        m_s[...] = jnp.full_like(m_s, NEG_INF)
        l_s[...] = jnp.zeros_like(l_s)
        acc_s[...] = jnp.zeros_like(acc_s)

    @pl.when(kt <= kl)
    def _():
        off = pl.multiple_of(kt * tk, tk)
        sel, cnt = _select(keys_s[:, pl.ds(off, tk)], thr_s[...], rem_s[...], cnt_s[...], tri_ref[...])
        cnt_s[...] = cnt
        d0 = jnp.minimum(qb * tq - kt * tk, DSA_DCAP)
        start = pl.multiple_of(DSA_DCAP - d0, LANES)
        wtab = tbr_ref[:, pl.ds(start, tk + 2 * LANES)]
        v = v_ref[...]
        for h in range(D_HEADS):
            cs = slice(h * HEAD_DIM, (h + 1) * HEAD_DIM)
            wb = jnp.broadcast_to(wtab[h:h + 1, :], (tq, tk + 2 * LANES))
            bias = pltpu.roll(wb, 1, 1, stride=1, stride_axis=0)[:, LANES:LANES + tk]
            s = _dot(q_ref[:, cs], kt_ref[cs, :]) * (HEAD_DIM ** -0.5) + bias
            s = jnp.where(sel, s, NEG_INF)
            m_old = m_s[h]
            m_new = jnp.maximum(m_old, jnp.max(s, axis=1, keepdims=True))
            m_safe = jnp.where(m_new == NEG_INF, 0.0, m_new)
            p = jnp.exp(s - m_safe)
            alpha = jnp.exp(m_old - m_safe)
            l_s[h] = alpha * l_s[h] + jnp.sum(p, axis=1, keepdims=True)
            acc_s[:, cs] = alpha * acc_s[:, cs] + _dot(p.astype(BF16), v[:, cs])
            m_s[h] = m_new

    @pl.when(kt == kl)
    def _():
        o_ref[...] = jnp.concatenate(
            [acc_s[:, h * HEAD_DIM:(h + 1) * HEAD_DIM] / l_s[h] for h in range(D_HEADS)], axis=1)


def dsa_prompt(q, iq, misc, ikt, kt_, v, tbr, tri, n_sel):
    seq = q.shape[0]
    tq, tk = DSA_TQ, DSA_TK
    nb, nkt = seq // tq, seq // tk
    last = lambda qb: (qb * tq + tq - 1) // tk
    qrow = lambda qb, kt: (qb, 0)
    fix = lambda qb, kt: (0, 0)
    return pl.pallas_call(
        functools.partial(_dsa_prompt_kernel, n_sel=n_sel), grid=(nb, nkt),
        in_specs=[pl.BlockSpec((tq, D_WIDTH), qrow), pl.BlockSpec((tq, IDX_HEADS * IDX_DIM), qrow),
                  pl.BlockSpec((tq, LANES), qrow), pl.BlockSpec(ikt.shape, fix),
                  pl.BlockSpec((D_WIDTH, tk), lambda qb, kt: (0, jnp.minimum(kt, last(qb)))),
                  pl.BlockSpec((tk, D_WIDTH), lambda qb, kt: (jnp.minimum(kt, last(qb)), 0)),
                  pl.BlockSpec(tbr.shape, fix), pl.BlockSpec(tri.shape, fix)],
        out_specs=pl.BlockSpec((tq, D_WIDTH), qrow),
        out_shape=jax.ShapeDtypeStruct((seq, D_WIDTH), F32),
        scratch_shapes=[pltpu.VMEM((tq, seq), I32), pltpu.VMEM((tq, 1), I32), pltpu.VMEM((tq, 1), F32),
                        pltpu.VMEM((tq, 1), F32), pltpu.VMEM((D_HEADS, tq, 1), F32),
                        pltpu.VMEM((D_HEADS, tq, 1), F32), pltpu.VMEM((tq, D_WIDTH), F32)],
        compiler_params=_cparams("parallel", "arbitrary"), name="dsa_prompt")(q, iq, misc, ikt, kt_, v, tbr, tri)


def _dsa_step_keys_kernel(pt_ref, iq_ref, iw_ref, page_ref, new_ref, keys_ref, *, n_new):
    p = pl.program_id(1)
    npage = pl.num_programs(1)
    psz = page_ref.shape[2]
    rows = keys_ref.shape[1]

    def score(kmat):
        r = jnp.maximum(_dot_nt(iq_ref[0], kmat.astype(BF16)), 0.0) * iw_ref[0]
        sc = r[0:rows]
        for h in range(1, IDX_HEADS):
            sc = sc + r[h * rows:(h + 1) * rows]
        return _sortable(sc * IDX_SCALE)

    keys_ref[0, :, pl.ds(pl.multiple_of(p * psz, psz), psz)] = score(page_ref[0, 0])

    @pl.when(p == 0)
    def _():
        s_i = lax.broadcasted_iota(I32, (rows, psz), 0)
        j_i = lax.broadcasted_iota(I32, (rows, psz), 1)
        ok = (j_i <= s_i) & (j_i < n_new)
        keys_ref[0, :, pl.ds(pl.multiple_of(npage * psz, psz), psz)] = jnp.where(ok, score(new_ref[0]), INT_MIN)


def dsa_step_keys(page_table, iqm, iwm, cache_kidx, ik_new, layer, n_new):
    nseq, npage = page_table.shape
    psz = cache_kidx.shape[2]
    rows = iqm.shape[1] // IDX_HEADS
    ncol = (npage + 1) * psz
    seqb = lambda n, p, pt: (n, 0, 0)
    grid_spec = pltpu.PrefetchScalarGridSpec(
        num_scalar_prefetch=1, grid=(nseq, npage),
        in_specs=[pl.BlockSpec((1,) + iqm.shape[1:], seqb), pl.BlockSpec((1,) + iwm.shape[1:], seqb),
                  pl.BlockSpec((1, 1, psz, IDX_DIM), lambda n, p, pt: (layer, pt[n * npage + p], 0, 0)),
                  pl.BlockSpec((1,) + ik_new.shape[1:], seqb)],
        out_specs=pl.BlockSpec((1, rows, ncol), seqb))
    return pl.pallas_call(
        functools.partial(_dsa_step_keys_kernel, n_new=n_new), grid_spec=grid_spec,
        out_shape=jax.ShapeDtypeStruct((nseq, rows, ncol), I32),
        compiler_params=_cparams("parallel", "arbitrary"), name="dsa_step_keys")(
            page_table.reshape(-1), iqm, iwm, cache_kidx, ik_new)


def _kth_kernel(keys_ref, thr_ref, rem_ref, *, tw, n_sel):
    thr, rem = _kth_largest(keys_ref, keys_ref.shape[1] // tw, tw, n_sel)
    thr_ref[...] = jnp.broadcast_to(thr, thr_ref.shape)
    rem_ref[...] = jnp.broadcast_to(rem, rem_ref.shape)


def kth_rows(keys, tw, n_sel, tr):
    rows, ncol = keys.shape
    return pl.pallas_call(
        functools.partial(_kth_kernel, tw=tw, n_sel=n_sel), grid=(rows // tr,),
        in_specs=[pl.BlockSpec((tr, ncol), lambda i: (i, 0))],
        out_specs=[pl.BlockSpec((tr, LANES), lambda i: (i, 0))] * 2,
        out_shape=[jax.ShapeDtypeStruct((rows, LANES), I32), jax.ShapeDtypeStruct((rows, LANES), F32)],
        compiler_params=_cparams("parallel"), name="kth_rows")(keys)


def _dsa_step_attn_kernel(pt_ref, q_ref, keys_ref, thr_ref, rem_ref, bias_ref, page_ref, new_ref, tri_ref, o_ref,
                          cnt_s, m_s, l_s, acc_s):
    p = pl.program_id(1)
    npage = pl.num_programs(1) - 1
    rows = q_ref.shape[1]

    @pl.when(p == 0)
    def _():
        cnt_s[...] = jnp.zeros_like(cnt_s)
        m_s[...] = jnp.full_like(m_s, NEG_INF)
        l_s[...] = jnp.zeros_like(l_s)
        acc_s[...] = jnp.zeros_like(acc_s)

    def process(kv):
        q = q_ref[0]
        head = lax.broadcasted_iota(I32, q.shape, 1) // HEAD_DIM
        qbd = jnp.concatenate([jnp.where(head == h, q, jnp.zeros_like(q)) for h in range(D_HEADS)], axis=0)
        sel, cnt = _select(keys_ref[0], thr_ref[0][:, 0:1], rem_ref[0][:, 0:1], cnt_s[...], tri_ref[...])
        cnt_s[...] = cnt
        neg = jnp.concatenate([jnp.where(sel, 0.0, NEG_INF)] * D_HEADS, axis=0)
        s = _dot_nt(qbd, kv[:, :D_WIDTH].astype(BF16)) * (HEAD_DIM ** -0.5) + bias_ref[...] + neg
        m_old = m_s[...]
        m_new = jnp.maximum(m_old, jnp.max(s, axis=1, keepdims=True))
        m_safe = jnp.where(m_new == NEG_INF, 0.0, m_new)
        pr = jnp.exp(s - m_safe)
        alpha = jnp.exp(m_old - m_safe)
        l_s[...] = alpha * l_s[...] + jnp.sum(pr, axis=1, keepdims=True)
        acc_s[...] = alpha * acc_s[...] + _dot(pr.astype(BF16), kv[:, D_WIDTH:].astype(BF16))
        m_s[...] = m_new

    @pl.when(p < npage)
    def _():
        process(page_ref[0, 0])

    @pl.when(p == npage)
    def _():
        process(new_ref[0])
        accn = acc_s[...] / l_s[...]
        head = lax.broadcasted_iota(I32, (rows, D_WIDTH), 1) // HEAD_DIM
        out = jnp.zeros((rows, D_WIDTH), F32)
        for h in range(D_HEADS):
            out = out + jnp.where(head == h, accn[h * rows:(h + 1) * rows], 0.0)
        o_ref[0] = out


def dsa_step_attn(page_table, q, keys, thr, rem, bias, cache_kv, kv_new, tri, layer):
    nseq, npage = page_table.shape
    psz = cache_kv.shape[2]
    rows = q.shape[1]
    seqb = lambda n, p, pt: (n, 0, 0)
    grid_spec = pltpu.PrefetchScalarGridSpec(
        num_scalar_prefetch=1, grid=(nseq, npage + 1),
        in_specs=[pl.BlockSpec((1, rows, D_WIDTH), seqb),
                  pl.BlockSpec((1, rows, psz), lambda n, p, pt: (n, 0, p)),
                  pl.BlockSpec((1, rows, LANES), seqb), pl.BlockSpec((1, rows, LANES), seqb),
                  pl.BlockSpec((D_HEADS * rows, psz), lambda n, p, pt: (0, p)),
                  pl.BlockSpec((1, 1, psz, 2 * D_WIDTH),
                               lambda n, p, pt: (layer, pt[n * npage + jnp.minimum(p, npage - 1)], 0, 0)),
                  pl.BlockSpec((1,) + kv_new.shape[1:], seqb),
                  pl.BlockSpec(tri.shape, lambda n, p, pt: (0, 0))],
        out_specs=pl.BlockSpec((1, rows, D_WIDTH), seqb),
        scratch_shapes=[pltpu.VMEM((rows, 1), F32), pltpu.VMEM((D_HEADS * rows, 1), F32),
                        pltpu.VMEM((D_HEADS * rows, 1), F32), pltpu.VMEM((D_HEADS * rows, D_WIDTH), F32)])
    return pl.pallas_call(
        _dsa_step_attn_kernel, grid_spec=grid_spec,
        out_shape=jax.ShapeDtypeStruct((nseq, rows, D_WIDTH), F32),
        compiler_params=_cparams("parallel", "arbitrary"), name="dsa_step_attn")(
            page_table.reshape(-1), q, keys, thr, rem, bias, cache_kv, kv_new, tri)


def _prep_ab_w(w):
    d = w.shape[0]
    aq, ak, av, bq, bk, bv, bg = _split_cols(w, [A_WIDTH] * 3 + [B_QK_WIDTH] * 2 + [B_V_WIDTH] * 2)
    half = B_KEY_DIM // 2

    def rot(m):
        return m.reshape(d, B_HEADS, 2, half)[:, :, ::-1, :].reshape(d, B_QK_WIDTH)

    pad = jnp.zeros((d, B_QK_PAD - B_QK_WIDTH), w.dtype)
    cols = [aq, ak, av, bq, pad, rot(bq), pad, bk, pad, rot(bk), pad, bv, bg]
    return jnp.concatenate(cols, axis=1).astype(BF16)


AB_SEGS = ((0, 3 * A_WIDTH, F32), (3 * A_WIDTH, 4 * B_QK_PAD, F32),
           (3 * A_WIDTH + 4 * B_QK_PAD, B_V_WIDTH, F32), (3 * A_WIDTH + 4 * B_QK_PAD + B_V_WIDTH, B_V_WIDTH, F32))


def _split_cols(x, widths):
    parts, start = [], 0
    for wd in widths:
        parts.append(x[..., start:start + wd])
        start += wd
    return parts


def _prep_cd_w(w):
    d = w.shape[0]
    cu, dq, dk, dv, iq, iw, ik = _split_cols(w, [C_WIDTH, D_WIDTH, D_WIDTH, D_WIDTH, IDX_HEADS * IDX_DIM, IDX_HEADS, IDX_DIM])
    pad = jnp.zeros((d, LANES - IDX_DIM - IDX_HEADS), w.dtype)
    return jnp.concatenate([cu, dq, dk, dv, iq, ik, iw, pad], axis=1).astype(BF16)


CD_SEGS = ((0, C_WIDTH, F32), (C_WIDTH, D_WIDTH, BF16), (C_WIDTH + D_WIDTH, 2 * D_WIDTH, F32),
           (C_WIDTH + 3 * D_WIDTH, IDX_HEADS * IDX_DIM, BF16), (C_WIDTH + 3 * D_WIDTH + IDX_HEADS * IDX_DIM, LANES, F32))


def _dil_prompt_bias(t5_table, g, dil, nkey):
    bias = t5_table[_t5_bucket(jnp.arange(nkey + 1) * dil)][:, g * A_GROUP_HEADS:(g + 1) * A_GROUP_HEADS].astype(F32)
    qi = np.arange(nkey)[:, None]
    kj = np.arange(2 * nkey)[None, :]
    m = np.clip(nkey + qi - kj, 0, nkey)
    return jnp.moveaxis(bias[m], -1, 0)


def _dil_step_bias(t5_table, g, dil, nkey, n_buf, n_new, rows, pad_new):
    bias = t5_table[_t5_bucket(jnp.arange(nkey + 1) * dil)][:, g * A_GROUP_HEADS:(g + 1) * A_GROUP_HEADS].astype(F32)
    s = np.arange(rows)[:, None]
    j = np.arange(n_buf + pad_new)[None, :]
    delta = n_buf + s - j
    ok = (delta >= 0) & (delta % dil == 0) & (delta // dil <= nkey) & (j < n_buf + n_new) & (s < n_new)
    ok = ok | ((s >= n_new) & (j == 0))
    m = np.clip(delta // dil, 0, nkey)
    full = jnp.where(jnp.asarray(ok)[None], jnp.moveaxis(bias[m], -1, 0), NEG_INF)
    return full.reshape(A_GROUP_HEADS * rows, n_buf + pad_new)


def _dsa_prompt_table(t5_table):
    d = (DSA_DCAP + LANES - 1) - jnp.arange(DSA_TAB)
    return t5_table[_t5_bucket(d)][:, :D_HEADS].astype(F32).T


def _tri(n):
    r = np.arange(n)
    return jnp.asarray((r[:, None] <= r[None, :]).astype(np.float32)).astype(BF16)


def _ffn_and_ple(h, p, i, prm, tm, final):
    h = moe(h, prm['norm_ffn'][i], prm['moe_wr'][i], prm['moe_br'][i], prm['moe_wg'][i], prm['moe_wu'][i],
            prm['moe_wd'][i], min(2 * tm, h.shape[0]))
    return ple(h, p, prm['norm_ple'][i], prm['ple_wg'][i], prm['ple_wp'][i], prm['norm_final'] if final else None, tm)


def _layer_ab_prompt(h, prm, j, tm):
    seq = h.shape[0]
    aqkv, bqk, bv, bg = norm_matmul(h, prm['norm_mix'][2 * j], prm['ab_w'][j], AB_SEGS, tm)
    os_, ls, bufs = [], [], []
    for g, (win, dil) in enumerate(A_PATTERNS):
        nkey = win // dil
        assert seq % (dil * nkey) == 0
        o, lse = dil_prompt(aqkv, _dil_prompt_bias(prm['t5_table'], g, dil, nkey), g, dil)
        os_.append(o)
        ls.append(lse)
        kcols = aqkv[seq - min(win, seq):, A_WIDTH + g * LANES:A_WIDTH + (g + 1) * LANES]
        vcols = aqkv[seq - min(win, seq):, 2 * A_WIDTH + g * LANES:2 * A_WIDTH + (g + 1) * LANES]
        bufs.append(jnp.stack([kcols, vcols], axis=1).reshape(1, -1, 2, A_GROUP_HEADS, HEAD_DIM))
    chunk = B_CHUNK if seq % B_CHUNK == 0 else seq
    cosv, sinv = _rope_tables(jnp.arange(seq))
    s0 = jnp.zeros((1, B_HEADS, B_KEY_DIM, B_VAL_DIM), F32)
    b_out, s_new = retention(bqk, cosv, sinv, bv, bg, _retention_tables(chunk, chunk), prm['ab_gn_gain'][j], s0,
                             chunk, seq // chunk)
    h = ab_out(h, os_, ls, b_out, prm['ab_w_out'][j], tm)
    return h, bufs, s_new


def _layer_ab_sample(h, prm, j, past, nseq, n_new, pos0):
    rows = 8
    aqkv, bqk, bv, bg = norm_matmul(h, prm['norm_mix'][2 * j], prm['ab_w'][j], AB_SEGS, h.shape[0])
    a3 = aqkv.reshape(nseq, n_new, 3 * A_WIDTH)
    pad_rows = lambda x, r: jnp.pad(x, ((0, 0), (0, r - x.shape[1]), (0, 0)))
    os_, ls, bufs = [], [], []
    for g, (win, dil) in enumerate(A_PATTERNS):
        nkey = win // dil
        buf = past['a'][g]
        n_buf = buf.shape[2]
        cs = lambda part: slice(part * A_WIDTH + g * LANES, part * A_WIDTH + (g + 1) * LANES)
        q = pad_rows(a3[:, :, cs(0)], rows)
        kv_new = jnp.concatenate([a3[:, :, cs(1)], a3[:, :, cs(2)]], axis=2)
        bias = _dil_step_bias(prm['t5_table'], g, dil, nkey, n_buf, n_new, rows, LANES)
        o, lse = dil_step(q, buf.reshape(buf.shape[0], nseq, n_buf, 2 * LANES), pad_rows(kv_new, LANES), bias, j)
        os_.append(o[:, :n_new].reshape(nseq * n_new, LANES))
        ls.append(lse[:, :n_new].reshape(nseq * n_new, LANES))
        kv_all = jnp.concatenate([buf[j].reshape(nseq, n_buf, 2 * LANES), kv_new], axis=1)
        bufs.append(kv_all[:, n_new:].reshape(nseq, n_buf, 2, A_GROUP_HEADS, HEAD_DIM))
    chunk = B_CHUNK if n_new % B_CHUNK == 0 else n_new
    assert chunk == n_new
    cpad = 16
    cosv, sinv = _rope_tables(pos0 + jnp.arange(cpad))
    padc = lambda x: jnp.pad(x.reshape(nseq, n_new, -1), ((0, 0), (0, cpad - n_new), (0, 0))).reshape(nseq * cpad, -1)
    b_out, s_new = retention(padc(bqk), cosv, sinv, padc(bv), padc(bg), _retention_tables(chunk, cpad),
                             prm['ab_gn_gain'][j], past['b'][j].astype(F32), cpad, 1)
    b_out = b_out.reshape(nseq, cpad, -1)[:, :n_new].reshape(nseq * n_new, -1)
    h = ab_out(h, os_, ls, b_out, prm['ab_w_out'][j], h.shape[0])
    return h, bufs, s_new


def _cd_project(h, prm, j, tm):
    return norm_matmul(h, prm['norm_mix'][2 * j + 1], prm['cd_w'][j], CD_SEGS, tm)


def _layer_cd_prompt(h, prm, j, tm):
    seq = h.shape[0]
    cu, dq, kv, iq, misc = _cd_project(h, prm, j, tm)
    x0 = jnp.zeros((1, C_FLAT), F32)
    c_out, xr, xi = s5(cu, prm['s5'][j], x0, x0, 1, min(256, seq))
    sc = jnp.stack([xr.reshape(C_GROUPS, C_STATE), xi.reshape(C_GROUPS, C_STATE)], axis=-1)[None]
    n_sel = min(IDX_TOPK, seq // 4)
    kb = kv[:, :D_WIDTH].astype(BF16).T
    vb = kv[:, D_WIDTH:].astype(BF16)
    ikt = misc[:, :IDX_DIM].astype(BF16).T
    d_out = dsa_prompt(dq, iq, misc, ikt, kb, vb, _dsa_prompt_table(prm['t5_table']), _tri(DSA_TK), n_sel)
    h = cd_out(h, c_out, d_out, prm['cd_w_out'][j], tm)
    return h, sc, kv, misc[:, :IDX_DIM]


def _layer_cd_sample(h, prm, j, past, nseq, n_new):
    rows = 8
    cu, dq, kv, iq, misc = _cd_project(h, prm, j, h.shape[0])
    tmaj = lambda x: x.reshape(nseq, n_new, -1).transpose(1, 0, 2).reshape(nseq * n_new, -1)
    x0 = past['c'][j].astype(F32)
    c_out, xr, xi = s5(tmaj(cu), prm['s5'][j], x0[..., 0].reshape(nseq, C_FLAT), x0[..., 1].reshape(nseq, C_FLAT),
                       nseq, n_new)
    c_out = c_out.reshape(n_new, nseq, -1).transpose(1, 0, 2).reshape(nseq * n_new, -1)
    sc = jnp.stack([xr.reshape(nseq, C_GROUPS, C_STATE), xi.reshape(nseq, C_GROUPS, C_STATE)], axis=-1)

    page_table = past['page_table']
    cache_kv, cache_kidx = past['d_kv'], past['d_kidx']
    npage = page_table.shape[1]
    psz = cache_kv.shape[2]
    n_past = npage * psz
    n_sel = min(IDX_TOPK, (n_past + n_new) // 4)
    pad_rows = lambda x, r: jnp.pad(x, ((0, 0), (0, r - x.shape[1]), (0, 0)))
    ik_new = misc[:, :IDX_DIM]
    iqm = pad_rows(iq.reshape(nseq, n_new, IDX_HEADS * IDX_DIM), rows).reshape(nseq, rows, IDX_HEADS, IDX_DIM)
    iqm = iqm.transpose(0, 2, 1, 3).reshape(nseq, IDX_HEADS * rows, IDX_DIM)
    iwm = pad_rows(misc[:, IDX_DIM:IDX_DIM + IDX_HEADS].reshape(nseq, n_new, IDX_HEADS), rows)
    iwm = jnp.broadcast_to(iwm.transpose(0, 2, 1).reshape(nseq, IDX_HEADS * rows, 1), (nseq, IDX_HEADS * rows, LANES))
    keys = dsa_step_keys(page_table, iqm, iwm, cache_kidx, pad_rows(ik_new.reshape(nseq, n_new, IDX_DIM), psz), j, n_new)
    ncol = keys.shape[2]
    thr, rem = kth_rows(keys.reshape(nseq * rows, ncol), ncol // 3 if (ncol // LANES) % 3 == 0 else LANES, n_sel,
                        min(LANES, nseq * rows))
    qpos = n_past + jnp.minimum(jnp.arange(rows), n_new - 1)
    dist = qpos[:, None] - jnp.arange(ncol)[None, :]
    bias = prm['t5_table'][_t5_bucket(dist)][..., :D_HEADS].astype(F32)
    bias = bias.transpose(2, 0, 1).reshape(D_HEADS * rows, ncol)
    q = pad_rows(dq.reshape(nseq, n_new, D_WIDTH), rows)
    kv_new = pad_rows(kv.reshape(nseq, n_new, 2 * D_WIDTH), psz)
    d_out = dsa_step_attn(page_table, q, keys, thr.reshape(nseq, rows, LANES), rem.reshape(nseq, rows, LANES), bias,
                          cache_kv.reshape(cache_kv.shape[0], cache_kv.shape[1], psz, 2 * D_WIDTH), kv_new, _tri(psz), j)
    d_out = d_out[:, :n_new].reshape(nseq * n_new, D_WIDTH)
    h = cd_out(h, c_out, d_out, prm['cd_w_out'][j], h.shape[0])
    return h, sc, kv, ik_new


def _trunk(x, p, prm, past, nseq, n_new, pos0, tm):
    depth = p.shape[0]
    h = x
    a_new = [[] for _ in A_PATTERNS]
    b_new, c_new, kv_new, kidx_new = [], [], [], []
    for i in range(depth):
        j = i // 2
        if i % 2 == 0:
            if past is None:
                h, bufs, sb = _layer_ab_prompt(h, prm, j, tm)
            else:
                h, bufs, sb = _layer_ab_sample(h, prm, j, past, nseq, n_new, pos0)
            for g in range(len(A_PATTERNS)):
                a_new[g].append(bufs[g])
            b_new.append(sb)
        else:
            if past is None:
                h, sc, kv, kidx = _layer_cd_prompt(h, prm, j, tm)
            else:
                h, sc, kv, kidx = _layer_cd_sample(h, prm, j, past, nseq, n_new)
            c_new.append(sc)
            kv_new.append(kv.reshape(nseq, n_new, 2, D_HEADS, HEAD_DIM))
            kidx_new.append(kidx.reshape(nseq, n_new, IDX_DIM))
        h = _ffn_and_ple(h, p[i], i, prm, tm, i == depth - 1)
    stack = lambda t: jnp.stack(t, axis=0)
    return h, tuple(stack(t) for t in a_new), stack(b_new), stack(c_new), stack(kv_new), stack(kidx_new)


def kernel(x_prompt, x_sample, p_prompt, p_sample, cache_a_kv0, cache_a_kv1, cache_a_kv2, state_b, state_c, cache_d_kv, cache_d_kidx, page_table, t5_table, norm_mix, norm_ffn, norm_ple, norm_final, ab_w_in, ab_w_out, ab_gn_gain, cd_w_in, cd_w_out, c_lambda_re, c_lambda_im, c_log_dt, c_b_re, c_b_im, c_c_re, c_c_im, c_d, c_w_glu, c_b_glu, moe_w_group, moe_b_group, moe_w_expert, moe_b_expert, moe_w_gate, moe_w_up, moe_w_down, ple_w_gate, ple_w_proj):
    depth, d_model = norm_mix.shape
    n_ab, n_cd = ab_w_in.shape[0], cd_w_in.shape[0]
    rpad = jnp.zeros((depth, d_model, LANES - MOE_GROUPS - MOE_EXPERTS), F32)
    prm = dict(
        t5_table=t5_table, norm_mix=norm_mix, norm_ffn=norm_ffn, norm_ple=norm_ple, norm_final=norm_final,
        ab_w=[_prep_ab_w(ab_w_in[j]) for j in range(n_ab)], ab_w_out=ab_w_out.astype(BF16), ab_gn_gain=ab_gn_gain,
        cd_w=[_prep_cd_w(cd_w_in[j]) for j in range(n_cd)], cd_w_out=cd_w_out.astype(BF16),
        s5=[_s5_prep(c_lambda_re[j], c_lambda_im[j], c_log_dt[j], c_b_re[j], c_b_im[j], c_c_re[j], c_c_im[j],
                     c_d[j], c_w_glu[j], c_b_glu[j]) for j in range(n_cd)],
        moe_wr=jnp.concatenate([moe_w_group, moe_w_expert.reshape(depth, d_model, MOE_EXPERTS), rpad], axis=2),
        moe_br=jnp.concatenate([moe_b_group, moe_b_expert.reshape(depth, MOE_EXPERTS),
                                jnp.zeros((depth, LANES - MOE_GROUPS - MOE_EXPERTS), F32)], axis=1)[:, None, :],
        moe_wg=moe_w_gate.astype(BF16), moe_wu=moe_w_up.astype(BF16), moe_wd=moe_w_down.astype(BF16),
        ple_wg=ple_w_gate.astype(BF16), ple_wp=ple_w_proj.astype(BF16))

    bsz, seq, _ = x_prompt.shape
    assert bsz == 1
    y_p, a_p, sb_p, sc_p, dkv_p, dki_p = _trunk(x_prompt[0], p_prompt[:, 0], prm, None, 1, seq, 0, min(256, seq))
    y_p = y_p[None]

    nseq, n_new, _ = x_sample.shape
    past = dict(a=(cache_a_kv0, cache_a_kv1, cache_a_kv2), b=state_b, c=state_c, d_kv=cache_d_kv, d_kidx=cache_d_kidx,
                page_table=page_table)
    n_past = page_table.shape[1] * cache_d_kv.shape[2]
    rows_s = nseq * n_new
    y_s, a_s, sb_s, sc_s, dkv_s, dki_s = _trunk(x_sample.reshape(rows_s, d_model),
                                                p_sample.reshape(depth, rows_s, -1), prm, past, nseq, n_new, n_past,
                                                rows_s)
    y_s = y_s.reshape(nseq, n_new, d_model)
    return (y_p, y_s, a_p[0], a_s[0], a_p[1], a_s[1], a_p[2], a_s[2], sb_p, sb_s, sc_p, sc_s, dkv_p, dkv_s, dki_p, dki_s)
```

```python
import functools
import math

import numpy as np
import jax
import jax.numpy as jnp
from jax import lax
from jax.experimental import pallas as pl
from jax.experimental.pallas import tpu as pltpu

F32 = jnp.float32
BF16 = jnp.bfloat16
I32 = jnp.int32

EPS = 1e-6
HEAD_DIM = 64
A_PATTERNS = ((128, 1), (512, 4), (2048, 16))
A_GROUP_HEADS = 2
A_WIDTH = 384
B_HEADS = 5
B_KEY_DIM = 64
B_VAL_DIM = 128
B_QK_WIDTH = 320
B_QK_PAD = 384
B_V_WIDTH = 640
B_CHUNK = 128
ROPE_BASE = 10000.0
C_GROUPS = 32
C_GROUP_SIZE = 16
C_WIDTH = 512
C_STATE = 64
C_FLAT = C_GROUPS * C_STATE
D_HEADS = 8
D_WIDTH = 512
IDX_HEADS = 4
IDX_DIM = 64
IDX_TOPK = 256
IDX_SCALE = (IDX_HEADS * IDX_DIM) ** -0.5
T5_BUCKETS = 32
T5_MAX_DIST = 2048
MOE_GROUPS = 4
MOE_GROUP_EXPERTS = 4
MOE_EXPERTS = 16
D_EXPERT = 512
LANES = 128
INT_MIN = -(2 ** 31)
NEG_INF = float("-inf")
LOG2E = math.log2(math.e)

VMEM_LIMIT_BYTES = 52 * 1024 * 1024

DSA_TQ = 256
DSA_TK = 1024
DSA_BQ = 128
DSA_BK = 512
DSA_DCAP = T5_MAX_DIST + DSA_BK
DSA_TAB = DSA_DCAP + DSA_BK + 2 * LANES
DSA_TRI = 256
KTH_ROWS = 128


def _cparams(*sem):
    return pltpu.CompilerParams(dimension_semantics=sem, vmem_limit_bytes=VMEM_LIMIT_BYTES)


def _rms(x, g):
    return x * lax.rsqrt(jnp.mean(x * x, axis=-1, keepdims=True) + EPS) * g


def _dot(a, b):
    return jnp.dot(a, b, preferred_element_type=F32)


def _dot_nt(a, b):
    return lax.dot_general(a, b, (((1,), (1,)), ((), ())), preferred_element_type=F32)


def _dot_tn(a, b):
    return lax.dot_general(a, b, (((0,), (0,)), ((), ())), preferred_element_type=F32)


def _sigmoid(x):
    return 1.0 / (1.0 + jnp.exp(-x))


def _t5_bucket(dist):
    max_exact = T5_BUCKETS // 2
    d = jnp.maximum(dist, 0)
    scaled = jnp.log(jnp.maximum(d, 1).astype(F32) / max_exact) / math.log(T5_MAX_DIST / max_exact)
    large = jnp.minimum(max_exact + (scaled * (T5_BUCKETS - max_exact)).astype(I32), T5_BUCKETS - 1)
    return jnp.where(d < max_exact, d, large)


def _norm_matmul_kernel(x_ref, g_ref, w_ref, *out_refs, segs):
    xn = _rms(x_ref[...], g_ref[...]).astype(BF16)
    for o_ref, (start, width) in zip(out_refs, segs):
        o_ref[...] = _dot(xn, w_ref[:, start:start + width]).astype(o_ref.dtype)


def norm_matmul(x, gain, w, segs, tm):
    t, d = x.shape
    n = w.shape[1]
    kern = functools.partial(_norm_matmul_kernel, segs=tuple((s, wd) for s, wd, _ in segs))
    return pl.pallas_call(
        kern, grid=(t // tm,),
        in_specs=[pl.BlockSpec((tm, d), lambda i: (i, 0)),
                  pl.BlockSpec((1, d), lambda i: (0, 0)),
                  pl.BlockSpec((d, n), lambda i: (0, 0))],
        out_specs=[pl.BlockSpec((tm, wd), lambda i: (i, 0)) for _, wd, _ in segs],
        out_shape=[jax.ShapeDtypeStruct((t, wd), dt) for _, wd, dt in segs],
        compiler_params=_cparams("parallel"), name="norm_matmul")(x, gain.reshape(1, d), w)


def _dil_prompt_kernel(q_ref, kc_ref, kp_ref, vc_ref, vp_ref, bias_ref, o_ref, lse_ref):
    n = pl.program_id(1)
    blk = q_ref.shape[0]
    q = q_ref[...]
    kw = jnp.concatenate([kp_ref[...], kc_ref[...]], axis=0)
    vw = jnp.concatenate([vp_ref[...], vc_ref[...]], axis=0)
    qi = lax.broadcasted_iota(I32, (blk, 2 * blk), 0)
    kj = lax.broadcasted_iota(I32, (blk, 2 * blk), 1)
    m = blk + qi - kj
    first_key = jnp.where(n > 0, 0, blk)
    mask = (m >= 0) & (m <= blk) & (kj >= first_key)
    outs, lses = [], []
    for h in range(A_GROUP_HEADS):
        cs = slice(h * HEAD_DIM, (h + 1) * HEAD_DIM)
        s = _dot_nt(q[:, cs].astype(BF16), kw[:, cs].astype(BF16)) * (HEAD_DIM ** -0.5) + bias_ref[h]
        s = jnp.where(mask, s, NEG_INF)
        mx = jnp.max(s, axis=1, keepdims=True)
        p = jnp.exp(s - mx)
        l = jnp.sum(p, axis=1, keepdims=True)
        outs.append(_dot((p / l).astype(BF16), vw[:, cs].astype(BF16)))
        lses.append(jnp.broadcast_to(mx + jnp.log(l), (blk, HEAD_DIM)))
    o_ref[...] = jnp.concatenate(outs, axis=1)
    lse_ref[...] = jnp.concatenate(lses, axis=1)


def dil_prompt(aqkv, bias_qk, g, dil):
    seq = aqkv.shape[0]
    blk = bias_qk.shape[1]
    nb = seq // (blk * dil)
    ncol = aqkv.shape[1] // LANES
    a = aqkv.reshape(seq // dil, dil * aqkv.shape[1])
    ng = len(A_PATTERNS)

    def cur(off):
        return pl.BlockSpec((blk, LANES), lambda r, n: (n, r * ncol + off + g))

    def prev(off):
        return pl.BlockSpec((blk, LANES), lambda r, n: (jnp.maximum(n - 1, 0), r * ncol + off + g))

    o, lse = pl.pallas_call(
        _dil_prompt_kernel, grid=(dil, nb),
        in_specs=[cur(0), cur(ng), prev(ng), cur(2 * ng), prev(2 * ng),
                  pl.BlockSpec(bias_qk.shape, lambda r, n: (0, 0, 0))],
        out_specs=[pl.BlockSpec((blk, LANES), lambda r, n: (n, r))] * 2,
        out_shape=[jax.ShapeDtypeStruct((seq // dil, dil * LANES), F32)] * 2,
        compiler_params=_cparams("parallel", "parallel"), name="dil_prompt")(a, a, a, a, a, bias_qk)
    return o.reshape(seq, LANES), lse.reshape(seq, LANES)


def _dil_step_kernel(q_ref, buf_ref, new_ref, bias_ref, o_ref, lse_ref):
    q = q_ref[0]
    rows = q.shape[0]
    head = lax.broadcasted_iota(I32, q.shape, 1) // HEAD_DIM
    qbd = jnp.concatenate([jnp.where(head == h, q, 0.0) for h in range(A_GROUP_HEADS)], axis=0).astype(BF16)
    kv = jnp.concatenate([buf_ref[0, 0], new_ref[0]], axis=0)
    k = kv[:, :LANES].astype(BF16)
    v = kv[:, LANES:].astype(BF16)
    s = _dot_nt(qbd, k) * (HEAD_DIM ** -0.5) + bias_ref[...]
    mx = jnp.max(s, axis=1, keepdims=True)
    p = jnp.exp(s - mx)
    l = jnp.sum(p, axis=1, keepdims=True)
    o = _dot((p / l).astype(BF16), v)
    lse = jnp.broadcast_to(mx + jnp.log(l), o.shape)
    o_ref[0] = jnp.where(head == 0, o[:rows], o[rows:])
    lse_ref[0] = jnp.where(head == 0, lse[:rows], lse[rows:])


def dil_step(q, buf, new, bias, layer):
    nseq, rows, _ = q.shape
    win = buf.shape[2]
    return pl.pallas_call(
        _dil_step_kernel, grid=(nseq,),
        in_specs=[pl.BlockSpec((1, rows, LANES), lambda n: (n, 0, 0)),
                  pl.BlockSpec((1, 1, win, 2 * LANES), lambda n: (layer, n, 0, 0)),
                  pl.BlockSpec((1,) + new.shape[1:], lambda n: (n, 0, 0)),
                  pl.BlockSpec(bias.shape, lambda n: (0, 0))],
        out_specs=[pl.BlockSpec((1, rows, LANES), lambda n: (n, 0, 0))] * 2,
        out_shape=[jax.ShapeDtypeStruct((nseq, rows, LANES), F32)] * 2,
        compiler_params=_cparams("parallel"), name="dil_step")(q, buf, new, bias)


def _retention_kernel(bqk_ref, cos_ref, sin_ref, bv_ref, bg_ref, decay_ref, win_ref, wend_ref, gch_ref, gain_ref,
                      s0_ref, out_ref, s_ref):
    c = pl.program_id(1)

    @pl.when(c == 0)
    def _():
        s_ref[...] = s0_ref[...]

    bqk = bqk_ref[...]
    cosv, sinv = cos_ref[...], sin_ref[...]
    w = B_QK_PAD
    q = (bqk[:, 0:w] * cosv + bqk[:, w:2 * w] * sinv) * (B_KEY_DIM ** -0.5)
    k = bqk[:, 2 * w:3 * w] * cosv + bqk[:, 3 * w:4 * w] * sinv
    q_in = q * win_ref[...]
    k_end = k * wend_ref[...]
    bv, bg, gain = bv_ref[...], bg_ref[...], gain_ref[...]
    outs = []
    for h in range(B_HEADS):
        ks = slice(h * B_KEY_DIM, (h + 1) * B_KEY_DIM)
        vs = slice(h * B_VAL_DIM, (h + 1) * B_VAL_DIM)
        vh = bv[:, vs].astype(BF16)
        state = s_ref[0, h]
        intra = _dot_nt(q[:, ks].astype(BF16), k[:, ks].astype(BF16)) * decay_ref[h]
        o = _dot(intra.astype(BF16), vh) + _dot(q_in[:, ks].astype(BF16), state.astype(BF16))
        s_ref[0, h] = gch_ref[h] * state + _dot_tn(k_end[:, ks].astype(BF16), vh)
        mu = jnp.mean(o, axis=1, keepdims=True)
        var = jnp.mean((o - mu) ** 2, axis=1, keepdims=True)
        on = (o - mu) * lax.rsqrt(var + EPS) * gain[:, vs]
        g = bg[:, vs]
        outs.append(g * _sigmoid(g) * on)
    out_ref[...] = jnp.concatenate(outs, axis=1)


def retention(bqk, cosv, sinv, bv, bg, tabs, gain, s0, chunk, nchunk):
    nseq = s0.shape[0]
    rows = bqk.shape[0]
    decay, w_in, w_end, g_chunk = tabs
    row = lambda b, c: (b * nchunk + c, 0)
    pos = lambda b, c: (c, 0)
    fix2 = lambda b, c: (0, 0)
    fix3 = lambda b, c: (0, 0, 0)
    return pl.pallas_call(
        _retention_kernel, grid=(nseq, nchunk),
        in_specs=[pl.BlockSpec((chunk, 4 * B_QK_PAD), row),
                  pl.BlockSpec((chunk, B_QK_PAD), pos), pl.BlockSpec((chunk, B_QK_PAD), pos),
                  pl.BlockSpec((chunk, B_V_WIDTH), row), pl.BlockSpec((chunk, B_V_WIDTH), row),
                  pl.BlockSpec(decay.shape, fix3), pl.BlockSpec(w_in.shape, fix2), pl.BlockSpec(w_end.shape, fix2),
                  pl.BlockSpec(g_chunk.shape, fix3), pl.BlockSpec((1, B_V_WIDTH), fix2),
                  pl.BlockSpec((1,) + s0.shape[1:], lambda b, c: (b, 0, 0, 0))],
        out_specs=[pl.BlockSpec((chunk, B_V_WIDTH), row),
                   pl.BlockSpec((1,) + s0.shape[1:], lambda b, c: (b, 0, 0, 0))],
        out_shape=[jax.ShapeDtypeStruct((rows, B_V_WIDTH), F32), jax.ShapeDtypeStruct(s0.shape, F32)],
        compiler_params=_cparams("parallel", "arbitrary"), name="retention")(
            bqk, cosv, sinv, bv, bg, decay, w_in, w_end, g_chunk, gain.reshape(1, B_V_WIDTH), s0)


def _retention_tables(chunk, pad):
    lg = np.log(1.0 - 2.0 ** (-5.0 - np.arange(B_HEADS, dtype=np.float32))).astype(np.float32)
    lg = jnp.asarray(lg)
    i = jnp.arange(pad, dtype=F32)
    diff = i[:, None] - i[None, :]
    decay = jnp.where(diff >= 0, jnp.exp(jnp.maximum(diff, 0.0)[None] * lg[:, None, None]), 0.0)
    w_end = jnp.exp((chunk - 1.0 - i)[None, :] * lg[:, None])
    w_in = jnp.exp((i + 1.0)[None, :] * lg[:, None])
    g_chunk = jnp.exp(chunk * lg)

    def widen(t):
        t = jnp.repeat(t.T, B_KEY_DIM, axis=1)
        return jnp.pad(t, ((0, 0), (0, B_QK_PAD - B_QK_WIDTH)))

    return decay, widen(w_in), widen(w_end), jnp.broadcast_to(g_chunk[:, None, None], (B_HEADS, 1, B_VAL_DIM))


def _rope_tables(pos):
    half = B_KEY_DIM // 2
    freq = ROPE_BASE ** (-jnp.arange(half, dtype=F32) / half)
    ang = pos.astype(F32)[:, None] * freq[None, :]
    cos, sin = jnp.cos(ang), jnp.sin(ang)
    cos_h = jnp.concatenate([cos, cos], axis=1)
    sin_h = jnp.concatenate([-sin, sin], axis=1)
    padw = ((0, 0), (0, B_QK_PAD - B_QK_WIDTH))
    return jnp.pad(jnp.tile(cos_h, (1, B_HEADS)), padw), jnp.pad(jnp.tile(sin_h, (1, B_HEADS)), padw)


def _ab_out_kernel(h_ref, o0_ref, o1_ref, o2_ref, l0_ref, l1_ref, l2_ref, b_ref, w_ref, out_ref):
    ls = [l0_ref[...], l1_ref[...], l2_ref[...]]
    os_ = [o0_ref[...], o1_ref[...], o2_ref[...]]
    mx = jnp.maximum(jnp.maximum(ls[0], ls[1]), ls[2])
    es = [jnp.exp(l - mx) for l in ls]
    den = es[0] + es[1] + es[2]
    acc = h_ref[...]
    for g in range(3):
        acc = acc + _dot((os_[g] * (es[g] / den)).astype(BF16), w_ref[g * LANES:(g + 1) * LANES, :])
    acc = acc + _dot(b_ref[...].astype(BF16), w_ref[A_WIDTH:, :])
    out_ref[...] = acc


def ab_out(h, os_, ls, b_out, w, tm):
    t, d = h.shape
    row = lambda i: (i, 0)
    small = pl.BlockSpec((tm, LANES), row)
    return pl.pallas_call(
        _ab_out_kernel, grid=(t // tm,),
        in_specs=[pl.BlockSpec((tm, d), row)] + [small] * 6 +
                 [pl.BlockSpec((tm, B_V_WIDTH), row), pl.BlockSpec(w.shape, lambda i: (0, 0))],
        out_specs=pl.BlockSpec((tm, d), row),
        out_shape=jax.ShapeDtypeStruct((t, d), F32),
        compiler_params=_cparams("parallel"), name="ab_out")(h, *os_, *ls, b_out, w)


def _cd_out_kernel(h_ref, c_ref, d_ref, w_ref, out_ref):
    acc = h_ref[...] + _dot(c_ref[...].astype(BF16), w_ref[:C_WIDTH, :])
    out_ref[...] = acc + _dot(d_ref[...].astype(BF16), w_ref[C_WIDTH:, :])


def cd_out(h, c_out, d_out, w, tm):
    t, d = h.shape
    row = lambda i: (i, 0)
    return pl.pallas_call(
        _cd_out_kernel, grid=(t // tm,),
        in_specs=[pl.BlockSpec((tm, d), row), pl.BlockSpec((tm, C_WIDTH), row), pl.BlockSpec((tm, D_WIDTH), row),
                  pl.BlockSpec(w.shape, lambda i: (0, 0))],
        out_specs=pl.BlockSpec((tm, d), row),
        out_shape=jax.ShapeDtypeStruct((t, d), F32),
        compiler_params=_cparams("parallel"), name="cd_out")(h, c_out, d_out, w)


def _route(logits):
    lane = lax.broadcasted_iota(I32, logits.shape, 1).astype(F32)
    gmask = lane < MOE_GROUPS
    gl = jnp.where(gmask, logits, NEG_INF)
    gmax = jnp.max(gl, axis=1, keepdims=True)
    gsum = jnp.sum(jnp.where(gmask, jnp.exp(gl - gmax), 0.0), axis=1, keepdims=True)
    g_w = 1.0 / gsum
    g_idx = jnp.min(jnp.where(gl == gmax, lane, LANES), axis=1, keepdims=True)
    lo = MOE_GROUPS + MOE_GROUP_EXPERTS * g_idx
    emask = (lane >= lo) & (lane < lo + MOE_GROUP_EXPERTS)
    el = jnp.where(emask, logits, NEG_INF)
    emax = jnp.max(el, axis=1, keepdims=True)
    eexp = jnp.where(emask, jnp.exp(el - emax), 0.0)
    p = eexp / jnp.sum(eexp, axis=1, keepdims=True)
    pm = jnp.where(emask, p, -1.0)
    p1 = jnp.max(pm, axis=1, keepdims=True)
    i1 = jnp.min(jnp.where(pm == p1, lane, LANES), axis=1, keepdims=True)
    pm2 = jnp.where(lane == i1, -1.0, pm)
    p2 = jnp.max(pm2, axis=1, keepdims=True)
    i2 = jnp.min(jnp.where(pm2 == p2, lane, LANES), axis=1, keepdims=True)
    den = p1 + p2
    return jnp.where(lane == i1, g_w * (p1 / den), jnp.where(lane == i2, g_w * (p2 / den), 0.0))


def _moe_kernel(h_ref, g_ref, wr_ref, br_ref, wg_ref, wu_ref, wd_ref, out_ref, xn_s, comb_s, acc_s):
    e = pl.program_id(1)

    @pl.when(e == 0)
    def _():
        xn = _rms(h_ref[...], g_ref[...]).astype(BF16)
        xn_s[...] = xn
        comb_s[...] = _route(_dot(xn, wr_ref[...]) + br_ref[...])
        acc_s[...] = jnp.zeros_like(acc_s)

    xb = xn_s[...]
    hg = _dot(xb, wg_ref[0])
    hu = _dot(xb, wu_ref[0])
    hdn = (hg * _sigmoid(hg)) * hu
    y = _dot(hdn.astype(BF16), wd_ref[0])
    comb = comb_s[...]
    lane = lax.broadcasted_iota(I32, comb.shape, 1)
    cw = jnp.sum(jnp.where(lane == MOE_GROUPS + e, comb, 0.0), axis=1, keepdims=True)
    acc_s[...] += cw * y

    @pl.when(e == MOE_EXPERTS - 1)
    def _():
        out_ref[...] = h_ref[...] + acc_s[...]


def moe(h, gain, wr, br, wg, wu, wd, tm):
    t, d = h.shape
    row = lambda i, e: (i, 0)
    fix = lambda i, e: (0, 0)
    ex = lambda i, e: (e, 0, 0)
    return pl.pallas_call(
        _moe_kernel, grid=(t // tm, MOE_EXPERTS),
        in_specs=[pl.BlockSpec((tm, d), row), pl.BlockSpec((1, d), fix),
                  pl.BlockSpec((d, LANES), fix), pl.BlockSpec((1, LANES), fix),
                  pl.BlockSpec((1, d, D_EXPERT), ex), pl.BlockSpec((1, d, D_EXPERT), ex),
                  pl.BlockSpec((1, D_EXPERT, d), ex)],
        out_specs=pl.BlockSpec((tm, d), row),
        out_shape=jax.ShapeDtypeStruct((t, d), F32),
        scratch_shapes=[pltpu.VMEM((tm, d), BF16), pltpu.VMEM((tm, LANES), F32), pltpu.VMEM((tm, d), F32)],
        compiler_params=_cparams("parallel", "arbitrary"), name="moe")(
            h, gain.reshape(1, d), wr, br, wg, wu, wd)


def _ple_kernel(h_ref, p_ref, g_ref, wg_ref, wp_ref, *rest, final):
    h = h_ref[...]
    gate = _sigmoid(_dot(_rms(h, g_ref[...]).astype(BF16), wg_ref[...]))
    hn = h + gate * _dot(p_ref[...].astype(BF16), wp_ref[...])
    if final:
        gf_ref, out_ref = rest
        out_ref[...] = _rms(hn, gf_ref[...])
    else:
        rest[0][...] = hn


def ple(h, p, gain, wg, wp, gain_final, tm):
    t, d = h.shape
    row = lambda i: (i, 0)
    fix = lambda i: (0, 0)
    final = gain_final is not None
    in_specs = [pl.BlockSpec((tm, d), row), pl.BlockSpec((tm, p.shape[1]), row), pl.BlockSpec((1, d), fix),
                pl.BlockSpec(wg.shape, fix), pl.BlockSpec(wp.shape, fix)]
    args = [h, p, gain.reshape(1, d), wg, wp]
    if final:
        in_specs.append(pl.BlockSpec((1, d), fix))
        args.append(gain_final.reshape(1, d))
    return pl.pallas_call(
        functools.partial(_ple_kernel, final=final), grid=(t // tm,),
        in_specs=in_specs, out_specs=pl.BlockSpec((tm, d), row),
        out_shape=jax.ShapeDtypeStruct((t, d), F32),
        compiler_params=_cparams("parallel"), name="ple")(*args)


def _s5_kernel(u_ref, a_ref, bre_ref, bim_ref, cre_ref, cim_ref, d_ref, wglu_ref, bglu_ref, x0r_ref, x0i_ref,
               out_ref, xr_ref, xi_ref, bur_s, bui_s, *, nbatch, steps):
    c = pl.program_id(0)

    @pl.when(c == 0)
    def _():
        xr_ref[...] = x0r_ref[...]
        xi_ref[...] = x0i_ref[...]

    u = u_ref[...]
    ub = u.astype(BF16)
    bur_s[...] = _dot(ub, bre_ref[...])
    bui_s[...] = _dot(ub, bim_ref[...])
    ar = a_ref[0:1, :]
    ai = a_ref[1:2, :]

    def step(t, carry):
        xr, xi = carry
        r0 = pl.multiple_of(t * nbatch, nbatch)
        nr = ar * xr - ai * xi + bur_s[pl.ds(r0, nbatch), :]
        ni = ar * xi + ai * xr + bui_s[pl.ds(r0, nbatch), :]
        bur_s[pl.ds(r0, nbatch), :] = nr
        bui_s[pl.ds(r0, nbatch), :] = ni
        return nr, ni

    xr, xi = lax.fori_loop(0, steps, step, (xr_ref[...], xi_ref[...]), unroll=min(steps, 8))
    xr_ref[...] = xr
    xi_ref[...] = xi
    y = _dot(bur_s[...].astype(BF16), cre_ref[...]) - _dot(bui_s[...].astype(BF16), cim_ref[...]) + d_ref[...] * u
    z = 0.5 * y * (1.0 + jnp.tanh(math.sqrt(2.0 / math.pi) * (y + 0.044715 * (y * y * y))))
    out_ref[...] = z * _sigmoid(_dot(z.astype(BF16), wglu_ref[...]) + bglu_ref[...])


def s5(u, prm, x0r, x0i, nbatch, steps):
    rows = u.shape[0]
    a, bre, bim, cre, cim, dskip, wglu, bglu = prm
    nchunk = rows // (nbatch * steps)
    tr = nbatch * steps
    fix = lambda c: (0, 0)
    full = lambda arr: pl.BlockSpec(arr.shape, fix)
    return pl.pallas_call(
        functools.partial(_s5_kernel, nbatch=nbatch, steps=steps), grid=(nchunk,),
        in_specs=[pl.BlockSpec((tr, C_WIDTH), lambda c: (c, 0))] +
                 [full(t) for t in (a, bre, bim, cre, cim, dskip, wglu, bglu, x0r, x0i)],
        out_specs=[pl.BlockSpec((tr, C_WIDTH), lambda c: (c, 0)), full(x0r), full(x0i)],
        out_shape=[jax.ShapeDtypeStruct((rows, C_WIDTH), F32), jax.ShapeDtypeStruct(x0r.shape, F32),
                   jax.ShapeDtypeStruct(x0i.shape, F32)],
        scratch_shapes=[pltpu.VMEM((tr, C_FLAT), F32), pltpu.VMEM((tr, C_FLAT), F32)],
        compiler_params=_cparams("arbitrary"), name="s5")(u, a, bre, bim, cre, cim, dskip, wglu, bglu, x0r, x0i)


def _s5_prep(lam_re, lam_im, log_dt, b_re, b_im, c_re, c_im, d_skip, w_glu, b_glu):
    dt = jnp.exp(log_dt)[:, None]
    mag = jnp.exp(lam_re * dt)
    ab_re, ab_im = mag * jnp.cos(lam_im * dt), mag * jnp.sin(lam_im * dt)
    den = lam_re * lam_re + lam_im * lam_im
    zr = ((ab_re - 1.0) * lam_re + ab_im * lam_im) / den
    zi = (ab_im * lam_re - (ab_re - 1.0) * lam_im) / den
    bb_re = zr[..., None] * b_re - zi[..., None] * b_im
    bb_im = zr[..., None] * b_im + zi[..., None] * b_re
    eye = jnp.eye(C_GROUPS, dtype=F32)
    b_dense = lambda bb: jnp.einsum('gnc,gh->gchn', bb, eye).reshape(C_WIDTH, C_FLAT).astype(BF16)
    c_dense = lambda cc: jnp.einsum('gcn,gh->gnhc', cc, eye).reshape(C_FLAT, C_WIDTH).astype(BF16)
    a = jnp.stack([ab_re.reshape(C_FLAT), ab_im.reshape(C_FLAT)], axis=0)
    return (a, b_dense(bb_re), b_dense(bb_im), c_dense(c_re), c_dense(c_im), d_skip.reshape(1, C_WIDTH),
            w_glu.astype(BF16), b_glu.reshape(1, C_WIDTH))


def _sortable(x):
    bits = lax.bitcast_convert_type(x, I32)
    return bits ^ ((bits >> 31) & jnp.int32(0x7FFFFFFF))


def _kth_largest(keys_ref, ntiles, tw, n_sel):
    total_rows = keys_ref.shape[0]
    rows = min(total_rows, KTH_ROWS)
    thrs, rems = [], []
    for r0 in range(0, total_rows, rows):
        def count_ge(cand, r0=r0):
            def body(t, cv):
                off = pl.multiple_of(t * tw, LANES)
                kk = keys_ref[r0:r0 + rows, pl.ds(off, tw)]
                for j in range(tw // LANES):
                    cv = cv + jnp.where(kk[:, j * LANES:(j + 1) * LANES] >= cand, 1.0, 0.0)
                return cv
            cv = lax.fori_loop(0, ntiles, body, jnp.zeros((rows, LANES), F32))
            return jnp.sum(cv, axis=1, keepdims=True)

        def bit(i, thr, count_ge=count_ge):
            cand = thr + lax.shift_left(jnp.int32(1), 31 - i)
            return jnp.where(count_ge(cand) >= n_sel, cand, thr)

        thr = lax.fori_loop(0, 32, bit, jnp.full((rows, 1), INT_MIN, I32))
        thrs.append(thr)
        rems.append(jnp.where(thr == INT_MIN, 0.0, n_sel - count_ge(thr + 1)))
    if len(thrs) == 1:
        return thrs[0], rems[0]
    return jnp.concatenate(thrs, axis=0), jnp.concatenate(rems, axis=0)


def _select(key, thr, rem, cnt, tri):
    eq = key == thr
    eqb = jnp.where(eq, 1.0, 0.0).astype(BF16)
    w = tri.shape[0]
    ranks = []
    for c in range(key.shape[1] // w):
        pc = _dot(eqb[:, c * w:(c + 1) * w], tri)
        ranks.append(cnt + pc)
        cnt = cnt + pc[:, -1:]
    rank = ranks[0] if len(ranks) == 1 else jnp.concatenate(ranks, axis=1)
    sel = (key > thr) | (eq & (rank <= rem))
    return sel, cnt


def _dsa_bias_kernel(tbr_ref, out_ref):
    tq, tk = DSA_BQ, DSA_BK
    start = pl.multiple_of(DSA_DCAP - pl.program_id(0) * LANES, LANES)
    wtab = tbr_ref[:, pl.ds(start, tk + 2 * LANES)]
    for h in range(D_HEADS):
        wb = jnp.broadcast_to(wtab[h:h + 1, :], (tq, tk + 2 * LANES))
        out_ref[0, h] = pltpu.roll(wb, 1, 1, stride=1, stride_axis=0)[:, LANES:LANES + tk]


def dsa_bias_tiles(tbr):
    nd = DSA_DCAP // LANES + 1
    return pl.pallas_call(
        _dsa_bias_kernel, grid=(nd,),
        in_specs=[pl.BlockSpec(tbr.shape, lambda d: (0, 0))],
        out_specs=pl.BlockSpec((1, D_HEADS, DSA_BQ, DSA_BK), lambda d: (d, 0, 0, 0)),
        out_shape=jax.ShapeDtypeStruct((nd, D_HEADS, DSA_BQ, DSA_BK), F32),
        compiler_params=_cparams("parallel"), name="dsa_bias_tiles")(tbr)


def _dsa_prompt_kernel(q_ref, iq_ref, misc_ref, ikt_ref, kt_ref, v_ref, *rest, n_sel):
    tq, tk = DSA_TQ, DSA_TK
    nqa, nkb = tq // DSA_BQ, tk // DSA_BK
    bias_refs = rest[:nqa * nkb]
    tri_ref, o_ref, keys_s, thr_s, rem_s, cnt_s, neg_s, m_s, l_s, acc_s = rest[nqa * nkb:]
    qb = pl.program_id(0)
    kt = pl.program_id(1)
    kl = (qb * tq + tq - 1) // tk

    @pl.when(kt == 0)
    def _():
        iq = iq_ref[...]
        iw = misc_ref[:, IDX_DIM:IDX_DIM + IDX_HEADS]
        qpos = qb * tq + lax.broadcasted_iota(I32, (tq, tk), 0)

        def scores(t, carry):
            off = pl.multiple_of(t * tk, tk)
            ikt = ikt_ref[:, pl.ds(off, tk)]
            sc = jnp.zeros((tq, tk), F32)
            for h in range(IDX_HEADS):
                s = _dot(iq[:, h * IDX_DIM:(h + 1) * IDX_DIM], ikt)
                sc = sc + jnp.maximum(s, 0.0) * iw[:, h:h + 1]
            kpos = off + lax.broadcasted_iota(I32, (tq, tk), 1)
            keys_s[:, pl.ds(off, tk)] = jnp.where(kpos <= qpos, _sortable(sc * IDX_SCALE), INT_MIN)
            return carry

        lax.fori_loop(0, kl + 1, scores, 0)
        thr, rem = _kth_largest(keys_s, kl + 1, tk, n_sel)
        thr_s[...] = thr
        rem_s[...] = rem
        cnt_s[...] = jnp.zeros_like(cnt_s)
        m_s[...] = jnp.full_like(m_s, NEG_INF)
        l_s[...] = jnp.zeros_like(l_s)
        acc_s[...] = jnp.zeros_like(acc_s)

    @pl.when(kt <= kl)
    def _():
        off = pl.multiple_of(kt * tk, tk)
        sel, cnt = _select(keys_s[:, pl.ds(off, tk)], thr_s[...], rem_s[...], cnt_s[...], tri_ref[...])
        cnt_s[...] = cnt
        neg_s[...] = jnp.where(sel, 0.0, NEG_INF)
        for h in range(D_HEADS):
            bias = jnp.concatenate(
                [jnp.concatenate([bias_refs[a * nkb + b][0, h] for b in range(nkb)], axis=1) for a in range(nqa)],
                axis=0)
            s = _dot(q_ref[h], kt_ref[h * HEAD_DIM:(h + 1) * HEAD_DIM, :]) * (LOG2E * HEAD_DIM ** -0.5)
            s = s + bias + neg_s[...]
            m_old = m_s[h]
            m_new = jnp.maximum(m_old, jnp.max(s, axis=1, keepdims=True))
            m_safe = jnp.where(m_new == NEG_INF, 0.0, m_new)
            p = jnp.exp2(s - m_safe)
            alpha = jnp.exp2(m_old - m_safe)
            l_s[h] = alpha * l_s[h] + jnp.sum(p, axis=1, keepdims=True)
            acc_s[h] = alpha * acc_s[h] + _dot(p.astype(BF16), v_ref[h])
            m_s[h] = m_new

    @pl.when(kt == kl)
    def _():
        o_ref[...] = jnp.concatenate([acc_s[h] / l_s[h] for h in range(D_HEADS)], axis=1)


def dsa_prompt(q, iq, misc, ikt, kt_, v, bias_tiles, tri, n_sel):
    seq = q.shape[1]
    tq, tk = DSA_TQ, DSA_TK
    nb, nkt = seq // tq, seq // tk
    last = lambda qb: (qb * tq + tq - 1) // tk
    qrow = lambda qb, kt: (qb, 0)
    fix = lambda qb, kt: (0, 0)

    def bias_spec(a, b):
        def idx(qb, kt):
            d = (qb * tq + a * DSA_BQ - jnp.minimum(kt, last(qb)) * tk - b * DSA_BK) // LANES
            return (jnp.clip(d, 0, DSA_DCAP // LANES), 0, 0, 0)
        return pl.BlockSpec((1, D_HEADS, DSA_BQ, DSA_BK), idx)

    bias_specs = [bias_spec(a, b) for a in range(tq // DSA_BQ) for b in range(tk // DSA_BK)]
    return pl.pallas_call(
        functools.partial(_dsa_prompt_kernel, n_sel=n_sel), grid=(nb, nkt),
        in_specs=[pl.BlockSpec((D_HEADS, tq, HEAD_DIM), lambda qb, kt: (0, qb, 0)),
                  pl.BlockSpec((tq, IDX_HEADS * IDX_DIM), qrow),
                  pl.BlockSpec((tq, LANES), qrow), pl.BlockSpec(ikt.shape, fix),
                  pl.BlockSpec((D_WIDTH, tk), lambda qb, kt: (0, jnp.minimum(kt, last(qb)))),
                  pl.BlockSpec((D_HEADS, tk, HEAD_DIM), lambda qb, kt: (0, jnp.minimum(kt, last(qb)), 0))] +
                 bias_specs + [pl.BlockSpec(tri.shape, fix)],
        out_specs=pl.BlockSpec((tq, D_WIDTH), qrow),
        out_shape=jax.ShapeDtypeStruct((seq, D_WIDTH), F32),
        scratch_shapes=[pltpu.VMEM((tq, seq), I32), pltpu.VMEM((tq, 1), I32), pltpu.VMEM((tq, 1), F32),
                        pltpu.VMEM((tq, 1), F32), pltpu.VMEM((tq, tk), F32), pltpu.VMEM((D_HEADS, tq, 1), F32),
                        pltpu.VMEM((D_HEADS, tq, 1), F32), pltpu.VMEM((D_HEADS, tq, HEAD_DIM), F32)],
        compiler_params=_cparams("parallel", "arbitrary"), name="dsa_prompt")(
            q, iq, misc, ikt, kt_, v, *([bias_tiles] * len(bias_specs)), tri)


def _dsa_step_keys_kernel(pt_ref, iq_ref, iw_ref, *rest, n_new, pg):
    page_refs, new_ref, keys_ref = rest[:pg], rest[pg], rest[pg + 1]
    p = pl.program_id(1)
    npage = pl.num_programs(1) * pg
    psz = new_ref.shape[1]
    rows = keys_ref.shape[1]

    def score(kmat):
        r = jnp.maximum(_dot_nt(iq_ref[0], kmat.astype(BF16)), 0.0) * iw_ref[0]
        sc = r[0:rows]
        for h in range(1, IDX_HEADS):
            sc = sc + r[h * rows:(h + 1) * rows]
        return _sortable(sc * IDX_SCALE)

    for i, page_ref in enumerate(page_refs):
        keys_ref[0, :, pl.ds(pl.multiple_of((p * pg + i) * psz, psz), psz)] = score(page_ref[0, 0])

    @pl.when(p == 0)
    def _():
        s_i = lax.broadcasted_iota(I32, (rows, psz), 0)
        j_i = lax.broadcasted_iota(I32, (rows, psz), 1)
        ok = (j_i <= s_i) & (j_i < n_new)
        keys_ref[0, :, pl.ds(pl.multiple_of(npage * psz, psz), psz)] = jnp.where(ok, score(new_ref[0]), INT_MIN)


def dsa_step_keys(page_table, iqm, iwm, cache_kidx, ik_new, layer, n_new, pg):
    nseq, npage = page_table.shape
    psz = cache_kidx.shape[2]
    rows = iqm.shape[1] // IDX_HEADS
    ncol = (npage + 1) * psz
    seqb = lambda n, p, pt: (n, 0, 0)
    page_spec = lambda i: pl.BlockSpec((1, 1, psz, IDX_DIM),
                                       lambda n, p, pt: (layer, pt[n * npage + p * pg + i], 0, 0))
    grid_spec = pltpu.PrefetchScalarGridSpec(
        num_scalar_prefetch=1, grid=(nseq, npage // pg),
        in_specs=[pl.BlockSpec((1,) + iqm.shape[1:], seqb), pl.BlockSpec((1,) + iwm.shape[1:], seqb)] +
                 [page_spec(i) for i in range(pg)] + [pl.BlockSpec((1,) + ik_new.shape[1:], seqb)],
        out_specs=pl.BlockSpec((1, rows, ncol), seqb))
    return pl.pallas_call(
        functools.partial(_dsa_step_keys_kernel, n_new=n_new, pg=pg), grid_spec=grid_spec,
        out_shape=jax.ShapeDtypeStruct((nseq, rows, ncol), I32),
        compiler_params=_cparams("parallel", "arbitrary"), name="dsa_step_keys")(
            page_table.reshape(-1), iqm, iwm, *([cache_kidx] * pg), ik_new)


def _kth_kernel(keys_ref, thr_ref, rem_ref, *, tw, n_sel):
    thr, rem = _kth_largest(keys_ref, keys_ref.shape[1] // tw, tw, n_sel)
    thr_ref[...] = jnp.broadcast_to(thr, thr_ref.shape)
    rem_ref[...] = jnp.broadcast_to(rem, rem_ref.shape)


def kth_rows(keys, tw, n_sel, tr):
    rows, ncol = keys.shape
    return pl.pallas_call(
        functools.partial(_kth_kernel, tw=tw, n_sel=n_sel), grid=(rows // tr,),
        in_specs=[pl.BlockSpec((tr, ncol), lambda i: (i, 0))],
        out_specs=[pl.BlockSpec((tr, LANES), lambda i: (i, 0))] * 2,
        out_shape=[jax.ShapeDtypeStruct((rows, LANES), I32), jax.ShapeDtypeStruct((rows, LANES), F32)],
        compiler_params=_cparams("parallel"), name="kth_rows")(keys)


def _dsa_step_attn_kernel(pt_ref, q_ref, keys_ref, keys_new_ref, thr_ref, rem_ref, bias_ref, bias_new_ref,
                          *rest, pg):
    k_refs, v_refs = rest[:pg], rest[pg:2 * pg]
    kn_ref, vn_ref, expand_ref, tri_ref, o_ref, cnt_s, m_s, l_s, acc_s = rest[2 * pg:]
    p = pl.program_id(1)
    nstep = pl.num_programs(1) - 1
    psz = tri_ref.shape[0]
    pw = psz * D_HEADS

    @pl.when(p == 0)
    def _():
        cnt_s[...] = jnp.zeros_like(cnt_s)
        m_s[...] = jnp.full_like(m_s, NEG_INF)
        l_s[...] = jnp.zeros_like(l_s)
        acc_s[...] = jnp.zeros_like(acc_s)

    def process(k, v, key, bias):
        kf = k.reshape(pw, HEAD_DIM).astype(BF16)
        vf = v.reshape(pw, HEAD_DIM).astype(BF16)
        sel, cnt = _select(key, thr_ref[0][:, 0:1], rem_ref[0][:, 0:1], cnt_s[...], tri_ref[...])
        cnt_s[...] = cnt
        sel01 = jnp.where(sel, 1.0, 0.0).astype(BF16)
        selx = _dot(jnp.concatenate([sel01] * D_HEADS, axis=0), expand_ref[...])
        s = _dot_nt(q_ref[0], kf) * (LOG2E * HEAD_DIM ** -0.5) + bias
        s = jnp.where(selx > 0.5, s, NEG_INF)
        m_old = m_s[...]
        m_new = jnp.maximum(m_old, jnp.max(s, axis=1, keepdims=True))
        m_safe = jnp.where(m_new == NEG_INF, 0.0, m_new)
        pr = jnp.exp2(s - m_safe)
        alpha = jnp.exp2(m_old - m_safe)
        l_s[...] = alpha * l_s[...] + jnp.sum(pr, axis=1, keepdims=True)
        acc_s[...] = alpha * acc_s[...] + _dot(pr.astype(BF16), vf)
        m_s[...] = m_new

    @pl.when(p < nstep)
    def _():
        for i in range(pg):
            process(k_refs[i][0, 0, :, 0], v_refs[i][0, 0, :, 0], keys_ref[0, :, i * psz:(i + 1) * psz],
                    bias_ref[:, i * pw:(i + 1) * pw])

    @pl.when(p == nstep)
    def _():
        process(kn_ref[0, :, 0], vn_ref[0, :, 0], keys_new_ref[0], bias_new_ref[...])
        o_ref[0] = acc_s[...] / l_s[...]


def dsa_step_attn(page_table, qm, keys, thr, rem, biasx, cache_kv, kv_new, expand, tri, layer, pg, p_far):
    nseq, npage = page_table.shape
    psz = cache_kv.shape[2]
    rows = keys.shape[1]
    qr = qm.shape[1]
    pw = psz * D_HEADS
    nstep = npage // pg
    seqb = lambda n, p, pt: (n, 0, 0)
    fix = lambda n, p, pt: (0, 0)
    stp = lambda p: jnp.minimum(p, nstep - 1)

    def page_spec(i, part):
        return pl.BlockSpec((1, 1, psz, 1, D_HEADS, HEAD_DIM),
                            lambda n, p, pt: (layer, pt[n * npage + stp(p) * pg + i], 0, part, 0, 0))

    def new_spec(part):
        return pl.BlockSpec((1, psz, 1, D_HEADS, HEAD_DIM), lambda n, p, pt: (n, 0, part, 0, 0))

    grid_spec = pltpu.PrefetchScalarGridSpec(
        num_scalar_prefetch=1, grid=(nseq, nstep + 1),
        in_specs=[pl.BlockSpec((1, qr, HEAD_DIM), seqb),
                  pl.BlockSpec((1, rows, pg * psz), lambda n, p, pt: (n, 0, stp(p))),
                  pl.BlockSpec((1, rows, psz), lambda n, p, pt: (n, 0, npage)),
                  pl.BlockSpec((1, rows, LANES), seqb), pl.BlockSpec((1, rows, LANES), seqb),
                  pl.BlockSpec((qr, pg * pw), lambda n, p, pt: (0, jnp.maximum(stp(p), p_far) - p_far)),
                  pl.BlockSpec((qr, pw), lambda n, p, pt: (0, npage - p_far * pg))] +
                 [page_spec(i, 0) for i in range(pg)] + [page_spec(i, 1) for i in range(pg)] +
                 [new_spec(0), new_spec(1), pl.BlockSpec(expand.shape, fix), pl.BlockSpec(tri.shape, fix)],
        out_specs=pl.BlockSpec((1, qr, HEAD_DIM), seqb),
        scratch_shapes=[pltpu.VMEM((rows, 1), F32), pltpu.VMEM((qr, 1), F32), pltpu.VMEM((qr, 1), F32),
                        pltpu.VMEM((qr, HEAD_DIM), F32)])
    return pl.pallas_call(
        functools.partial(_dsa_step_attn_kernel, pg=pg), grid_spec=grid_spec,
        out_shape=jax.ShapeDtypeStruct((nseq, qr, HEAD_DIM), F32),
        compiler_params=_cparams("parallel", "arbitrary"), name="dsa_step_attn")(
            page_table.reshape(-1), qm, keys, keys, thr, rem, biasx, biasx, *([cache_kv] * (2 * pg)),
            kv_new, kv_new, expand, tri)


def _prep_ab_w(w):
    d = w.shape[0]
    aq, ak, av, bq, bk, bv, bg = _split_cols(w, [A_WIDTH] * 3 + [B_QK_WIDTH] * 2 + [B_V_WIDTH] * 2)
    half = B_KEY_DIM // 2

    def rot(m):
        return m.reshape(d, B_HEADS, 2, half)[:, :, ::-1, :].reshape(d, B_QK_WIDTH)

    pad = jnp.zeros((d, B_QK_PAD - B_QK_WIDTH), w.dtype)
    cols = [aq, ak, av, bq, pad, rot(bq), pad, bk, pad, rot(bk), pad, bv, bg]
    return jnp.concatenate(cols, axis=1).astype(BF16)


AB_SEGS = ((0, 3 * A_WIDTH, F32), (3 * A_WIDTH, 4 * B_QK_PAD, F32),
           (3 * A_WIDTH + 4 * B_QK_PAD, B_V_WIDTH, F32), (3 * A_WIDTH + 4 * B_QK_PAD + B_V_WIDTH, B_V_WIDTH, F32))


def _split_cols(x, widths):
    parts, start = [], 0
    for wd in widths:
        parts.append(x[..., start:start + wd])
        start += wd
    return parts


def _prep_cd_w(w):
    d = w.shape[0]
    cu, dq, dk, dv, iq, iw, ik = _split_cols(w, [C_WIDTH, D_WIDTH, D_WIDTH, D_WIDTH, IDX_HEADS * IDX_DIM, IDX_HEADS, IDX_DIM])
    pad = jnp.zeros((d, LANES - IDX_DIM - IDX_HEADS), w.dtype)
    return jnp.concatenate([cu, dq, dk, dv, iq, ik, iw, pad], axis=1).astype(BF16)


CD_SEGS = ((0, C_WIDTH, F32), (C_WIDTH, D_WIDTH, BF16), (C_WIDTH + D_WIDTH, 2 * D_WIDTH, F32),
           (C_WIDTH + 3 * D_WIDTH, IDX_HEADS * IDX_DIM, BF16), (C_WIDTH + 3 * D_WIDTH + IDX_HEADS * IDX_DIM, LANES, F32))


def _bias_by_distance(t5_table, dist):
    onehot = (_t5_bucket(dist)[..., None] == jnp.arange(T5_BUCKETS)).astype(F32)
    return jnp.einsum('...b,bh->...h', onehot, t5_table.astype(F32), precision=lax.Precision.HIGHEST)


def _dil_group_bias(t5_table, g, dil, nkey):
    return _bias_by_distance(t5_table, jnp.arange(nkey + 1) * dil)[:, g * A_GROUP_HEADS:(g + 1) * A_GROUP_HEADS]


def _dil_prompt_bias(t5_table, g, dil, nkey):
    bias = _dil_group_bias(t5_table, g, dil, nkey)
    ext = jnp.concatenate([jnp.broadcast_to(bias[:1], (nkey - 1, A_GROUP_HEADS)), bias,
                           jnp.broadcast_to(bias[-1:], (nkey - 1, A_GROUP_HEADS))], axis=0)[::-1]
    rows = [lax.slice_in_dim(ext, nkey - 1 - qi, nkey - 1 - qi + 2 * nkey, axis=0) for qi in range(nkey)]
    return jnp.moveaxis(jnp.stack(rows, axis=0), -1, 0)


def _dil_step_bias(t5_table, g, dil, nkey, n_buf, n_new, rows, pad_new):
    assert n_buf == nkey * dil
    rev = _dil_group_bias(t5_table, g, dil, nkey)[::-1]
    width = n_buf + pad_new
    out = []
    for s in range(rows):
        if s < n_new:
            blockc = jnp.full((nkey + 1, dil, A_GROUP_HEADS), NEG_INF, F32).at[:, s % dil, :].set(rev)
            flat = blockc.reshape((nkey + 1) * dil, A_GROUP_HEADS)
            front = s - s % dil
            row = jnp.pad(flat, ((front, width - front - flat.shape[0]), (0, 0)), constant_values=NEG_INF)
        else:
            row = jnp.full((width, A_GROUP_HEADS), NEG_INF, F32).at[0].set(0.0)
        out.append(row)
    return jnp.stack(out, axis=0).transpose(2, 0, 1).reshape(A_GROUP_HEADS * rows, width)


def _dsa_prompt_table(t5_table):
    d = (DSA_DCAP + LANES - 1) - jnp.arange(DSA_TAB)
    return _bias_by_distance(t5_table, d)[:, :D_HEADS].T * LOG2E


def _tri(n):
    r = np.arange(n)
    return jnp.asarray((r[:, None] <= r[None, :]).astype(np.float32)).astype(BF16)


def _ffn_and_ple(h, p, i, prm, tm, final):
    h = moe(h, prm['norm_ffn'][i], prm['moe_wr'][i], prm['moe_br'][i], prm['moe_wg'][i], prm['moe_wu'][i],
            prm['moe_wd'][i], min(2 * tm, h.shape[0]))
    return ple(h, p, prm['norm_ple'][i], prm['ple_wg'][i], prm['ple_wp'][i], prm['norm_final'] if final else None, tm)


def _layer_ab_prompt(h, prm, j, tm):
    seq = h.shape[0]
    aqkv, bqk, bv, bg = norm_matmul(h, prm['norm_mix'][2 * j], prm['ab_w'][j], AB_SEGS, tm)
    os_, ls, bufs = [], [], []
    for g, (win, dil) in enumerate(A_PATTERNS):
        nkey = win // dil
        assert seq % (dil * nkey) == 0
        o, lse = dil_prompt(aqkv, _dil_prompt_bias(prm['t5_table'], g, dil, nkey), g, dil)
        os_.append(o)
        ls.append(lse)
        kcols = aqkv[seq - min(win, seq):, A_WIDTH + g * LANES:A_WIDTH + (g + 1) * LANES]
        vcols = aqkv[seq - min(win, seq):, 2 * A_WIDTH + g * LANES:2 * A_WIDTH + (g + 1) * LANES]
        bufs.append(jnp.stack([kcols, vcols], axis=1).reshape(1, -1, 2, A_GROUP_HEADS, HEAD_DIM))
    chunk = B_CHUNK if seq % B_CHUNK == 0 else seq
    cosv, sinv = _rope_tables(jnp.arange(seq))
    s0 = jnp.zeros((1, B_HEADS, B_KEY_DIM, B_VAL_DIM), F32)
    b_out, s_new = retention(bqk, cosv, sinv, bv, bg, _retention_tables(chunk, chunk), prm['ab_gn_gain'][j], s0,
                             chunk, seq // chunk)
    h = ab_out(h, os_, ls, b_out, prm['ab_w_out'][j], tm)
    return h, bufs, s_new


def _layer_ab_sample(h, prm, j, past, nseq, n_new, pos0):
    rows = 8
    aqkv, bqk, bv, bg = norm_matmul(h, prm['norm_mix'][2 * j], prm['ab_w'][j], AB_SEGS, h.shape[0])
    a3 = aqkv.reshape(nseq, n_new, 3 * A_WIDTH)
    pad_rows = lambda x, r: jnp.pad(x, ((0, 0), (0, r - x.shape[1]), (0, 0)))
    os_, ls, bufs = [], [], []
    for g, (win, dil) in enumerate(A_PATTERNS):
        nkey = win // dil
        buf = past['a'][g]
        n_buf = buf.shape[2]
        cs = lambda part: slice(part * A_WIDTH + g * LANES, part * A_WIDTH + (g + 1) * LANES)
        q = pad_rows(a3[:, :, cs(0)], rows)
        kv_new = jnp.concatenate([a3[:, :, cs(1)], a3[:, :, cs(2)]], axis=2)
        bias = _dil_step_bias(prm['t5_table'], g, dil, nkey, n_buf, n_new, rows, LANES)
        o, lse = dil_step(q, buf.reshape(buf.shape[0], nseq, n_buf, 2 * LANES), pad_rows(kv_new, LANES), bias, j)
        os_.append(o[:, :n_new].reshape(nseq * n_new, LANES))
        ls.append(lse[:, :n_new].reshape(nseq * n_new, LANES))
        kv_all = jnp.concatenate([buf[j].reshape(nseq, n_buf, 2 * LANES), kv_new], axis=1)
        bufs.append(kv_all[:, n_new:].reshape(nseq, n_buf, 2, A_GROUP_HEADS, HEAD_DIM))
    chunk = B_CHUNK if n_new % B_CHUNK == 0 else n_new
    assert chunk == n_new
    cpad = 16
    cosv, sinv = _rope_tables(pos0 + jnp.arange(cpad))
    padc = lambda x: jnp.pad(x.reshape(nseq, n_new, -1), ((0, 0), (0, cpad - n_new), (0, 0))).reshape(nseq * cpad, -1)
    b_out, s_new = retention(padc(bqk), cosv, sinv, padc(bv), padc(bg), _retention_tables(chunk, cpad),
                             prm['ab_gn_gain'][j], past['b'][j].astype(F32), cpad, 1)
    b_out = b_out.reshape(nseq, cpad, -1)[:, :n_new].reshape(nseq * n_new, -1)
    h = ab_out(h, os_, ls, b_out, prm['ab_w_out'][j], h.shape[0])
    return h, bufs, s_new


def _cd_project(h, prm, j, tm):
    return norm_matmul(h, prm['norm_mix'][2 * j + 1], prm['cd_w'][j], CD_SEGS, tm)


def _layer_cd_prompt(h, prm, j, tm):
    seq = h.shape[0]
    cu, dq, kv, iq, misc = _cd_project(h, prm, j, tm)
    x0 = jnp.zeros((1, C_FLAT), F32)
    c_out, xr, xi = s5(cu, prm['s5'][j], x0, x0, 1, min(256, seq))
    sc = jnp.stack([xr.reshape(C_GROUPS, C_STATE), xi.reshape(C_GROUPS, C_STATE)], axis=-1)[None]
    n_sel = min(IDX_TOPK, seq // 4)
    heads_first = lambda x: x.reshape(seq, D_HEADS, HEAD_DIM).transpose(1, 0, 2)
    kb = kv[:, :D_WIDTH].astype(BF16).T
    vb = heads_first(kv[:, D_WIDTH:].astype(BF16))
    ikt = misc[:, :IDX_DIM].astype(BF16).T
    bias_tiles = dsa_bias_tiles(_dsa_prompt_table(prm['t5_table']))
    d_out = dsa_prompt(heads_first(dq), iq, misc, ikt, kb, vb, bias_tiles, _tri(DSA_TRI), n_sel)
    h = cd_out(h, c_out, d_out, prm['cd_w_out'][j], tm)
    return h, sc, kv, misc[:, :IDX_DIM]


def _layer_cd_sample(h, prm, j, past, nseq, n_new):
    rows = 8
    cu, dq, kv, iq, misc = _cd_project(h, prm, j, h.shape[0])
    tmaj = lambda x: x.reshape(nseq, n_new, -1).transpose(1, 0, 2).reshape(nseq * n_new, -1)
    x0 = past['c'][j].astype(F32)
    c_out, xr, xi = s5(tmaj(cu), prm['s5'][j], x0[..., 0].reshape(nseq, C_FLAT), x0[..., 1].reshape(nseq, C_FLAT),
                       nseq, n_new)
    c_out = c_out.reshape(n_new, nseq, -1).transpose(1, 0, 2).reshape(nseq * n_new, -1)
    sc = jnp.stack([xr.reshape(nseq, C_GROUPS, C_STATE), xi.reshape(nseq, C_GROUPS, C_STATE)], axis=-1)

    page_table = past['page_table']
    cache_kv, cache_kidx = past['d_kv'], past['d_kidx']
    npage = page_table.shape[1]
    psz = cache_kv.shape[2]
    n_past = npage * psz
    n_sel = min(IDX_TOPK, (n_past + n_new) // 4)
    pad_rows = lambda x, r: jnp.pad(x, ((0, 0), (0, r - x.shape[1]), (0, 0)))
    ik_new = misc[:, :IDX_DIM]
    iqm = pad_rows(iq.reshape(nseq, n_new, IDX_HEADS * IDX_DIM), rows).reshape(nseq, rows, IDX_HEADS, IDX_DIM)
    iqm = iqm.transpose(0, 2, 1, 3).reshape(nseq, IDX_HEADS * rows, IDX_DIM)
    iwm = pad_rows(misc[:, IDX_DIM:IDX_DIM + IDX_HEADS].reshape(nseq, n_new, IDX_HEADS), rows)
    iwm = jnp.broadcast_to(iwm.transpose(0, 2, 1).reshape(nseq, IDX_HEADS * rows, 1), (nseq, IDX_HEADS * rows, LANES))
    pg_k = math.gcd(npage, 16)
    keys = dsa_step_keys(page_table, iqm, iwm, cache_kidx, pad_rows(ik_new.reshape(nseq, n_new, IDX_DIM), psz), j, n_new,
                         pg_k)
    ncol = keys.shape[2]
    thr, rem = kth_rows(keys.reshape(nseq * rows, ncol), ncol // 3 if (ncol // LANES) % 3 == 0 else LANES, n_sel,
                        min(LANES, nseq * rows))
    pg = math.gcd(npage, 8)
    p_far = max((n_past - T5_MAX_DIST) // (pg * psz) - 1, 0)
    k_lo = p_far * pg * psz
    qpos = n_past + jnp.minimum(jnp.arange(rows), n_new - 1)
    dist = qpos[:, None] - (k_lo + jnp.arange(ncol - k_lo))[None, :]
    bias = _bias_by_distance(prm['t5_table'], dist)[..., :D_HEADS] * LOG2E
    bias = jnp.repeat(bias.transpose(2, 0, 1), D_HEADS, axis=2)
    col_head = jnp.arange(bias.shape[2]) % D_HEADS
    biasx = jnp.where(col_head[None, None, :] == jnp.arange(D_HEADS)[:, None, None], bias, NEG_INF)
    biasx = biasx.reshape(D_HEADS * rows, -1)
    qm = pad_rows(dq.reshape(nseq, n_new, D_WIDTH), rows).reshape(nseq, rows, D_HEADS, HEAD_DIM)
    qm = qm.transpose(0, 2, 1, 3).reshape(nseq, D_HEADS * rows, HEAD_DIM)
    kv_new = jnp.pad(kv.reshape(nseq, n_new, 2, D_HEADS, HEAD_DIM), ((0, 0), (0, psz - n_new), (0, 0), (0, 0), (0, 0)))
    expand = jnp.asarray(np.arange(psz)[:, None] == (np.arange(psz * D_HEADS)[None, :] // D_HEADS), BF16)
    d_out = dsa_step_attn(page_table, qm, keys, thr.reshape(nseq, rows, LANES), rem.reshape(nseq, rows, LANES), biasx,
                          cache_kv, kv_new, expand, _tri(psz), j, pg, p_far)
    d_out = d_out.reshape(nseq, D_HEADS, rows, HEAD_DIM).transpose(0, 2, 1, 3)[:, :n_new].reshape(nseq * n_new, D_WIDTH)
    h = cd_out(h, c_out, d_out, prm['cd_w_out'][j], h.shape[0])
    return h, sc, kv, ik_new


def _trunk(x, p, prm, past, nseq, n_new, pos0, tm):
    depth = p.shape[0]
    h = x
    a_new = [[] for _ in A_PATTERNS]
    b_new, c_new, kv_new, kidx_new = [], [], [], []
    for i in range(depth):
        j = i // 2
        if i % 2 == 0:
            if past is None:
                h, bufs, sb = _layer_ab_prompt(h, prm, j, tm)
            else:
                h, bufs, sb = _layer_ab_sample(h, prm, j, past, nseq, n_new, pos0)
            for g in range(len(A_PATTERNS)):
                a_new[g].append(bufs[g])
            b_new.append(sb)
        else:
            if past is None:
                h, sc, kv, kidx = _layer_cd_prompt(h, prm, j, tm)
            else:
                h, sc, kv, kidx = _layer_cd_sample(h, prm, j, past, nseq, n_new)
            c_new.append(sc)
            kv_new.append(kv.reshape(nseq, n_new, 2, D_HEADS, HEAD_DIM))
            kidx_new.append(kidx.reshape(nseq, n_new, IDX_DIM))
        h = _ffn_and_ple(h, p[i], i, prm, tm, i == depth - 1)
    stack = lambda t: jnp.stack(t, axis=0)
    return h, tuple(stack(t) for t in a_new), stack(b_new), stack(c_new), stack(kv_new), stack(kidx_new)


def kernel(x_prompt, x_sample, p_prompt, p_sample, cache_a_kv0, cache_a_kv1, cache_a_kv2, state_b, state_c, cache_d_kv, cache_d_kidx, page_table, t5_table, norm_mix, norm_ffn, norm_ple, norm_final, ab_w_in, ab_w_out, ab_gn_gain, cd_w_in, cd_w_out, c_lambda_re, c_lambda_im, c_log_dt, c_b_re, c_b_im, c_c_re, c_c_im, c_d, c_w_glu, c_b_glu, moe_w_group, moe_b_group, moe_w_expert, moe_b_expert, moe_w_gate, moe_w_up, moe_w_down, ple_w_gate, ple_w_proj):
    depth, d_model = norm_mix.shape
    n_ab, n_cd = ab_w_in.shape[0], cd_w_in.shape[0]
    rpad = jnp.zeros((depth, d_model, LANES - MOE_GROUPS - MOE_EXPERTS), F32)
    prm = dict(
        t5_table=t5_table, norm_mix=norm_mix, norm_ffn=norm_ffn, norm_ple=norm_ple, norm_final=norm_final,
        ab_w=[_prep_ab_w(ab_w_in[j]) for j in range(n_ab)], ab_w_out=ab_w_out.astype(BF16), ab_gn_gain=ab_gn_gain,
        cd_w=[_prep_cd_w(cd_w_in[j]) for j in range(n_cd)], cd_w_out=cd_w_out.astype(BF16),
        s5=[_s5_prep(c_lambda_re[j], c_lambda_im[j], c_log_dt[j], c_b_re[j], c_b_im[j], c_c_re[j], c_c_im[j],
                     c_d[j], c_w_glu[j], c_b_glu[j]) for j in range(n_cd)],
        moe_wr=jnp.concatenate([moe_w_group, moe_w_expert.reshape(depth, d_model, MOE_EXPERTS), rpad],
                               axis=2).astype(BF16),
        moe_br=jnp.concatenate([moe_b_group, moe_b_expert.reshape(depth, MOE_EXPERTS),
                                jnp.zeros((depth, LANES - MOE_GROUPS - MOE_EXPERTS), F32)], axis=1)[:, None, :],
        moe_wg=moe_w_gate.astype(BF16), moe_wu=moe_w_up.astype(BF16), moe_wd=moe_w_down.astype(BF16),
        ple_wg=ple_w_gate.astype(BF16), ple_wp=ple_w_proj.astype(BF16))

    bsz, seq, _ = x_prompt.shape
    assert bsz == 1
    y_p, a_p, sb_p, sc_p, dkv_p, dki_p = _trunk(x_prompt[0], p_prompt[:, 0], prm, None, 1, seq, 0, min(256, seq))
    y_p = y_p[None]

    nseq, n_new, _ = x_sample.shape
    past = dict(a=(cache_a_kv0, cache_a_kv1, cache_a_kv2), b=state_b, c=state_c, d_kv=cache_d_kv, d_kidx=cache_d_kidx,
                page_table=page_table)
    n_past = page_table.shape[1] * cache_d_kv.shape[2]
    rows_s = nseq * n_new
    y_s, a_s, sb_s, sc_s, dkv_s, dki_s = _trunk(x_sample.reshape(rows_s, d_model),
                                                p_sample.reshape(depth, rows_s, -1), prm, past, nseq, n_new, n_past,
                                                rows_s)
    y_s = y_s.reshape(nseq, n_new, d_model)
    return (y_p, y_s, a_p[0], a_s[0], a_p[1], a_s[1], a_p[2], a_s[2], sb_p, sb_s, sc_p, sc_s, dkv_p, dkv_s, dki_p, dki_s)
```

```python
import functools
import math

import numpy as np
import jax
import jax.numpy as jnp
from jax import lax
from jax.experimental import pallas as pl
from jax.experimental.pallas import tpu as pltpu

F32 = jnp.float32
BF16 = jnp.bfloat16
I32 = jnp.int32

EPS = 1e-6
HEAD_DIM = 64
A_PATTERNS = ((128, 1), (512, 4), (2048, 16))
A_GROUP_HEADS = 2
A_WIDTH = 384
B_HEADS = 5
B_KEY_DIM = 64
B_VAL_DIM = 128
B_QK_WIDTH = 320
B_QK_PAD = 384
B_V_WIDTH = 640
B_CHUNK = 128
ROPE_BASE = 10000.0
C_GROUPS = 32
C_GROUP_SIZE = 16
C_WIDTH = 512
C_STATE = 64
C_FLAT = C_GROUPS * C_STATE
D_HEADS = 8
D_WIDTH = 512
IDX_HEADS = 4
IDX_DIM = 64
IDX_TOPK = 256
IDX_SCALE = (IDX_HEADS * IDX_DIM) ** -0.5
T5_BUCKETS = 32
T5_MAX_DIST = 2048
MOE_GROUPS = 4
MOE_GROUP_EXPERTS = 4
MOE_EXPERTS = 16
D_EXPERT = 512
LANES = 128
INT_MIN = -(2 ** 31)
NEG_INF = float("-inf")
LOG2E = math.log2(math.e)

VMEM_LIMIT_BYTES = 52 * 1024 * 1024

DSA_TQ = 256
DSA_TK = 1024
DSA_BQ = 128
DSA_BK = 512
DSA_DCAP = T5_MAX_DIST + DSA_BK
DSA_TAB = DSA_DCAP + DSA_BK + 2 * LANES
DSA_TRI = 256
KTH_ROWS = 128


def _cparams(*sem):
    return pltpu.CompilerParams(dimension_semantics=sem, vmem_limit_bytes=VMEM_LIMIT_BYTES)


def _rms(x, g):
    return x * lax.rsqrt(jnp.mean(x * x, axis=-1, keepdims=True) + EPS) * g


def _dot(a, b):
    return jnp.dot(a, b, preferred_element_type=F32)


def _dot_nt(a, b):
    return lax.dot_general(a, b, (((1,), (1,)), ((), ())), preferred_element_type=F32)


def _dot_tn(a, b):
    return lax.dot_general(a, b, (((0,), (0,)), ((), ())), preferred_element_type=F32)


def _sigmoid(x):
    return 1.0 / (1.0 + jnp.exp(-x))


def _t5_bucket(dist):
    max_exact = T5_BUCKETS // 2
    d = jnp.maximum(dist, 0)
    scaled = jnp.log(jnp.maximum(d, 1).astype(F32) / max_exact) / math.log(T5_MAX_DIST / max_exact)
    large = jnp.minimum(max_exact + (scaled * (T5_BUCKETS - max_exact)).astype(I32), T5_BUCKETS - 1)
    return jnp.where(d < max_exact, d, large)


def _norm_matmul_kernel(x_ref, g_ref, w_ref, *out_refs, segs):
    xn = _rms(x_ref[...], g_ref[...]).astype(BF16)
    for o_ref, (start, width) in zip(out_refs, segs):
        o_ref[...] = _dot(xn, w_ref[:, start:start + width]).astype(o_ref.dtype)


def norm_matmul(x, gain, w, segs, tm):
    t, d = x.shape
    n = w.shape[1]
    kern = functools.partial(_norm_matmul_kernel, segs=tuple((s, wd) for s, wd, _ in segs))
    return pl.pallas_call(
        kern, grid=(t // tm,),
        in_specs=[pl.BlockSpec((tm, d), lambda i: (i, 0)),
                  pl.BlockSpec((1, d), lambda i: (0, 0)),
                  pl.BlockSpec((d, n), lambda i: (0, 0))],
        out_specs=[pl.BlockSpec((tm, wd), lambda i: (i, 0)) for _, wd, _ in segs],
        out_shape=[jax.ShapeDtypeStruct((t, wd), dt) for _, wd, dt in segs],
        compiler_params=_cparams("parallel"), name="norm_matmul")(x, gain.reshape(1, d), w)


def _dil_prompt_kernel(q_ref, kc_ref, kp_ref, vc_ref, vp_ref, bias_ref, o_ref, lse_ref):
    n = pl.program_id(1)
    blk = q_ref.shape[0]
    q = q_ref[...]
    kw = jnp.concatenate([kp_ref[...], kc_ref[...]], axis=0)
    vw = jnp.concatenate([vp_ref[...], vc_ref[...]], axis=0)
    qi = lax.broadcasted_iota(I32, (blk, 2 * blk), 0)
    kj = lax.broadcasted_iota(I32, (blk, 2 * blk), 1)
    m = blk + qi - kj
    first_key = jnp.where(n > 0, 0, blk)
    mask = (m >= 0) & (m <= blk) & (kj >= first_key)
    outs, lses = [], []
    for h in range(A_GROUP_HEADS):
        cs = slice(h * HEAD_DIM, (h + 1) * HEAD_DIM)
        s = _dot_nt(q[:, cs].astype(BF16), kw[:, cs].astype(BF16)) * (HEAD_DIM ** -0.5) + bias_ref[h]
        s = jnp.where(mask, s, NEG_INF)
        mx = jnp.max(s, axis=1, keepdims=True)
        p = jnp.exp(s - mx)
        l = jnp.sum(p, axis=1, keepdims=True)
        outs.append(_dot((p / l).astype(BF16), vw[:, cs].astype(BF16)))
        lses.append(jnp.broadcast_to(mx + jnp.log(l), (blk, HEAD_DIM)))
    o_ref[...] = jnp.concatenate(outs, axis=1)
    lse_ref[...] = jnp.concatenate(lses, axis=1)


def dil_prompt(aqkv, bias_qk, g, dil):
    seq = aqkv.shape[0]
    blk = bias_qk.shape[1]
    nb = seq // (blk * dil)
    ncol = aqkv.shape[1] // LANES
    a = aqkv.reshape(seq // dil, dil * aqkv.shape[1])
    ng = len(A_PATTERNS)

    def cur(off):
        return pl.BlockSpec((blk, LANES), lambda r, n: (n, r * ncol + off + g))

    def prev(off):
        return pl.BlockSpec((blk, LANES), lambda r, n: (jnp.maximum(n - 1, 0), r * ncol + off + g))

    o, lse = pl.pallas_call(
        _dil_prompt_kernel, grid=(dil, nb),
        in_specs=[cur(0), cur(ng), prev(ng), cur(2 * ng), prev(2 * ng),
                  pl.BlockSpec(bias_qk.shape, lambda r, n: (0, 0, 0))],
        out_specs=[pl.BlockSpec((blk, LANES), lambda r, n: (n, r))] * 2,
        out_shape=[jax.ShapeDtypeStruct((seq // dil, dil * LANES), F32)] * 2,
        compiler_params=_cparams("parallel", "parallel"), name="dil_prompt")(a, a, a, a, a, bias_qk)
    return o.reshape(seq, LANES), lse.reshape(seq, LANES)


def _dil_step_kernel(q_ref, buf_ref, new_ref, bias_ref, o_ref, lse_ref):
    q = q_ref[0]
    rows = q.shape[0]
    head = lax.broadcasted_iota(I32, q.shape, 1) // HEAD_DIM
    qbd = jnp.concatenate([jnp.where(head == h, q, 0.0) for h in range(A_GROUP_HEADS)], axis=0).astype(BF16)
    kt = jnp.concatenate([buf_ref[0, 0, 0], new_ref[0, 0]], axis=1).astype(BF16)
    vt = jnp.concatenate([buf_ref[0, 0, 1], new_ref[0, 1]], axis=1).astype(BF16)
    s = _dot(qbd, kt) * (HEAD_DIM ** -0.5) + bias_ref[...]
    mx = jnp.max(s, axis=1, keepdims=True)
    p = jnp.exp(s - mx)
    l = jnp.sum(p, axis=1, keepdims=True)
    o = _dot_nt((p / l).astype(BF16), vt)
    lse = jnp.broadcast_to(mx + jnp.log(l), o.shape)
    o_ref[0] = jnp.where(head == 0, o[:rows], o[rows:])
    lse_ref[0] = jnp.where(head == 0, lse[:rows], lse[rows:])


def dil_step(q, buf, new, bias, layer):
    nseq, rows, _ = q.shape
    win = buf.shape[4]
    return pl.pallas_call(
        _dil_step_kernel, grid=(nseq,),
        in_specs=[pl.BlockSpec((1, rows, LANES), lambda n: (n, 0, 0)),
                  pl.BlockSpec((1, 1, 2, LANES, win), lambda n: (layer, n, 0, 0, 0)),
                  pl.BlockSpec((1,) + new.shape[1:], lambda n: (n, 0, 0, 0)),
                  pl.BlockSpec(bias.shape, lambda n: (0, 0))],
        out_specs=[pl.BlockSpec((1, rows, LANES), lambda n: (n, 0, 0))] * 2,
        out_shape=[jax.ShapeDtypeStruct((nseq, rows, LANES), F32)] * 2,
        compiler_params=_cparams("parallel"), name="dil_step")(q, buf, new, bias)


def _retention_kernel(bqk_ref, cos_ref, sin_ref, bv_ref, bg_ref, decay_ref, win_ref, wend_ref, gch_ref, gain_ref,
                      s0_ref, out_ref, s_ref):
    c = pl.program_id(1)

    @pl.when(c == 0)
    def _():
        s_ref[...] = s0_ref[...]

    bqk = bqk_ref[...]
    cosv, sinv = cos_ref[...], sin_ref[...]
    w = B_QK_PAD
    q = (bqk[:, 0:w] * cosv + bqk[:, w:2 * w] * sinv) * (B_KEY_DIM ** -0.5)
    k = bqk[:, 2 * w:3 * w] * cosv + bqk[:, 3 * w:4 * w] * sinv
    q_in = q * win_ref[...]
    k_end = k * wend_ref[...]
    bv, bg, gain = bv_ref[...], bg_ref[...], gain_ref[...]
    outs = []
    for h in range(B_HEADS):
        ks = slice(h * B_KEY_DIM, (h + 1) * B_KEY_DIM)
        vs = slice(h * B_VAL_DIM, (h + 1) * B_VAL_DIM)
        vh = bv[:, vs].astype(BF16)
        state = s_ref[0, h]
        intra = _dot_nt(q[:, ks].astype(BF16), k[:, ks].astype(BF16)) * decay_ref[h]
        o = _dot(intra.astype(BF16), vh) + _dot(q_in[:, ks].astype(BF16), state.astype(BF16))
        s_ref[0, h] = gch_ref[h] * state + _dot_tn(k_end[:, ks].astype(BF16), vh)
        mu = jnp.mean(o, axis=1, keepdims=True)
        var = jnp.mean((o - mu) ** 2, axis=1, keepdims=True)
        on = (o - mu) * lax.rsqrt(var + EPS) * gain[:, vs]
        g = bg[:, vs]
        outs.append(g * _sigmoid(g) * on)
    out_ref[...] = jnp.concatenate(outs, axis=1)


def retention(bqk, cosv, sinv, bv, bg, tabs, gain, s0, chunk, nchunk):
    nseq = s0.shape[0]
    rows = bqk.shape[0]
    decay, w_in, w_end, g_chunk = tabs
    row = lambda b, c: (b * nchunk + c, 0)
    pos = lambda b, c: (c, 0)
    fix2 = lambda b, c: (0, 0)
    fix3 = lambda b, c: (0, 0, 0)
    return pl.pallas_call(
        _retention_kernel, grid=(nseq, nchunk),
        in_specs=[pl.BlockSpec((chunk, 4 * B_QK_PAD), row),
                  pl.BlockSpec((chunk, B_QK_PAD), pos), pl.BlockSpec((chunk, B_QK_PAD), pos),
                  pl.BlockSpec((chunk, B_V_WIDTH), row), pl.BlockSpec((chunk, B_V_WIDTH), row),
                  pl.BlockSpec(decay.shape, fix3), pl.BlockSpec(w_in.shape, fix2), pl.BlockSpec(w_end.shape, fix2),
                  pl.BlockSpec(g_chunk.shape, fix3), pl.BlockSpec((1, B_V_WIDTH), fix2),
                  pl.BlockSpec((1,) + s0.shape[1:], lambda b, c: (b, 0, 0, 0))],
        out_specs=[pl.BlockSpec((chunk, B_V_WIDTH), row),
                   pl.BlockSpec((1,) + s0.shape[1:], lambda b, c: (b, 0, 0, 0))],
        out_shape=[jax.ShapeDtypeStruct((rows, B_V_WIDTH), F32), jax.ShapeDtypeStruct(s0.shape, F32)],
        compiler_params=_cparams("parallel", "arbitrary"), name="retention")(
            bqk, cosv, sinv, bv, bg, decay, w_in, w_end, g_chunk, gain.reshape(1, B_V_WIDTH), s0)


def _retention_tables(chunk, pad):
    lg = np.log(1.0 - 2.0 ** (-5.0 - np.arange(B_HEADS, dtype=np.float32))).astype(np.float32)
    lg = jnp.asarray(lg)
    i = jnp.arange(pad, dtype=F32)
    diff = i[:, None] - i[None, :]
    decay = jnp.where(diff >= 0, jnp.exp(jnp.maximum(diff, 0.0)[None] * lg[:, None, None]), 0.0)
    w_end = jnp.exp((chunk - 1.0 - i)[None, :] * lg[:, None])
    w_in = jnp.exp((i + 1.0)[None, :] * lg[:, None])
    g_chunk = jnp.exp(chunk * lg)

    def widen(t):
        t = jnp.repeat(t.T, B_KEY_DIM, axis=1)
        return jnp.pad(t, ((0, 0), (0, B_QK_PAD - B_QK_WIDTH)))

    return decay, widen(w_in), widen(w_end), jnp.broadcast_to(g_chunk[:, None, None], (B_HEADS, 1, B_VAL_DIM))


def _rope_tables(pos):
    half = B_KEY_DIM // 2
    freq = ROPE_BASE ** (-jnp.arange(half, dtype=F32) / half)
    ang = pos.astype(F32)[:, None] * freq[None, :]
    cos, sin = jnp.cos(ang), jnp.sin(ang)
    cos_h = jnp.concatenate([cos, cos], axis=1)
    sin_h = jnp.concatenate([-sin, sin], axis=1)
    padw = ((0, 0), (0, B_QK_PAD - B_QK_WIDTH))
    return jnp.pad(jnp.tile(cos_h, (1, B_HEADS)), padw), jnp.pad(jnp.tile(sin_h, (1, B_HEADS)), padw)


def _ab_out_kernel(h_ref, o0_ref, o1_ref, o2_ref, l0_ref, l1_ref, l2_ref, b_ref, w_ref, out_ref):
    ls = [l0_ref[...], l1_ref[...], l2_ref[...]]
    os_ = [o0_ref[...], o1_ref[...], o2_ref[...]]
    mx = jnp.maximum(jnp.maximum(ls[0], ls[1]), ls[2])
    es = [jnp.exp(l - mx) for l in ls]
    den = es[0] + es[1] + es[2]
    acc = h_ref[...]
    for g in range(3):
        acc = acc + _dot((os_[g] * (es[g] / den)).astype(BF16), w_ref[g * LANES:(g + 1) * LANES, :])
    acc = acc + _dot(b_ref[...].astype(BF16), w_ref[A_WIDTH:, :])
    out_ref[...] = acc


def ab_out(h, os_, ls, b_out, w, tm):
    t, d = h.shape
    row = lambda i: (i, 0)
    small = pl.BlockSpec((tm, LANES), row)
    return pl.pallas_call(
        _ab_out_kernel, grid=(t // tm,),
        in_specs=[pl.BlockSpec((tm, d), row)] + [small] * 6 +
                 [pl.BlockSpec((tm, B_V_WIDTH), row), pl.BlockSpec(w.shape, lambda i: (0, 0))],
        out_specs=pl.BlockSpec((tm, d), row),
        out_shape=jax.ShapeDtypeStruct((t, d), F32),
        compiler_params=_cparams("parallel"), name="ab_out")(h, *os_, *ls, b_out, w)


def _cd_out_kernel(h_ref, c_ref, d_ref, w_ref, out_ref):
    acc = h_ref[...] + _dot(c_ref[...].astype(BF16), w_ref[:C_WIDTH, :])
    out_ref[...] = acc + _dot(d_ref[...].astype(BF16), w_ref[C_WIDTH:, :])


def cd_out(h, c_out, d_out, w, tm):
    t, d = h.shape
    row = lambda i: (i, 0)
    return pl.pallas_call(
        _cd_out_kernel, grid=(t // tm,),
        in_specs=[pl.BlockSpec((tm, d), row), pl.BlockSpec((tm, C_WIDTH), row), pl.BlockSpec((tm, D_WIDTH), row),
                  pl.BlockSpec(w.shape, lambda i: (0, 0))],
        out_specs=pl.BlockSpec((tm, d), row),
        out_shape=jax.ShapeDtypeStruct((t, d), F32),
        compiler_params=_cparams("parallel"), name="cd_out")(h, c_out, d_out, w)


def _route(logits):
    lane = lax.broadcasted_iota(I32, logits.shape, 1).astype(F32)
    gmask = lane < MOE_GROUPS
    gl = jnp.where(gmask, logits, NEG_INF)
    gmax = jnp.max(gl, axis=1, keepdims=True)
    gsum = jnp.sum(jnp.where(gmask, jnp.exp(gl - gmax), 0.0), axis=1, keepdims=True)
    g_w = 1.0 / gsum
    g_idx = jnp.min(jnp.where(gl == gmax, lane, LANES), axis=1, keepdims=True)
    lo = MOE_GROUPS + MOE_GROUP_EXPERTS * g_idx
    emask = (lane >= lo) & (lane < lo + MOE_GROUP_EXPERTS)
    el = jnp.where(emask, logits, NEG_INF)
    emax = jnp.max(el, axis=1, keepdims=True)
    eexp = jnp.where(emask, jnp.exp(el - emax), 0.0)
    p = eexp / jnp.sum(eexp, axis=1, keepdims=True)
    pm = jnp.where(emask, p, -1.0)
    p1 = jnp.max(pm, axis=1, keepdims=True)
    i1 = jnp.min(jnp.where(pm == p1, lane, LANES), axis=1, keepdims=True)
    pm2 = jnp.where(lane == i1, -1.0, pm)
    p2 = jnp.max(pm2, axis=1, keepdims=True)
    i2 = jnp.min(jnp.where(pm2 == p2, lane, LANES), axis=1, keepdims=True)
    den = p1 + p2
    return jnp.where(lane == i1, g_w * (p1 / den), jnp.where(lane == i2, g_w * (p2 / den), 0.0))


def _moe_kernel(h_ref, g_ref, wr_ref, br_ref, wg_ref, wu_ref, wd_ref, out_ref, xn_s, comb_s, acc_s):
    e = pl.program_id(1)

    @pl.when(e == 0)
    def _():
        xn = _rms(h_ref[...], g_ref[...]).astype(BF16)
        xn_s[...] = xn
        comb_s[...] = _route(_dot(xn, wr_ref[...]) + br_ref[...])
        acc_s[...] = jnp.zeros_like(acc_s)

    xb = xn_s[...]
    hg = _dot(xb, wg_ref[0])
    hu = _dot(xb, wu_ref[0])
    hdn = (hg * _sigmoid(hg)) * hu
    y = _dot(hdn.astype(BF16), wd_ref[0])
    comb = comb_s[...]
    lane = lax.broadcasted_iota(I32, comb.shape, 1)
    cw = jnp.sum(jnp.where(lane == MOE_GROUPS + e, comb, 0.0), axis=1, keepdims=True)
    acc_s[...] += cw * y

    @pl.when(e == MOE_EXPERTS - 1)
    def _():
        out_ref[...] = h_ref[...] + acc_s[...]


def moe(h, gain, wr, br, wg, wu, wd, tm):
    t, d = h.shape
    row = lambda i, e: (i, 0)
    fix = lambda i, e: (0, 0)
    ex = lambda i, e: (e, 0, 0)
    return pl.pallas_call(
        _moe_kernel, grid=(t // tm, MOE_EXPERTS),
        in_specs=[pl.BlockSpec((tm, d), row), pl.BlockSpec((1, d), fix),
                  pl.BlockSpec((d, LANES), fix), pl.BlockSpec((1, LANES), fix),
                  pl.BlockSpec((1, d, D_EXPERT), ex), pl.BlockSpec((1, d, D_EXPERT), ex),
                  pl.BlockSpec((1, D_EXPERT, d), ex)],
        out_specs=pl.BlockSpec((tm, d), row),
        out_shape=jax.ShapeDtypeStruct((t, d), F32),
        scratch_shapes=[pltpu.VMEM((tm, d), BF16), pltpu.VMEM((tm, LANES), F32), pltpu.VMEM((tm, d), F32)],
        compiler_params=_cparams("parallel", "arbitrary"), name="moe")(
            h, gain.reshape(1, d), wr, br, wg, wu, wd)


def _ple_kernel(h_ref, p_ref, g_ref, wg_ref, wp_ref, *rest, final):
    h = h_ref[...]
    gate = _sigmoid(_dot(_rms(h, g_ref[...]).astype(BF16), wg_ref[...]))
    hn = h + gate * _dot(p_ref[...].astype(BF16), wp_ref[...])
    if final:
        gf_ref, out_ref = rest
        out_ref[...] = _rms(hn, gf_ref[...])
    else:
        rest[0][...] = hn


def ple(h, p, gain, wg, wp, gain_final, tm):
    t, d = h.shape
    row = lambda i: (i, 0)
    fix = lambda i: (0, 0)
    final = gain_final is not None
    in_specs = [pl.BlockSpec((tm, d), row), pl.BlockSpec((tm, p.shape[1]), row), pl.BlockSpec((1, d), fix),
                pl.BlockSpec(wg.shape, fix), pl.BlockSpec(wp.shape, fix)]
    args = [h, p, gain.reshape(1, d), wg, wp]
    if final:
        in_specs.append(pl.BlockSpec((1, d), fix))
        args.append(gain_final.reshape(1, d))
    return pl.pallas_call(
        functools.partial(_ple_kernel, final=final), grid=(t // tm,),
        in_specs=in_specs, out_specs=pl.BlockSpec((tm, d), row),
        out_shape=jax.ShapeDtypeStruct((t, d), F32),
        compiler_params=_cparams("parallel"), name="ple")(*args)


def _s5_kernel(u_ref, a_ref, bre_ref, bim_ref, cre_ref, cim_ref, d_ref, wglu_ref, bglu_ref, x0r_ref, x0i_ref,
               out_ref, xr_ref, xi_ref, bur_s, bui_s, *, nbatch, steps):
    c = pl.program_id(0)

    @pl.when(c == 0)
    def _():
        xr_ref[...] = x0r_ref[...]
        xi_ref[...] = x0i_ref[...]

    u = u_ref[...]
    ub = u.astype(BF16)
    bur_s[...] = _dot(ub, bre_ref[...])
    bui_s[...] = _dot(ub, bim_ref[...])
    ar = a_ref[0:1, :]
    ai = a_ref[1:2, :]

    def step(t, carry):
        xr, xi = carry
        r0 = pl.multiple_of(t * nbatch, nbatch)
        nr = ar * xr - ai * xi + bur_s[pl.ds(r0, nbatch), :]
        ni = ar * xi + ai * xr + bui_s[pl.ds(r0, nbatch), :]
        bur_s[pl.ds(r0, nbatch), :] = nr
        bui_s[pl.ds(r0, nbatch), :] = ni
        return nr, ni

    xr, xi = lax.fori_loop(0, steps, step, (xr_ref[...], xi_ref[...]), unroll=min(steps, 8))
    xr_ref[...] = xr
    xi_ref[...] = xi
    y = _dot(bur_s[...].astype(BF16), cre_ref[...]) - _dot(bui_s[...].astype(BF16), cim_ref[...]) + d_ref[...] * u
    z = 0.5 * y * (1.0 + jnp.tanh(math.sqrt(2.0 / math.pi) * (y + 0.044715 * (y * y * y))))
    out_ref[...] = z * _sigmoid(_dot(z.astype(BF16), wglu_ref[...]) + bglu_ref[...])


def s5(u, prm, x0r, x0i, nbatch, steps):
    rows = u.shape[0]
    a, bre, bim, cre, cim, dskip, wglu, bglu = prm
    nchunk = rows // (nbatch * steps)
    tr = nbatch * steps
    fix = lambda c: (0, 0)
    full = lambda arr: pl.BlockSpec(arr.shape, fix)
    return pl.pallas_call(
        functools.partial(_s5_kernel, nbatch=nbatch, steps=steps), grid=(nchunk,),
        in_specs=[pl.BlockSpec((tr, C_WIDTH), lambda c: (c, 0))] +
                 [full(t) for t in (a, bre, bim, cre, cim, dskip, wglu, bglu, x0r, x0i)],
        out_specs=[pl.BlockSpec((tr, C_WIDTH), lambda c: (c, 0)), full(x0r), full(x0i)],
        out_shape=[jax.ShapeDtypeStruct((rows, C_WIDTH), F32), jax.ShapeDtypeStruct(x0r.shape, F32),
                   jax.ShapeDtypeStruct(x0i.shape, F32)],
        scratch_shapes=[pltpu.VMEM((tr, C_FLAT), F32), pltpu.VMEM((tr, C_FLAT), F32)],
        compiler_params=_cparams("arbitrary"), name="s5")(u, a, bre, bim, cre, cim, dskip, wglu, bglu, x0r, x0i)


def _s5_prep(lam_re, lam_im, log_dt, b_re, b_im, c_re, c_im, d_skip, w_glu, b_glu):
    dt = jnp.exp(log_dt)[:, None]
    mag = jnp.exp(lam_re * dt)
    ab_re, ab_im = mag * jnp.cos(lam_im * dt), mag * jnp.sin(lam_im * dt)
    den = lam_re * lam_re + lam_im * lam_im
    zr = ((ab_re - 1.0) * lam_re + ab_im * lam_im) / den
    zi = (ab_im * lam_re - (ab_re - 1.0) * lam_im) / den
    bb_re = zr[..., None] * b_re - zi[..., None] * b_im
    bb_im = zr[..., None] * b_im + zi[..., None] * b_re
    eye = jnp.eye(C_GROUPS, dtype=F32)
    b_dense = lambda bb: jnp.einsum('gnc,gh->gchn', bb, eye).reshape(C_WIDTH, C_FLAT).astype(BF16)
    c_dense = lambda cc: jnp.einsum('gcn,gh->gnhc', cc, eye).reshape(C_FLAT, C_WIDTH).astype(BF16)
    a = jnp.stack([ab_re.reshape(C_FLAT), ab_im.reshape(C_FLAT)], axis=0)
    return (a, b_dense(bb_re), b_dense(bb_im), c_dense(c_re), c_dense(c_im), d_skip.reshape(1, C_WIDTH),
            w_glu.astype(BF16), b_glu.reshape(1, C_WIDTH))


def _sortable(x):
    bits = lax.bitcast_convert_type(x, I32)
    return bits ^ ((bits >> 31) & jnp.int32(0x7FFFFFFF))


def _kth_largest(keys_ref, ntiles, tw, n_sel):
    total_rows = keys_ref.shape[0]
    rows = min(total_rows, KTH_ROWS)
    thrs, rems = [], []
    for r0 in range(0, total_rows, rows):
        def count_ge(cand, r0=r0):
            def body(t, cv):
                off = pl.multiple_of(t * tw, LANES)
                kk = keys_ref[r0:r0 + rows, pl.ds(off, tw)]
                for j in range(tw // LANES):
                    cv = cv + jnp.where(kk[:, j * LANES:(j + 1) * LANES] >= cand, 1.0, 0.0)
                return cv
            cv = lax.fori_loop(0, ntiles, body, jnp.zeros((rows, LANES), F32))
            return jnp.sum(cv, axis=1, keepdims=True)

        def bit(i, thr, count_ge=count_ge):
            cand = thr + lax.shift_left(jnp.int32(1), 31 - i)
            return jnp.where(count_ge(cand) >= n_sel, cand, thr)

        thr = lax.fori_loop(0, 32, bit, jnp.full((rows, 1), INT_MIN, I32))
        thrs.append(thr)
        rems.append(jnp.where(thr == INT_MIN, 0.0, n_sel - count_ge(thr + 1)))
    if len(thrs) == 1:
        return thrs[0], rems[0]
    return jnp.concatenate(thrs, axis=0), jnp.concatenate(rems, axis=0)


def _select(key, thr, rem, cnt, tri):
    eq = key == thr
    eqb = jnp.where(eq, 1.0, 0.0).astype(BF16)
    w = tri.shape[0]
    ranks = []
    for c in range(key.shape[1] // w):
        pc = _dot(eqb[:, c * w:(c + 1) * w], tri)
        ranks.append(cnt + pc)
        cnt = cnt + pc[:, -1:]
    rank = ranks[0] if len(ranks) == 1 else jnp.concatenate(ranks, axis=1)
    sel = (key > thr) | (eq & (rank <= rem))
    return sel, cnt


def _dsa_bias_kernel(tbr_ref, out_ref):
    tq, tk = DSA_BQ, DSA_BK
    start = pl.multiple_of(DSA_DCAP - pl.program_id(0) * LANES, LANES)
    wtab = tbr_ref[:, pl.ds(start, tk + 2 * LANES)]
    for h in range(D_HEADS):
        wb = jnp.broadcast_to(wtab[h:h + 1, :], (tq, tk + 2 * LANES))
        out_ref[0, h] = pltpu.roll(wb, 1, 1, stride=1, stride_axis=0)[:, LANES:LANES + tk]


def dsa_bias_tiles(tbr):
    nd = DSA_DCAP // LANES + 1
    return pl.pallas_call(
        _dsa_bias_kernel, grid=(nd,),
        in_specs=[pl.BlockSpec(tbr.shape, lambda d: (0, 0))],
        out_specs=pl.BlockSpec((1, D_HEADS, DSA_BQ, DSA_BK), lambda d: (d, 0, 0, 0)),
        out_shape=jax.ShapeDtypeStruct((nd, D_HEADS, DSA_BQ, DSA_BK), F32),
        compiler_params=_cparams("parallel"), name="dsa_bias_tiles")(tbr)


def _dsa_prompt_kernel(q_ref, iq_ref, misc_ref, ikt_ref, kt_ref, v_ref, *rest, n_sel):
    tq, tk = DSA_TQ, DSA_TK
    nqa, nkb = tq // DSA_BQ, tk // DSA_BK
    bias_refs = rest[:nqa * nkb]
    tri_ref, o_ref, keys_s, thr_s, rem_s, cnt_s, neg_s, m_s, l_s, acc_s = rest[nqa * nkb:]
    qb = pl.program_id(0)
    kt = pl.program_id(1)
    kl = (qb * tq + tq - 1) // tk

    @pl.when(kt == 0)
    def _():
        iq = iq_ref[...]
        iw = misc_ref[:, IDX_DIM:IDX_DIM + IDX_HEADS]
        qpos = qb * tq + lax.broadcasted_iota(I32, (tq, tk), 0)

        def scores(t, carry):
            off = pl.multiple_of(t * tk, tk)
            ikt = ikt_ref[:, pl.ds(off, tk)]
            sc = jnp.zeros((tq, tk), F32)
            for h in range(IDX_HEADS):
                s = _dot(iq[:, h * IDX_DIM:(h + 1) * IDX_DIM], ikt)
                sc = sc + jnp.maximum(s, 0.0) * iw[:, h:h + 1]
            kpos = off + lax.broadcasted_iota(I32, (tq, tk), 1)
            keys_s[:, pl.ds(off, tk)] = jnp.where(kpos <= qpos, _sortable(sc * IDX_SCALE), INT_MIN)
            return carry

        lax.fori_loop(0, kl + 1, scores, 0)
        thr, rem = _kth_largest(keys_s, kl + 1, tk, n_sel)
        thr_s[...] = thr
        rem_s[...] = rem
        cnt_s[...] = jnp.zeros_like(cnt_s)
        m_s[...] = jnp.full_like(m_s, NEG_INF)
        l_s[...] = jnp.zeros_like(l_s)
        acc_s[...] = jnp.zeros_like(acc_s)

    @pl.when(kt <= kl)
    def _():
        off = pl.multiple_of(kt * tk, tk)
        sel, cnt = _select(keys_s[:, pl.ds(off, tk)], thr_s[...], rem_s[...], cnt_s[...], tri_ref[...])
        cnt_s[...] = cnt
        neg_s[...] = jnp.where(sel, 0.0, NEG_INF)
        for h in range(D_HEADS):
            bias = jnp.concatenate(
                [jnp.concatenate([bias_refs[a * nkb + b][0, h] for b in range(nkb)], axis=1) for a in range(nqa)],
                axis=0)
            s = _dot(q_ref[h], kt_ref[h * HEAD_DIM:(h + 1) * HEAD_DIM, :]) * (LOG2E * HEAD_DIM ** -0.5)
            s = s + bias + neg_s[...]
            m_old = m_s[h]
            m_new = jnp.maximum(m_old, jnp.max(s, axis=1, keepdims=True))
            m_safe = jnp.where(m_new == NEG_INF, 0.0, m_new)
            p = jnp.exp2(s - m_safe)
            alpha = jnp.exp2(m_old - m_safe)
            l_s[h] = alpha * l_s[h] + jnp.sum(p, axis=1, keepdims=True)
            acc_s[h] = alpha * acc_s[h] + _dot(p.astype(BF16), v_ref[h])
            m_s[h] = m_new

    @pl.when(kt == kl)
    def _():
        o_ref[...] = jnp.concatenate([acc_s[h] / l_s[h] for h in range(D_HEADS)], axis=1)


def dsa_prompt(q, iq, misc, ikt, kt_, v, bias_tiles, tri, n_sel):
    seq = q.shape[1]
    tq, tk = DSA_TQ, DSA_TK
    nb, nkt = seq // tq, seq // tk
    last = lambda qb: (qb * tq + tq - 1) // tk
    qrow = lambda qb, kt: (qb, 0)
    fix = lambda qb, kt: (0, 0)

    def bias_spec(a, b):
        def idx(qb, kt):
            d = (qb * tq + a * DSA_BQ - jnp.minimum(kt, last(qb)) * tk - b * DSA_BK) // LANES
            return (jnp.clip(d, 0, DSA_DCAP // LANES), 0, 0, 0)
        return pl.BlockSpec((1, D_HEADS, DSA_BQ, DSA_BK), idx)

    bias_specs = [bias_spec(a, b) for a in range(tq // DSA_BQ) for b in range(tk // DSA_BK)]
    return pl.pallas_call(
        functools.partial(_dsa_prompt_kernel, n_sel=n_sel), grid=(nb, nkt),
        in_specs=[pl.BlockSpec((D_HEADS, tq, HEAD_DIM), lambda qb, kt: (0, qb, 0)),
                  pl.BlockSpec((tq, IDX_HEADS * IDX_DIM), qrow),
                  pl.BlockSpec((tq, LANES), qrow), pl.BlockSpec(ikt.shape, fix),
                  pl.BlockSpec((D_WIDTH, tk), lambda qb, kt: (0, jnp.minimum(kt, last(qb)))),
                  pl.BlockSpec((D_HEADS, tk, HEAD_DIM), lambda qb, kt: (0, jnp.minimum(kt, last(qb)), 0))] +
                 bias_specs + [pl.BlockSpec(tri.shape, fix)],
        out_specs=pl.BlockSpec((tq, D_WIDTH), qrow),
        out_shape=jax.ShapeDtypeStruct((seq, D_WIDTH), F32),
        scratch_shapes=[pltpu.VMEM((tq, seq), I32), pltpu.VMEM((tq, 1), I32), pltpu.VMEM((tq, 1), F32),
                        pltpu.VMEM((tq, 1), F32), pltpu.VMEM((tq, tk), F32), pltpu.VMEM((D_HEADS, tq, 1), F32),
                        pltpu.VMEM((D_HEADS, tq, 1), F32), pltpu.VMEM((D_HEADS, tq, HEAD_DIM), F32)],
        compiler_params=_cparams("parallel", "arbitrary"), name="dsa_prompt")(
            q, iq, misc, ikt, kt_, v, *([bias_tiles] * len(bias_specs)), tri)


def _dsa_step_keys_kernel(pt_ref, iq_ref, iw_ref, *rest, n_new, pg):
    page_refs, new_ref, keys_ref = rest[:pg], rest[pg], rest[pg + 1]
    p = pl.program_id(1)
    npage = pl.num_programs(1) * pg
    psz = new_ref.shape[2]
    rows = keys_ref.shape[1]

    def score(kt):
        r = jnp.maximum(_dot(iq_ref[0], kt.astype(BF16)), 0.0) * iw_ref[0][:, 0:1]
        sc = r[0:rows]
        for h in range(1, IDX_HEADS):
            sc = sc + r[h * rows:(h + 1) * rows]
        return _sortable(sc * IDX_SCALE)

    kt = jnp.concatenate([page_ref[0, 0] for page_ref in page_refs], axis=1)
    keys_ref[0, :, pl.ds(pl.multiple_of(p * (pg * psz), psz), pg * psz)] = score(kt)

    @pl.when(p == 0)
    def _():
        s_i = lax.broadcasted_iota(I32, (rows, psz), 0)
        j_i = lax.broadcasted_iota(I32, (rows, psz), 1)
        ok = (j_i <= s_i) & (j_i < n_new)
        keys_ref[0, :, pl.ds(pl.multiple_of(npage * psz, psz), psz)] = jnp.where(ok, score(new_ref[0]), INT_MIN)


def dsa_step_keys(page_table, iqm, iwm, cache_kidx, ik_new, layer, n_new, pg):
    nseq, npage = page_table.shape
    psz = cache_kidx.shape[3]
    rows = iqm.shape[1] // IDX_HEADS
    ncol = (npage + 1) * psz
    seqb = lambda n, p, pt: (n, 0, 0)
    page_spec = lambda i: pl.BlockSpec((1, 1, IDX_DIM, psz),
                                       lambda n, p, pt: (layer, pt[n * npage + p * pg + i], 0, 0))
    grid_spec = pltpu.PrefetchScalarGridSpec(
        num_scalar_prefetch=1, grid=(nseq, npage // pg),
        in_specs=[pl.BlockSpec((1,) + iqm.shape[1:], seqb), pl.BlockSpec((1,) + iwm.shape[1:], seqb)] +
                 [page_spec(i) for i in range(pg)] + [pl.BlockSpec((1,) + ik_new.shape[1:], seqb)],
        out_specs=pl.BlockSpec((1, rows, ncol), seqb))
    return pl.pallas_call(
        functools.partial(_dsa_step_keys_kernel, n_new=n_new, pg=pg), grid_spec=grid_spec,
        out_shape=jax.ShapeDtypeStruct((nseq, rows, ncol), I32),
        compiler_params=_cparams("parallel", "arbitrary"), name="dsa_step_keys")(
            page_table.reshape(-1), iqm, iwm, *([cache_kidx] * pg), ik_new)


def _topk_mask_kernel(keys_ref, tri_ref, neg_ref, *, tw, n_sel):
    thr, rem = _kth_largest(keys_ref, keys_ref.shape[1] // tw, tw, n_sel)
    w = tri_ref.shape[0]

    def chunk(c, cnt):
        off = pl.multiple_of(c * w, w)
        sel, cnt = _select(keys_ref[:, pl.ds(off, w)], thr, rem, cnt, tri_ref[...])
        neg_ref[:, pl.ds(off, w)] = jnp.where(sel, 0.0, NEG_INF)
        return cnt

    lax.fori_loop(0, keys_ref.shape[1] // w, chunk, jnp.zeros(thr.shape, F32))


def topk_mask_rows(keys, tri, tw, n_sel, tr):
    rows, ncol = keys.shape
    return pl.pallas_call(
        functools.partial(_topk_mask_kernel, tw=tw, n_sel=n_sel), grid=(rows // tr,),
        in_specs=[pl.BlockSpec((tr, ncol), lambda i: (i, 0)), pl.BlockSpec(tri.shape, lambda i: (0, 0))],
        out_specs=pl.BlockSpec((tr, ncol), lambda i: (i, 0)),
        out_shape=jax.ShapeDtypeStruct((rows, ncol), F32),
        compiler_params=_cparams("parallel"), name="topk_mask_rows")(keys, tri)


def _dsa_step_attn_kernel(pt_ref, q_ref, neg_ref, neg_new_ref, bias_ref, bias_new_ref, *rest, pg):
    k_refs, v_refs = rest[:pg], rest[pg:2 * pg]
    kn_ref, vn_ref, o_ref, m_s, l_s, acc_s = rest[2 * pg:]
    p = pl.program_id(1)
    nstep = pl.num_programs(1) - 1
    rows = neg_ref.shape[1]

    @pl.when(p == 0)
    def _():
        m_s[...] = jnp.full_like(m_s, NEG_INF)
        l_s[...] = jnp.zeros_like(l_s)
        acc_s[...] = jnp.zeros_like(acc_s)

    def process(kt, vt, neg, bias):
        s = _dot(q_ref[0], kt.astype(BF16)) * (LOG2E * HEAD_DIM ** -0.5) + bias
        s = s + jnp.concatenate([neg] * D_HEADS, axis=0)
        m_old = m_s[...]
        m_new = jnp.maximum(m_old, jnp.max(s, axis=1, keepdims=True))
        m_safe = jnp.where(m_new == NEG_INF, 0.0, m_new)
        pr = jnp.exp2(s - m_safe)
        alpha = jnp.exp2(m_old - m_safe)
        l_s[...] = alpha * l_s[...] + jnp.sum(pr, axis=1, keepdims=True)
        acc_s[...] = alpha * acc_s[...] + _dot_nt(pr.astype(BF16), vt.astype(BF16))
        m_s[...] = m_new

    @pl.when(p < nstep)
    def _():
        process(jnp.concatenate([r[0, 0, 0] for r in k_refs], axis=1),
                jnp.concatenate([r[0, 0, 0] for r in v_refs], axis=1), neg_ref[0], bias_ref[...])

    @pl.when(p == nstep)
    def _():
        process(kn_ref[0, 0], vn_ref[0, 0], neg_new_ref[0], bias_new_ref[...])
        accn = acc_s[...] / l_s[...]
        head = lax.broadcasted_iota(I32, (rows, D_WIDTH), 1) // HEAD_DIM
        out = jnp.zeros((rows, D_WIDTH), F32)
        for h in range(D_HEADS):
            out = out + jnp.where(head == h, accn[h * rows:(h + 1) * rows], 0.0)
        o_ref[0] = out


def dsa_step_attn(page_table, qbd, neg, bias, cache_kvt, kvt_new, layer, pg):
    nseq, npage = page_table.shape
    psz = cache_kvt.shape[4]
    rows = neg.shape[1]
    qr = qbd.shape[1]
    nstep = npage // pg
    seqb = lambda n, p, pt: (n, 0, 0)
    stp = lambda p: jnp.minimum(p, nstep - 1)

    def page_spec(i, part):
        return pl.BlockSpec((1, 1, 1, D_WIDTH, psz),
                            lambda n, p, pt: (layer, pt[n * npage + stp(p) * pg + i], part, 0, 0))

    def new_spec(part):
        return pl.BlockSpec((1, 1, D_WIDTH, psz), lambda n, p, pt: (n, part, 0, 0))

    grid_spec = pltpu.PrefetchScalarGridSpec(
        num_scalar_prefetch=1, grid=(nseq, nstep + 1),
        in_specs=[pl.BlockSpec((1, qr, D_WIDTH), seqb),
                  pl.BlockSpec((1, rows, pg * psz), lambda n, p, pt: (n, 0, stp(p))),
                  pl.BlockSpec((1, rows, psz), lambda n, p, pt: (n, 0, npage)),
                  pl.BlockSpec((qr, pg * psz), lambda n, p, pt: (0, stp(p))),
                  pl.BlockSpec((qr, psz), lambda n, p, pt: (0, npage))] +
                 [page_spec(i, 0) for i in range(pg)] + [page_spec(i, 1) for i in range(pg)] +
                 [new_spec(0), new_spec(1)],
        out_specs=pl.BlockSpec((1, rows, D_WIDTH), seqb),
        scratch_shapes=[pltpu.VMEM((qr, 1), F32), pltpu.VMEM((qr, 1), F32), pltpu.VMEM((qr, D_WIDTH), F32)])
    return pl.pallas_call(
        functools.partial(_dsa_step_attn_kernel, pg=pg), grid_spec=grid_spec,
        out_shape=jax.ShapeDtypeStruct((nseq, rows, D_WIDTH), F32),
        compiler_params=_cparams("parallel", "arbitrary"), name="dsa_step_attn")(
            page_table.reshape(-1), qbd, neg, neg, bias, bias, *([cache_kvt] * (2 * pg)), kvt_new, kvt_new)


def _prep_ab_w(w):
    d = w.shape[0]
    aq, ak, av, bq, bk, bv, bg = _split_cols(w, [A_WIDTH] * 3 + [B_QK_WIDTH] * 2 + [B_V_WIDTH] * 2)
    half = B_KEY_DIM // 2

    def rot(m):
        return m.reshape(d, B_HEADS, 2, half)[:, :, ::-1, :].reshape(d, B_QK_WIDTH)

    pad = jnp.zeros((d, B_QK_PAD - B_QK_WIDTH), w.dtype)
    cols = [aq, ak, av, bq, pad, rot(bq), pad, bk, pad, rot(bk), pad, bv, bg]
    return jnp.concatenate(cols, axis=1).astype(BF16)


AB_SEGS = ((0, 3 * A_WIDTH, F32), (3 * A_WIDTH, 4 * B_QK_PAD, F32),
           (3 * A_WIDTH + 4 * B_QK_PAD, B_V_WIDTH, F32), (3 * A_WIDTH + 4 * B_QK_PAD + B_V_WIDTH, B_V_WIDTH, F32))


def _split_cols(x, widths):
    parts, start = [], 0
    for wd in widths:
        parts.append(x[..., start:start + wd])
        start += wd
    return parts


def _prep_cd_w(w):
    d = w.shape[0]
    cu, dq, dk, dv, iq, iw, ik = _split_cols(w, [C_WIDTH, D_WIDTH, D_WIDTH, D_WIDTH, IDX_HEADS * IDX_DIM, IDX_HEADS, IDX_DIM])
    pad = jnp.zeros((d, LANES - IDX_DIM - IDX_HEADS), w.dtype)
    return jnp.concatenate([cu, dq, dk, dv, iq, ik, iw, pad], axis=1).astype(BF16)


CD_SEGS = ((0, C_WIDTH, F32), (C_WIDTH, D_WIDTH, BF16), (C_WIDTH + D_WIDTH, 2 * D_WIDTH, F32),
           (C_WIDTH + 3 * D_WIDTH, IDX_HEADS * IDX_DIM, BF16), (C_WIDTH + 3 * D_WIDTH + IDX_HEADS * IDX_DIM, LANES, F32))


def _bias_by_distance(t5_table, dist):
    onehot = (_t5_bucket(dist)[..., None] == jnp.arange(T5_BUCKETS)).astype(F32)
    return jnp.einsum('...b,bh->...h', onehot, t5_table.astype(F32), precision=lax.Precision.HIGHEST)


def _dil_group_bias(t5_table, g, dil, nkey):
    return _bias_by_distance(t5_table, jnp.arange(nkey + 1) * dil)[:, g * A_GROUP_HEADS:(g + 1) * A_GROUP_HEADS]


def _dil_prompt_bias(t5_table, g, dil, nkey):
    bias = _dil_group_bias(t5_table, g, dil, nkey)
    ext = jnp.concatenate([jnp.broadcast_to(bias[:1], (nkey - 1, A_GROUP_HEADS)), bias,
                           jnp.broadcast_to(bias[-1:], (nkey - 1, A_GROUP_HEADS))], axis=0)[::-1]
    rows = [lax.slice_in_dim(ext, nkey - 1 - qi, nkey - 1 - qi + 2 * nkey, axis=0) for qi in range(nkey)]
    return jnp.moveaxis(jnp.stack(rows, axis=0), -1, 0)


def _dil_step_bias(t5_table, g, dil, nkey, n_buf, n_new, rows, pad_new):
    assert n_buf == nkey * dil
    rev = _dil_group_bias(t5_table, g, dil, nkey)[::-1]
    width = n_buf + pad_new
    out = []
    for s in range(rows):
        if s < n_new:
            blockc = jnp.full((nkey + 1, dil, A_GROUP_HEADS), NEG_INF, F32).at[:, s % dil, :].set(rev)
            flat = blockc.reshape((nkey + 1) * dil, A_GROUP_HEADS)
            front = s - s % dil
            row = jnp.pad(flat, ((front, width - front - flat.shape[0]), (0, 0)), constant_values=NEG_INF)
        else:
            row = jnp.full((width, A_GROUP_HEADS), NEG_INF, F32).at[0].set(0.0)
        out.append(row)
    return jnp.stack(out, axis=0).transpose(2, 0, 1).reshape(A_GROUP_HEADS * rows, width)


def _dsa_prompt_table(t5_table):
    d = (DSA_DCAP + LANES - 1) - jnp.arange(DSA_TAB)
    return _bias_by_distance(t5_table, d)[:, :D_HEADS].T * LOG2E


def _tri(n):
    r = np.arange(n)
    return jnp.asarray((r[:, None] <= r[None, :]).astype(np.float32)).astype(BF16)


def _ffn_and_ple(h, p, i, prm, tm, final):
    h = moe(h, prm['norm_ffn'][i], prm['moe_wr'][i], prm['moe_br'][i], prm['moe_wg'][i], prm['moe_wu'][i],
            prm['moe_wd'][i], min(2 * tm, h.shape[0]))
    return ple(h, p, prm['norm_ple'][i], prm['ple_wg'][i], prm['ple_wp'][i], prm['norm_final'] if final else None, tm)


def _layer_ab_prompt(h, prm, j, tm):
    seq = h.shape[0]
    aqkv, bqk, bv, bg = norm_matmul(h, prm['norm_mix'][2 * j], prm['ab_w'][j], AB_SEGS, tm)
    os_, ls, bufs = [], [], []
    for g, (win, dil) in enumerate(A_PATTERNS):
        nkey = win // dil
        assert seq % (dil * nkey) == 0
        o, lse = dil_prompt(aqkv, _dil_prompt_bias(prm['t5_table'], g, dil, nkey), g, dil)
        os_.append(o)
        ls.append(lse)
        kcols = aqkv[seq - min(win, seq):, A_WIDTH + g * LANES:A_WIDTH + (g + 1) * LANES]
        vcols = aqkv[seq - min(win, seq):, 2 * A_WIDTH + g * LANES:2 * A_WIDTH + (g + 1) * LANES]
        bufs.append(jnp.stack([kcols, vcols], axis=1).reshape(1, -1, 2, A_GROUP_HEADS, HEAD_DIM))
    chunk = B_CHUNK if seq % B_CHUNK == 0 else seq
    cosv, sinv = _rope_tables(jnp.arange(seq))
    s0 = jnp.zeros((1, B_HEADS, B_KEY_DIM, B_VAL_DIM), F32)
    b_out, s_new = retention(bqk, cosv, sinv, bv, bg, _retention_tables(chunk, chunk), prm['ab_gn_gain'][j], s0,
                             chunk, seq // chunk)
    h = ab_out(h, os_, ls, b_out, prm['ab_w_out'][j], tm)
    return h, bufs, s_new


def _layer_ab_sample(h, prm, j, past, nseq, n_new, pos0):
    rows = 8
    aqkv, bqk, bv, bg = norm_matmul(h, prm['norm_mix'][2 * j], prm['ab_w'][j], AB_SEGS, h.shape[0])
    a3 = aqkv.reshape(nseq, n_new, 3 * A_WIDTH)
    pad_rows = lambda x, r: jnp.pad(x, ((0, 0), (0, r - x.shape[1]), (0, 0)))
    os_, ls, bufs = [], [], []
    for g, (win, dil) in enumerate(A_PATTERNS):
        nkey = win // dil
        buf = past['a'][g]
        n_buf = buf.shape[2]
        cs = lambda part: slice(part * A_WIDTH + g * LANES, part * A_WIDTH + (g + 1) * LANES)
        q = pad_rows(a3[:, :, cs(0)], rows)
        kv_new = jnp.concatenate([a3[:, :, cs(1)], a3[:, :, cs(2)]], axis=2)
        bias = _dil_step_bias(prm['t5_table'], g, dil, nkey, n_buf, n_new, rows, LANES)
        buf_t = buf.transpose(0, 1, 3, 4, 5, 2).reshape(buf.shape[0], nseq, 2, LANES, n_buf)
        new_t = jnp.pad(kv_new.reshape(nseq, n_new, 2, LANES).transpose(0, 2, 3, 1),
                        ((0, 0), (0, 0), (0, 0), (0, LANES - n_new)))
        o, lse = dil_step(q, buf_t, new_t, bias, j)
        os_.append(o[:, :n_new].reshape(nseq * n_new, LANES))
        ls.append(lse[:, :n_new].reshape(nseq * n_new, LANES))
        kv_all = jnp.concatenate([buf[j].reshape(nseq, n_buf, 2 * LANES), kv_new], axis=1)
        bufs.append(kv_all[:, n_new:].reshape(nseq, n_buf, 2, A_GROUP_HEADS, HEAD_DIM))
    chunk = B_CHUNK if n_new % B_CHUNK == 0 else n_new
    assert chunk == n_new
    cpad = 16
    cosv, sinv = _rope_tables(pos0 + jnp.arange(cpad))
    padc = lambda x: jnp.pad(x.reshape(nseq, n_new, -1), ((0, 0), (0, cpad - n_new), (0, 0))).reshape(nseq * cpad, -1)
    b_out, s_new = retention(padc(bqk), cosv, sinv, padc(bv), padc(bg), _retention_tables(chunk, cpad),
                             prm['ab_gn_gain'][j], past['b'][j].astype(F32), cpad, 1)
    b_out = b_out.reshape(nseq, cpad, -1)[:, :n_new].reshape(nseq * n_new, -1)
    h = ab_out(h, os_, ls, b_out, prm['ab_w_out'][j], h.shape[0])
    return h, bufs, s_new


def _cd_project(h, prm, j, tm):
    return norm_matmul(h, prm['norm_mix'][2 * j + 1], prm['cd_w'][j], CD_SEGS, tm)


def _layer_cd_prompt(h, prm, j, tm):
    seq = h.shape[0]
    cu, dq, kv, iq, misc = _cd_project(h, prm, j, tm)
    x0 = jnp.zeros((1, C_FLAT), F32)
    c_out, xr, xi = s5(cu, prm['s5'][j], x0, x0, 1, min(256, seq))
    sc = jnp.stack([xr.reshape(C_GROUPS, C_STATE), xi.reshape(C_GROUPS, C_STATE)], axis=-1)[None]
    n_sel = min(IDX_TOPK, seq // 4)
    heads_first = lambda x: x.reshape(seq, D_HEADS, HEAD_DIM).transpose(1, 0, 2)
    kb = kv[:, :D_WIDTH].astype(BF16).T
    vb = heads_first(kv[:, D_WIDTH:].astype(BF16))
    ikt = misc[:, :IDX_DIM].astype(BF16).T
    bias_tiles = dsa_bias_tiles(_dsa_prompt_table(prm['t5_table']))
    d_out = dsa_prompt(heads_first(dq), iq, misc, ikt, kb, vb, bias_tiles, _tri(DSA_TRI), n_sel)
    h = cd_out(h, c_out, d_out, prm['cd_w_out'][j], tm)
    return h, sc, kv, misc[:, :IDX_DIM]


def _layer_cd_sample(h, prm, j, past, nseq, n_new):
    rows = 8
    cu, dq, kv, iq, misc = _cd_project(h, prm, j, h.shape[0])
    tmaj = lambda x: x.reshape(nseq, n_new, -1).transpose(1, 0, 2).reshape(nseq * n_new, -1)
    x0 = past['c'][j].astype(F32)
    c_out, xr, xi = s5(tmaj(cu), prm['s5'][j], x0[..., 0].reshape(nseq, C_FLAT), x0[..., 1].reshape(nseq, C_FLAT),
                       nseq, n_new)
    c_out = c_out.reshape(n_new, nseq, -1).transpose(1, 0, 2).reshape(nseq * n_new, -1)
    sc = jnp.stack([xr.reshape(nseq, C_GROUPS, C_STATE), xi.reshape(nseq, C_GROUPS, C_STATE)], axis=-1)

    page_table = past['page_table']
    cache_kv, cache_kidx = past['d_kv'], past['d_kidx']
    npage = page_table.shape[1]
    psz = cache_kv.shape[2]
    n_past = npage * psz
    n_sel = min(IDX_TOPK, (n_past + n_new) // 4)
    pad_rows = lambda x, r: jnp.pad(x, ((0, 0), (0, r - x.shape[1]), (0, 0)))
    ik_new = misc[:, :IDX_DIM]
    iqm = pad_rows(iq.reshape(nseq, n_new, IDX_HEADS * IDX_DIM), rows).reshape(nseq, rows, IDX_HEADS, IDX_DIM)
    iqm = iqm.transpose(0, 2, 1, 3).reshape(nseq, IDX_HEADS * rows, IDX_DIM)
    iwm = pad_rows(misc[:, IDX_DIM:IDX_DIM + IDX_HEADS].reshape(nseq, n_new, IDX_HEADS), rows)
    iwm = jnp.broadcast_to(iwm.transpose(0, 2, 1).reshape(nseq, IDX_HEADS * rows, 1), (nseq, IDX_HEADS * rows, LANES))
    kidx_t = cache_kidx.transpose(0, 1, 3, 2)
    kv_t = cache_kv.transpose(0, 1, 3, 4, 5, 2).reshape(cache_kv.shape[0], cache_kv.shape[1], 2, D_WIDTH, psz)
    lane_pad = lambda x: jnp.pad(x, ((0, 0),) * (x.ndim - 1) + ((0, psz - x.shape[-1]),))
    ik_new_t = lane_pad(ik_new.reshape(nseq, n_new, IDX_DIM).transpose(0, 2, 1))
    keys = dsa_step_keys(page_table, iqm, iwm, kidx_t, ik_new_t, j, n_new, math.gcd(npage, 16))
    ncol = keys.shape[2]
    neg = topk_mask_rows(keys.reshape(nseq * rows, ncol), _tri(LANES),
                         ncol // 3 if (ncol // LANES) % 3 == 0 else LANES, n_sel, min(LANES, nseq * rows))
    qpos = n_past + jnp.minimum(jnp.arange(rows), n_new - 1)
    dist = qpos[:, None] - jnp.arange(ncol)[None, :]
    bias = _bias_by_distance(prm['t5_table'], dist)[..., :D_HEADS] * LOG2E
    bias = bias.transpose(2, 0, 1).reshape(D_HEADS * rows, ncol)
    q = pad_rows(dq.reshape(nseq, n_new, D_WIDTH), rows)
    col_head = jnp.arange(D_WIDTH) // HEAD_DIM
    qbd = jnp.where(col_head[None, None, None, :] == jnp.arange(D_HEADS)[None, :, None, None], q[:, None], 0)
    qbd = qbd.reshape(nseq, D_HEADS * rows, D_WIDTH).astype(dq.dtype)
    kvt_new = lane_pad(kv.reshape(nseq, n_new, 2, D_WIDTH).transpose(0, 2, 3, 1))
    d_out = dsa_step_attn(page_table, qbd, neg.reshape(nseq, rows, ncol), bias, kv_t, kvt_new, j, math.gcd(npage, 8))
    d_out = d_out[:, :n_new].reshape(nseq * n_new, D_WIDTH)
    h = cd_out(h, c_out, d_out, prm['cd_w_out'][j], h.shape[0])
    return h, sc, kv, ik_new


def _trunk(x, p, prm, past, nseq, n_new, pos0, tm):
    depth = p.shape[0]
    h = x
    a_new = [[] for _ in A_PATTERNS]
    b_new, c_new, kv_new, kidx_new = [], [], [], []
    for i in range(depth):
        j = i // 2
        if i % 2 == 0:
            if past is None:
                h, bufs, sb = _layer_ab_prompt(h, prm, j, tm)
            else:
                h, bufs, sb = _layer_ab_sample(h, prm, j, past, nseq, n_new, pos0)
            for g in range(len(A_PATTERNS)):
                a_new[g].append(bufs[g])
            b_new.append(sb)
        else:
            if past is None:
                h, sc, kv, kidx = _layer_cd_prompt(h, prm, j, tm)
            else:
                h, sc, kv, kidx = _layer_cd_sample(h, prm, j, past, nseq, n_new)
            c_new.append(sc)
            kv_new.append(kv.reshape(nseq, n_new, 2, D_HEADS, HEAD_DIM))
            kidx_new.append(kidx.reshape(nseq, n_new, IDX_DIM))
        h = _ffn_and_ple(h, p[i], i, prm, tm, i == depth - 1)
    stack = lambda t: jnp.stack(t, axis=0)
    return h, tuple(stack(t) for t in a_new), stack(b_new), stack(c_new), stack(kv_new), stack(kidx_new)


def kernel(x_prompt, x_sample, p_prompt, p_sample, cache_a_kv0, cache_a_kv1, cache_a_kv2, state_b, state_c, cache_d_kv, cache_d_kidx, page_table, t5_table, norm_mix, norm_ffn, norm_ple, norm_final, ab_w_in, ab_w_out, ab_gn_gain, cd_w_in, cd_w_out, c_lambda_re, c_lambda_im, c_log_dt, c_b_re, c_b_im, c_c_re, c_c_im, c_d, c_w_glu, c_b_glu, moe_w_group, moe_b_group, moe_w_expert, moe_b_expert, moe_w_gate, moe_w_up, moe_w_down, ple_w_gate, ple_w_proj):
    depth, d_model = norm_mix.shape
    n_ab, n_cd = ab_w_in.shape[0], cd_w_in.shape[0]
    rpad = jnp.zeros((depth, d_model, LANES - MOE_GROUPS - MOE_EXPERTS), F32)
    prm = dict(
        t5_table=t5_table, norm_mix=norm_mix, norm_ffn=norm_ffn, norm_ple=norm_ple, norm_final=norm_final,
        ab_w=[_prep_ab_w(ab_w_in[j]) for j in range(n_ab)], ab_w_out=ab_w_out.astype(BF16), ab_gn_gain=ab_gn_gain,
        cd_w=[_prep_cd_w(cd_w_in[j]) for j in range(n_cd)], cd_w_out=cd_w_out.astype(BF16),
        s5=[_s5_prep(c_lambda_re[j], c_lambda_im[j], c_log_dt[j], c_b_re[j], c_b_im[j], c_c_re[j], c_c_im[j],
                     c_d[j], c_w_glu[j], c_b_glu[j]) for j in range(n_cd)],
        moe_wr=jnp.concatenate([moe_w_group, moe_w_expert.reshape(depth, d_model, MOE_EXPERTS), rpad],
                               axis=2).astype(BF16),
        moe_br=jnp.concatenate([moe_b_group, moe_b_expert.reshape(depth, MOE_EXPERTS),
                                jnp.zeros((depth, LANES - MOE_GROUPS - MOE_EXPERTS), F32)], axis=1)[:, None, :],
        moe_wg=moe_w_gate.astype(BF16), moe_wu=moe_w_up.astype(BF16), moe_wd=moe_w_down.astype(BF16),
        ple_wg=ple_w_gate.astype(BF16), ple_wp=ple_w_proj.astype(BF16))

    bsz, seq, _ = x_prompt.shape
    assert bsz == 1
    y_p, a_p, sb_p, sc_p, dkv_p, dki_p = _trunk(x_prompt[0], p_prompt[:, 0], prm, None, 1, seq, 0, min(256, seq))
    y_p = y_p[None]

    nseq, n_new, _ = x_sample.shape
    past = dict(a=(cache_a_kv0, cache_a_kv1, cache_a_kv2), b=state_b, c=state_c, d_kv=cache_d_kv, d_kidx=cache_d_kidx,
                page_table=page_table)
    n_past = page_table.shape[1] * cache_d_kv.shape[2]
    rows_s = nseq * n_new
    y_s, a_s, sb_s, sc_s, dkv_s, dki_s = _trunk(x_sample.reshape(rows_s, d_model),
                                                p_sample.reshape(depth, rows_s, -1), prm, past, nseq, n_new, n_past,
                                                rows_s)
    y_s = y_s.reshape(nseq, n_new, d_model)
    return (y_p, y_s, a_p[0], a_s[0], a_p[1], a_s[1], a_p[2], a_s[2], sb_p, sb_s, sc_p, sc_s, dkv_p, dkv_s, dki_p, dki_s)
```

```python
import functools
import math

import numpy as np
import jax
import jax.numpy as jnp
from jax import lax
from jax.experimental import pallas as pl
from jax.experimental.pallas import tpu as pltpu

F32 = jnp.float32
BF16 = jnp.bfloat16
I32 = jnp.int32

EPS = 1e-6
HEAD_DIM = 64
A_PATTERNS = ((128, 1), (512, 4), (2048, 16))
A_GROUP_HEADS = 2
A_WIDTH = 384
B_HEADS = 5
B_KEY_DIM = 64
B_VAL_DIM = 128
B_QK_WIDTH = 320
B_QK_PAD = 384
B_V_WIDTH = 640
B_CHUNK = 128
ROPE_BASE = 10000.0
C_GROUPS = 32
C_GROUP_SIZE = 16
C_WIDTH = 512
C_STATE = 64
C_FLAT = C_GROUPS * C_STATE
D_HEADS = 8
D_WIDTH = 512
IDX_HEADS = 4
IDX_DIM = 64
IDX_TOPK = 256
IDX_SCALE = (IDX_HEADS * IDX_DIM) ** -0.5
T5_BUCKETS = 32
T5_MAX_DIST = 2048
MOE_GROUPS = 4
MOE_GROUP_EXPERTS = 4
MOE_EXPERTS = 16
D_EXPERT = 512
LANES = 128
INT_MIN = -(2 ** 31)
NEG_INF = float("-inf")
LOG2E = math.log2(math.e)

VMEM_LIMIT_BYTES = 52 * 1024 * 1024

DSA_TQ = 256
DSA_TK = 1024
DSA_BQ = 128
DSA_BK = 512
DSA_DCAP = T5_MAX_DIST + DSA_BK
DSA_TAB = DSA_DCAP + DSA_BK + 2 * LANES
DSA_TRI = 256
DSA_RC = 32
KTH_ROWS = 128


def _cparams(*sem):
    return pltpu.CompilerParams(dimension_semantics=sem, vmem_limit_bytes=VMEM_LIMIT_BYTES)


def _rms(x, g):
    return x * lax.rsqrt(jnp.mean(x * x, axis=-1, keepdims=True) + EPS) * g


def _dot(a, b):
    return jnp.dot(a, b, preferred_element_type=F32)


def _dot_nt(a, b):
    return lax.dot_general(a, b, (((1,), (1,)), ((), ())), preferred_element_type=F32)


def _dot_tn(a, b):
    return lax.dot_general(a, b, (((0,), (0,)), ((), ())), preferred_element_type=F32)


def _sigmoid(x):
    return 1.0 / (1.0 + jnp.exp(-x))


def _t5_bucket(dist):
    max_exact = T5_BUCKETS // 2
    d = jnp.maximum(dist, 0)
    scaled = jnp.log(jnp.maximum(d, 1).astype(F32) / max_exact) / math.log(T5_MAX_DIST / max_exact)
    large = jnp.minimum(max_exact + (scaled * (T5_BUCKETS - max_exact)).astype(I32), T5_BUCKETS - 1)
    return jnp.where(d < max_exact, d, large)


def _norm_matmul_kernel(x_ref, g_ref, w_ref, *out_refs, segs):
    xn = _rms(x_ref[...], g_ref[...]).astype(BF16)
    for o_ref, (start, width) in zip(out_refs, segs):
        o_ref[...] = _dot(xn, w_ref[:, start:start + width]).astype(o_ref.dtype)


def norm_matmul(x, gain, w, segs, tm):
    t, d = x.shape
    n = w.shape[1]
    kern = functools.partial(_norm_matmul_kernel, segs=tuple((s, wd) for s, wd, _ in segs))
    return pl.pallas_call(
        kern, grid=(t // tm,),
        in_specs=[pl.BlockSpec((tm, d), lambda i: (i, 0)),
                  pl.BlockSpec((1, d), lambda i: (0, 0)),
                  pl.BlockSpec((d, n), lambda i: (0, 0))],
        out_specs=[pl.BlockSpec((tm, wd), lambda i: (i, 0)) for _, wd, _ in segs],
        out_shape=[jax.ShapeDtypeStruct((t, wd), dt) for _, wd, dt in segs],
        compiler_params=_cparams("parallel"), name="norm_matmul")(x, gain.reshape(1, d), w)


def _dil_prompt_kernel(q_ref, kc_ref, kp_ref, vc_ref, vp_ref, bias_ref, o_ref, lse_ref):
    n = pl.program_id(1)
    blk = q_ref.shape[0]
    q = q_ref[...]
    kw = jnp.concatenate([kp_ref[...], kc_ref[...]], axis=0)
    vw = jnp.concatenate([vp_ref[...], vc_ref[...]], axis=0)
    qi = lax.broadcasted_iota(I32, (blk, 2 * blk), 0)
    kj = lax.broadcasted_iota(I32, (blk, 2 * blk), 1)
    m = blk + qi - kj
    first_key = jnp.where(n > 0, 0, blk)
    mask = (m >= 0) & (m <= blk) & (kj >= first_key)
    outs, lses = [], []
    for h in range(A_GROUP_HEADS):
        cs = slice(h * HEAD_DIM, (h + 1) * HEAD_DIM)
        s = _dot_nt(q[:, cs].astype(BF16), kw[:, cs].astype(BF16)) * (HEAD_DIM ** -0.5) + bias_ref[h]
        s = jnp.where(mask, s, NEG_INF)
        mx = jnp.max(s, axis=1, keepdims=True)
        p = jnp.exp(s - mx)
        l = jnp.sum(p, axis=1, keepdims=True)
        outs.append(_dot((p / l).astype(BF16), vw[:, cs].astype(BF16)))
        lses.append(jnp.broadcast_to(mx + jnp.log(l), (blk, HEAD_DIM)))
    o_ref[...] = jnp.concatenate(outs, axis=1)
    lse_ref[...] = jnp.concatenate(lses, axis=1)


def dil_prompt(aqkv, bias_qk, g, dil):
    seq = aqkv.shape[0]
    blk = bias_qk.shape[1]
    nb = seq // (blk * dil)
    ncol = aqkv.shape[1] // LANES
    a = aqkv.reshape(seq // dil, dil * aqkv.shape[1])
    ng = len(A_PATTERNS)

    def cur(off):
        return pl.BlockSpec((blk, LANES), lambda r, n: (n, r * ncol + off + g))

    def prev(off):
        return pl.BlockSpec((blk, LANES), lambda r, n: (jnp.maximum(n - 1, 0), r * ncol + off + g))

    o, lse = pl.pallas_call(
        _dil_prompt_kernel, grid=(dil, nb),
        in_specs=[cur(0), cur(ng), prev(ng), cur(2 * ng), prev(2 * ng),
                  pl.BlockSpec(bias_qk.shape, lambda r, n: (0, 0, 0))],
        out_specs=[pl.BlockSpec((blk, LANES), lambda r, n: (n, r))] * 2,
        out_shape=[jax.ShapeDtypeStruct((seq // dil, dil * LANES), F32)] * 2,
        compiler_params=_cparams("parallel", "parallel"), name="dil_prompt")(a, a, a, a, a, bias_qk)
    return o.reshape(seq, LANES), lse.reshape(seq, LANES)


def _dil_step_kernel(q_ref, buf_ref, new_ref, bias_ref, o_ref, lse_ref):
    q = q_ref[0]
    rows = q.shape[0]
    head = lax.broadcasted_iota(I32, q.shape, 1) // HEAD_DIM
    qbd = jnp.concatenate([jnp.where(head == h, q, 0.0) for h in range(A_GROUP_HEADS)], axis=0).astype(BF16)
    kt = jnp.concatenate([buf_ref[0, 0, 0], new_ref[0, 0]], axis=1).astype(BF16)
    vt = jnp.concatenate([buf_ref[0, 0, 1], new_ref[0, 1]], axis=1).astype(BF16)
    s = _dot(qbd, kt) * (HEAD_DIM ** -0.5) + bias_ref[...]
    mx = jnp.max(s, axis=1, keepdims=True)
    p = jnp.exp(s - mx)
    l = jnp.sum(p, axis=1, keepdims=True)
    o = _dot_nt((p / l).astype(BF16), vt)
    lse = jnp.broadcast_to(mx + jnp.log(l), o.shape)
    o_ref[0] = jnp.where(head == 0, o[:rows], o[rows:])
    lse_ref[0] = jnp.where(head == 0, lse[:rows], lse[rows:])


def dil_step(q, buf, new, bias, layer):
    nseq, rows, _ = q.shape
    win = buf.shape[4]
    return pl.pallas_call(
        _dil_step_kernel, grid=(nseq,),
        in_specs=[pl.BlockSpec((1, rows, LANES), lambda n: (n, 0, 0)),
                  pl.BlockSpec((1, 1, 2, LANES, win), lambda n: (layer, n, 0, 0, 0)),
                  pl.BlockSpec((1,) + new.shape[1:], lambda n: (n, 0, 0, 0)),
                  pl.BlockSpec(bias.shape, lambda n: (0, 0))],
        out_specs=[pl.BlockSpec((1, rows, LANES), lambda n: (n, 0, 0))] * 2,
        out_shape=[jax.ShapeDtypeStruct((nseq, rows, LANES), F32)] * 2,
        compiler_params=_cparams("parallel"), name="dil_step")(q, buf, new, bias)


def _retention_kernel(bqk_ref, cos_ref, sin_ref, bv_ref, bg_ref, decay_ref, win_ref, wend_ref, gch_ref, gain_ref,
                      s0_ref, out_ref, s_ref):
    c = pl.program_id(1)

    @pl.when(c == 0)
    def _():
        s_ref[...] = s0_ref[...]

    bqk = bqk_ref[...]
    cosv, sinv = cos_ref[...], sin_ref[...]
    w = B_QK_PAD
    q = (bqk[:, 0:w] * cosv + bqk[:, w:2 * w] * sinv) * (B_KEY_DIM ** -0.5)
    k = bqk[:, 2 * w:3 * w] * cosv + bqk[:, 3 * w:4 * w] * sinv
    q_in = q * win_ref[...]
    k_end = k * wend_ref[...]
    bv, bg, gain = bv_ref[...], bg_ref[...], gain_ref[...]
    outs = []
    for h in range(B_HEADS):
        ks = slice(h * B_KEY_DIM, (h + 1) * B_KEY_DIM)
        vs = slice(h * B_VAL_DIM, (h + 1) * B_VAL_DIM)
        vh = bv[:, vs].astype(BF16)
        state = s_ref[0, h]
        intra = _dot_nt(q[:, ks].astype(BF16), k[:, ks].astype(BF16)) * decay_ref[h]
        o = _dot(intra.astype(BF16), vh) + _dot(q_in[:, ks].astype(BF16), state.astype(BF16))
        s_ref[0, h] = gch_ref[h] * state + _dot_tn(k_end[:, ks].astype(BF16), vh)
        mu = jnp.mean(o, axis=1, keepdims=True)
        var = jnp.mean((o - mu) ** 2, axis=1, keepdims=True)
        on = (o - mu) * lax.rsqrt(var + EPS) * gain[:, vs]
        g = bg[:, vs]
        outs.append(g * _sigmoid(g) * on)
    out_ref[...] = jnp.concatenate(outs, axis=1)


def retention(bqk, cosv, sinv, bv, bg, tabs, gain, s0, chunk, nchunk):
    nseq = s0.shape[0]
    rows = bqk.shape[0]
    decay, w_in, w_end, g_chunk = tabs
    row = lambda b, c: (b * nchunk + c, 0)
    pos = lambda b, c: (c, 0)
    fix2 = lambda b, c: (0, 0)
    fix3 = lambda b, c: (0, 0, 0)
    return pl.pallas_call(
        _retention_kernel, grid=(nseq, nchunk),
        in_specs=[pl.BlockSpec((chunk, 4 * B_QK_PAD), row),
                  pl.BlockSpec((chunk, B_QK_PAD), pos), pl.BlockSpec((chunk, B_QK_PAD), pos),
                  pl.BlockSpec((chunk, B_V_WIDTH), row), pl.BlockSpec((chunk, B_V_WIDTH), row),
                  pl.BlockSpec(decay.shape, fix3), pl.BlockSpec(w_in.shape, fix2), pl.BlockSpec(w_end.shape, fix2),
                  pl.BlockSpec(g_chunk.shape, fix3), pl.BlockSpec((1, B_V_WIDTH), fix2),
                  pl.BlockSpec((1,) + s0.shape[1:], lambda b, c: (b, 0, 0, 0))],
        out_specs=[pl.BlockSpec((chunk, B_V_WIDTH), row),
                   pl.BlockSpec((1,) + s0.shape[1:], lambda b, c: (b, 0, 0, 0))],
        out_shape=[jax.ShapeDtypeStruct((rows, B_V_WIDTH), F32), jax.ShapeDtypeStruct(s0.shape, F32)],
        compiler_params=_cparams("parallel", "arbitrary"), name="retention")(
            bqk, cosv, sinv, bv, bg, decay, w_in, w_end, g_chunk, gain.reshape(1, B_V_WIDTH), s0)


def _retention_tables(chunk, pad):
    lg = np.log(1.0 - 2.0 ** (-5.0 - np.arange(B_HEADS, dtype=np.float32))).astype(np.float32)
    lg = jnp.asarray(lg)
    i = jnp.arange(pad, dtype=F32)
    diff = i[:, None] - i[None, :]
    decay = jnp.where(diff >= 0, jnp.exp(jnp.maximum(diff, 0.0)[None] * lg[:, None, None]), 0.0)
    w_end = jnp.exp((chunk - 1.0 - i)[None, :] * lg[:, None])
    w_in = jnp.exp((i + 1.0)[None, :] * lg[:, None])
    g_chunk = jnp.exp(chunk * lg)

    def widen(t):
        t = jnp.repeat(t.T, B_KEY_DIM, axis=1)
        return jnp.pad(t, ((0, 0), (0, B_QK_PAD - B_QK_WIDTH)))

    return decay, widen(w_in), widen(w_end), jnp.broadcast_to(g_chunk[:, None, None], (B_HEADS, 1, B_VAL_DIM))


def _rope_tables(pos):
    half = B_KEY_DIM // 2
    freq = ROPE_BASE ** (-jnp.arange(half, dtype=F32) / half)
    ang = pos.astype(F32)[:, None] * freq[None, :]
    cos, sin = jnp.cos(ang), jnp.sin(ang)
    cos_h = jnp.concatenate([cos, cos], axis=1)
    sin_h = jnp.concatenate([-sin, sin], axis=1)
    padw = ((0, 0), (0, B_QK_PAD - B_QK_WIDTH))
    return jnp.pad(jnp.tile(cos_h, (1, B_HEADS)), padw), jnp.pad(jnp.tile(sin_h, (1, B_HEADS)), padw)


def _ab_out_kernel(h_ref, o0_ref, o1_ref, o2_ref, l0_ref, l1_ref, l2_ref, b_ref, w_ref, out_ref):
    ls = [l0_ref[...], l1_ref[...], l2_ref[...]]
    os_ = [o0_ref[...], o1_ref[...], o2_ref[...]]
    mx = jnp.maximum(jnp.maximum(ls[0], ls[1]), ls[2])
    es = [jnp.exp(l - mx) for l in ls]
    den = es[0] + es[1] + es[2]
    acc = h_ref[...]
    for g in range(3):
        acc = acc + _dot((os_[g] * (es[g] / den)).astype(BF16), w_ref[g * LANES:(g + 1) * LANES, :])
    acc = acc + _dot(b_ref[...].astype(BF16), w_ref[A_WIDTH:, :])
    out_ref[...] = acc


def ab_out(h, os_, ls, b_out, w, tm):
    t, d = h.shape
    row = lambda i: (i, 0)
    small = pl.BlockSpec((tm, LANES), row)
    return pl.pallas_call(
        _ab_out_kernel, grid=(t // tm,),
        in_specs=[pl.BlockSpec((tm, d), row)] + [small] * 6 +
                 [pl.BlockSpec((tm, B_V_WIDTH), row), pl.BlockSpec(w.shape, lambda i: (0, 0))],
        out_specs=pl.BlockSpec((tm, d), row),
        out_shape=jax.ShapeDtypeStruct((t, d), F32),
        compiler_params=_cparams("parallel"), name="ab_out")(h, *os_, *ls, b_out, w)


def _cd_out_kernel(h_ref, c_ref, d_ref, w_ref, out_ref):
    acc = h_ref[...] + _dot(c_ref[...].astype(BF16), w_ref[:C_WIDTH, :])
    out_ref[...] = acc + _dot(d_ref[...].astype(BF16), w_ref[C_WIDTH:, :])


def cd_out(h, c_out, d_out, w, tm):
    t, d = h.shape
    row = lambda i: (i, 0)
    return pl.pallas_call(
        _cd_out_kernel, grid=(t // tm,),
        in_specs=[pl.BlockSpec((tm, d), row), pl.BlockSpec((tm, C_WIDTH), row), pl.BlockSpec((tm, D_WIDTH), row),
                  pl.BlockSpec(w.shape, lambda i: (0, 0))],
        out_specs=pl.BlockSpec((tm, d), row),
        out_shape=jax.ShapeDtypeStruct((t, d), F32),
        compiler_params=_cparams("parallel"), name="cd_out")(h, c_out, d_out, w)


def _route(logits):
    lane = lax.broadcasted_iota(I32, logits.shape, 1).astype(F32)
    gmask = lane < MOE_GROUPS
    gl = jnp.where(gmask, logits, NEG_INF)
    gmax = jnp.max(gl, axis=1, keepdims=True)
    gsum = jnp.sum(jnp.where(gmask, jnp.exp(gl - gmax), 0.0), axis=1, keepdims=True)
    g_w = 1.0 / gsum
    g_idx = jnp.min(jnp.where(gl == gmax, lane, LANES), axis=1, keepdims=True)
    lo = MOE_GROUPS + MOE_GROUP_EXPERTS * g_idx
    emask = (lane >= lo) & (lane < lo + MOE_GROUP_EXPERTS)
    el = jnp.where(emask, logits, NEG_INF)
    emax = jnp.max(el, axis=1, keepdims=True)
    eexp = jnp.where(emask, jnp.exp(el - emax), 0.0)
    p = eexp / jnp.sum(eexp, axis=1, keepdims=True)
    pm = jnp.where(emask, p, -1.0)
    p1 = jnp.max(pm, axis=1, keepdims=True)
    i1 = jnp.min(jnp.where(pm == p1, lane, LANES), axis=1, keepdims=True)
    pm2 = jnp.where(lane == i1, -1.0, pm)
    p2 = jnp.max(pm2, axis=1, keepdims=True)
    i2 = jnp.min(jnp.where(pm2 == p2, lane, LANES), axis=1, keepdims=True)
    den = p1 + p2
    return jnp.where(lane == i1, g_w * (p1 / den), jnp.where(lane == i2, g_w * (p2 / den), 0.0))


def _moe_kernel(h_ref, g_ref, wr_ref, br_ref, wg_ref, wu_ref, wd_ref, out_ref, xn_s, comb_s, acc_s):
    e = pl.program_id(1)

    @pl.when(e == 0)
    def _():
        xn = _rms(h_ref[...], g_ref[...]).astype(BF16)
        xn_s[...] = xn
        comb_s[...] = _route(_dot(xn, wr_ref[...]) + br_ref[...])
        acc_s[...] = jnp.zeros_like(acc_s)

    xb = xn_s[...]
    hg = _dot(xb, wg_ref[0])
    hu = _dot(xb, wu_ref[0])
    hdn = (hg * _sigmoid(hg)) * hu
    y = _dot(hdn.astype(BF16), wd_ref[0])
    comb = comb_s[...]
    lane = lax.broadcasted_iota(I32, comb.shape, 1)
    cw = jnp.sum(jnp.where(lane == MOE_GROUPS + e, comb, 0.0), axis=1, keepdims=True)
    acc_s[...] += cw * y

    @pl.when(e == MOE_EXPERTS - 1)
    def _():
        out_ref[...] = h_ref[...] + acc_s[...]


def moe(h, gain, wr, br, wg, wu, wd, tm):
    t, d = h.shape
    row = lambda i, e: (i, 0)
    fix = lambda i, e: (0, 0)
    ex = lambda i, e: (e, 0, 0)
    return pl.pallas_call(
        _moe_kernel, grid=(t // tm, MOE_EXPERTS),
        in_specs=[pl.BlockSpec((tm, d), row), pl.BlockSpec((1, d), fix),
                  pl.BlockSpec((d, LANES), fix), pl.BlockSpec((1, LANES), fix),
                  pl.BlockSpec((1, d, D_EXPERT), ex), pl.BlockSpec((1, d, D_EXPERT), ex),
                  pl.BlockSpec((1, D_EXPERT, d), ex)],
        out_specs=pl.BlockSpec((tm, d), row),
        out_shape=jax.ShapeDtypeStruct((t, d), F32),
        scratch_shapes=[pltpu.VMEM((tm, d), BF16), pltpu.VMEM((tm, LANES), F32), pltpu.VMEM((tm, d), F32)],
        compiler_params=_cparams("parallel", "arbitrary"), name="moe")(
            h, gain.reshape(1, d), wr, br, wg, wu, wd)


def _ple_kernel(h_ref, p_ref, g_ref, wg_ref, wp_ref, *rest, final):
    h = h_ref[...]
    gate = _sigmoid(_dot(_rms(h, g_ref[...]).astype(BF16), wg_ref[...]))
    hn = h + gate * _dot(p_ref[...].astype(BF16), wp_ref[...])
    if final:
        gf_ref, out_ref = rest
        out_ref[...] = _rms(hn, gf_ref[...])
    else:
        rest[0][...] = hn


def ple(h, p, gain, wg, wp, gain_final, tm):
    t, d = h.shape
    row = lambda i: (i, 0)
    fix = lambda i: (0, 0)
    final = gain_final is not None
    in_specs = [pl.BlockSpec((tm, d), row), pl.BlockSpec((tm, p.shape[1]), row), pl.BlockSpec((1, d), fix),
                pl.BlockSpec(wg.shape, fix), pl.BlockSpec(wp.shape, fix)]
    args = [h, p, gain.reshape(1, d), wg, wp]
    if final:
        in_specs.append(pl.BlockSpec((1, d), fix))
        args.append(gain_final.reshape(1, d))
    return pl.pallas_call(
        functools.partial(_ple_kernel, final=final), grid=(t // tm,),
        in_specs=in_specs, out_specs=pl.BlockSpec((tm, d), row),
        out_shape=jax.ShapeDtypeStruct((t, d), F32),
        compiler_params=_cparams("parallel"), name="ple")(*args)


def _s5_kernel(u_ref, a_ref, bre_ref, bim_ref, cre_ref, cim_ref, d_ref, wglu_ref, bglu_ref, x0r_ref, x0i_ref,
               out_ref, xr_ref, xi_ref, bur_s, bui_s, *, nbatch, steps):
    c = pl.program_id(0)

    @pl.when(c == 0)
    def _():
        xr_ref[...] = x0r_ref[...]
        xi_ref[...] = x0i_ref[...]

    u = u_ref[...]
    ub = u.astype(BF16)
    bur_s[...] = _dot(ub, bre_ref[...])
    bui_s[...] = _dot(ub, bim_ref[...])
    ar = a_ref[0:1, :]
    ai = a_ref[1:2, :]

    def step(t, carry):
        xr, xi = carry
        r0 = pl.multiple_of(t * nbatch, nbatch)
        nr = ar * xr - ai * xi + bur_s[pl.ds(r0, nbatch), :]
        ni = ar * xi + ai * xr + bui_s[pl.ds(r0, nbatch), :]
        bur_s[pl.ds(r0, nbatch), :] = nr
        bui_s[pl.ds(r0, nbatch), :] = ni
        return nr, ni

    xr, xi = lax.fori_loop(0, steps, step, (xr_ref[...], xi_ref[...]), unroll=min(steps, 8))
    xr_ref[...] = xr
    xi_ref[...] = xi
    y = _dot(bur_s[...].astype(BF16), cre_ref[...]) - _dot(bui_s[...].astype(BF16), cim_ref[...]) + d_ref[...] * u
    z = 0.5 * y * (1.0 + jnp.tanh(math.sqrt(2.0 / math.pi) * (y + 0.044715 * (y * y * y))))
    out_ref[...] = z * _sigmoid(_dot(z.astype(BF16), wglu_ref[...]) + bglu_ref[...])


def s5(u, prm, x0r, x0i, nbatch, steps):
    rows = u.shape[0]
    a, bre, bim, cre, cim, dskip, wglu, bglu = prm
    nchunk = rows // (nbatch * steps)
    tr = nbatch * steps
    fix = lambda c: (0, 0)
    full = lambda arr: pl.BlockSpec(arr.shape, fix)
    return pl.pallas_call(
        functools.partial(_s5_kernel, nbatch=nbatch, steps=steps), grid=(nchunk,),
        in_specs=[pl.BlockSpec((tr, C_WIDTH), lambda c: (c, 0))] +
                 [full(t) for t in (a, bre, bim, cre, cim, dskip, wglu, bglu, x0r, x0i)],
        out_specs=[pl.BlockSpec((tr, C_WIDTH), lambda c: (c, 0)), full(x0r), full(x0i)],
        out_shape=[jax.ShapeDtypeStruct((rows, C_WIDTH), F32), jax.ShapeDtypeStruct(x0r.shape, F32),
                   jax.ShapeDtypeStruct(x0i.shape, F32)],
        scratch_shapes=[pltpu.VMEM((tr, C_FLAT), F32), pltpu.VMEM((tr, C_FLAT), F32)],
        compiler_params=_cparams("arbitrary"), name="s5")(u, a, bre, bim, cre, cim, dskip, wglu, bglu, x0r, x0i)


def _s5_prep(lam_re, lam_im, log_dt, b_re, b_im, c_re, c_im, d_skip, w_glu, b_glu):
    dt = jnp.exp(log_dt)[:, None]
    mag = jnp.exp(lam_re * dt)
    ab_re, ab_im = mag * jnp.cos(lam_im * dt), mag * jnp.sin(lam_im * dt)
    den = lam_re * lam_re + lam_im * lam_im
    zr = ((ab_re - 1.0) * lam_re + ab_im * lam_im) / den
    zi = (ab_im * lam_re - (ab_re - 1.0) * lam_im) / den
    bb_re = zr[..., None] * b_re - zi[..., None] * b_im
    bb_im = zr[..., None] * b_im + zi[..., None] * b_re
    eye = jnp.eye(C_GROUPS, dtype=F32)
    b_dense = lambda bb: jnp.einsum('gnc,gh->gchn', bb, eye).reshape(C_WIDTH, C_FLAT).astype(BF16)
    c_dense = lambda cc: jnp.einsum('gcn,gh->gnhc', cc, eye).reshape(C_FLAT, C_WIDTH).astype(BF16)
    a = jnp.stack([ab_re.reshape(C_FLAT), ab_im.reshape(C_FLAT)], axis=0)
    return (a, b_dense(bb_re), b_dense(bb_im), c_dense(c_re), c_dense(c_im), d_skip.reshape(1, C_WIDTH),
            w_glu.astype(BF16), b_glu.reshape(1, C_WIDTH))


def _sortable(x):
    bits = lax.bitcast_convert_type(x, I32)
    return bits ^ ((bits >> 31) & jnp.int32(0x7FFFFFFF))


def _kth_largest(keys_ref, ntiles, tw, n_sel):
    total_rows = keys_ref.shape[0]
    rows = min(total_rows, KTH_ROWS)
    thrs, rems = [], []
    for r0 in range(0, total_rows, rows):
        def count_ge(cand, r0=r0):
            def body(t, cv):
                off = pl.multiple_of(t * tw, LANES)
                kk = keys_ref[r0:r0 + rows, pl.ds(off, tw)]
                for j in range(tw // LANES):
                    cv = cv + jnp.where(kk[:, j * LANES:(j + 1) * LANES] >= cand, 1.0, 0.0)
                return cv
            cv = lax.fori_loop(0, ntiles, body, jnp.zeros((rows, LANES), F32))
            return jnp.sum(cv, axis=1, keepdims=True)

        def bit(i, thr, count_ge=count_ge):
            cand = thr + lax.shift_left(jnp.int32(1), 31 - i)
            return jnp.where(count_ge(cand) >= n_sel, cand, thr)

        thr = lax.fori_loop(0, 32, bit, jnp.full((rows, 1), INT_MIN, I32))
        thrs.append(thr)
        rems.append(jnp.where(thr == INT_MIN, 0.0, n_sel - count_ge(thr + 1)))
    if len(thrs) == 1:
        return thrs[0], rems[0]
    return jnp.concatenate(thrs, axis=0), jnp.concatenate(rems, axis=0)


def _select(key, thr, rem, cnt, tri):
    eq = key == thr
    eqb = jnp.where(eq, 1.0, 0.0).astype(BF16)
    w = tri.shape[0]
    ranks = []
    for c in range(key.shape[1] // w):
        pc = _dot(eqb[:, c * w:(c + 1) * w], tri)
        ranks.append(cnt + pc)
        cnt = cnt + pc[:, -1:]
    rank = ranks[0] if len(ranks) == 1 else jnp.concatenate(ranks, axis=1)
    sel = (key > thr) | (eq & (rank <= rem))
    return sel, cnt


def _dsa_bias_kernel(tbr_ref, out_ref):
    tq, tk = DSA_BQ, DSA_BK
    start = pl.multiple_of(DSA_DCAP - pl.program_id(0) * LANES, LANES)
    wtab = tbr_ref[:, pl.ds(start, tk + 2 * LANES)]
    for h in range(D_HEADS):
        wb = jnp.broadcast_to(wtab[h:h + 1, :], (tq, tk + 2 * LANES))
        out_ref[0, h] = pltpu.roll(wb, 1, 1, stride=1, stride_axis=0)[:, LANES:LANES + tk]


def dsa_bias_tiles(tbr):
    nd = DSA_DCAP // LANES + 1
    return pl.pallas_call(
        _dsa_bias_kernel, grid=(nd,),
        in_specs=[pl.BlockSpec(tbr.shape, lambda d: (0, 0))],
        out_specs=pl.BlockSpec((1, D_HEADS, DSA_BQ, DSA_BK), lambda d: (d, 0, 0, 0)),
        out_shape=jax.ShapeDtypeStruct((nd, D_HEADS, DSA_BQ, DSA_BK), F32),
        compiler_params=_cparams("parallel"), name="dsa_bias_tiles")(tbr)


def _dsa_prompt_kernel(q_ref, iq_ref, misc_ref, ikt_ref, kt_ref, vt_ref, *rest, n_sel):
    tq, tk = DSA_TQ, DSA_TK
    nqa, nkb = tq // DSA_BQ, tk // DSA_BK
    bias_refs = rest[:nqa * nkb]
    tri_ref, o_ref, keys_s, thr_s, rem_s, cnt_s, neg_s, m_s, l_s, acc_s, s_s, p_s, alpha_s = rest[nqa * nkb:]
    qb = pl.program_id(0)
    kt = pl.program_id(1)
    kl = (qb * tq + tq - 1) // tk

    @pl.when(kt == 0)
    def _():
        iq = iq_ref[...]
        iw = misc_ref[:, IDX_DIM:IDX_DIM + IDX_HEADS]
        qpos = qb * tq + lax.broadcasted_iota(I32, (tq, tk), 0)

        def scores(t, carry):
            off = pl.multiple_of(t * tk, tk)
            ikt = ikt_ref[:, pl.ds(off, tk)]
            sc = jnp.zeros((tq, tk), F32)
            for h in range(IDX_HEADS):
                s = _dot(iq[:, h * IDX_DIM:(h + 1) * IDX_DIM], ikt)
                sc = sc + jnp.maximum(s, 0.0) * iw[:, h:h + 1]
            kpos = off + lax.broadcasted_iota(I32, (tq, tk), 1)
            keys_s[:, pl.ds(off, tk)] = jnp.where(kpos <= qpos, _sortable(sc * IDX_SCALE), INT_MIN)
            return carry

        lax.fori_loop(0, kl + 1, scores, 0)
        thr, rem = _kth_largest(keys_s, kl + 1, tk, n_sel)
        thr_s[...] = thr
        rem_s[...] = rem
        cnt_s[...] = jnp.zeros_like(cnt_s)
        m_s[...] = jnp.full_like(m_s, NEG_INF)
        l_s[...] = jnp.zeros_like(l_s)
        acc_s[...] = jnp.zeros_like(acc_s)

    @pl.when(kt <= kl)
    def _():
        off = pl.multiple_of(kt * tk, tk)
        sel, cnt = _select(keys_s[:, pl.ds(off, tk)], thr_s[...], rem_s[...], cnt_s[...], tri_ref[...])
        cnt_s[...] = cnt
        neg_s[...] = jnp.where(sel, 0.0, NEG_INF)
        for h in range(D_HEADS):
            sb, pb = s_s.at[h % 2], p_s.at[h % 2]
            sb[...] = _dot(q_ref[h], kt_ref[h * HEAD_DIM:(h + 1) * HEAD_DIM, :])
            for r in range(0, tq, DSA_RC):
                rs = slice(r, r + DSA_RC)
                a, ro = r // DSA_BQ, r % DSA_BQ
                bias = jnp.concatenate([bias_refs[a * nkb + b][0, h, ro:ro + DSA_RC, :] for b in range(nkb)], axis=1)
                s = sb[rs, :] * (LOG2E * HEAD_DIM ** -0.5) + bias + neg_s[rs, :]
                m_old = m_s[h, rs]
                m_new = jnp.maximum(m_old, jnp.max(s, axis=1, keepdims=True))
                m_safe = jnp.where(m_new == NEG_INF, 0.0, m_new)
                p = jnp.exp2(s - m_safe)
                alpha = jnp.exp2(m_old - m_safe)
                l_s[h, rs] = alpha * l_s[h, rs] + jnp.sum(p, axis=1, keepdims=True)
                alpha_s[h % 2, rs] = alpha
                pb[rs, :] = p.astype(BF16)
                m_s[h, rs] = m_new
            acc_s[h] = alpha_s[h % 2] * acc_s[h] + _dot_nt(pb[...], vt_ref[h * HEAD_DIM:(h + 1) * HEAD_DIM, :])

    @pl.when(kt == kl)
    def _():
        o_ref[...] = jnp.concatenate([acc_s[h] / l_s[h] for h in range(D_HEADS)], axis=1)


def dsa_prompt(q, iq, misc, ikt, kt_, v, bias_tiles, tri, n_sel):
    seq = q.shape[1]
    tq, tk = DSA_TQ, DSA_TK
    nb, nkt = seq // tq, seq // tk
    last = lambda qb: (qb * tq + tq - 1) // tk
    qrow = lambda qb, kt: (qb, 0)
    fix = lambda qb, kt: (0, 0)

    def bias_spec(a, b):
        def idx(qb, kt):
            d = (qb * tq + a * DSA_BQ - jnp.minimum(kt, last(qb)) * tk - b * DSA_BK) // LANES
            return (jnp.clip(d, 0, DSA_DCAP // LANES), 0, 0, 0)
        return pl.BlockSpec((1, D_HEADS, DSA_BQ, DSA_BK), idx)

    bias_specs = [bias_spec(a, b) for a in range(tq // DSA_BQ) for b in range(tk // DSA_BK)]
    return pl.pallas_call(
        functools.partial(_dsa_prompt_kernel, n_sel=n_sel), grid=(nb, nkt),
        in_specs=[pl.BlockSpec((D_HEADS, tq, HEAD_DIM), lambda qb, kt: (0, qb, 0)),
                  pl.BlockSpec((tq, IDX_HEADS * IDX_DIM), qrow),
                  pl.BlockSpec((tq, LANES), qrow), pl.BlockSpec(ikt.shape, fix),
                  pl.BlockSpec((D_WIDTH, tk), lambda qb, kt: (0, jnp.minimum(kt, last(qb)))),
                  pl.BlockSpec((D_WIDTH, tk), lambda qb, kt: (0, jnp.minimum(kt, last(qb))))] +
                 bias_specs + [pl.BlockSpec(tri.shape, fix)],
        out_specs=pl.BlockSpec((tq, D_WIDTH), qrow),
        out_shape=jax.ShapeDtypeStruct((seq, D_WIDTH), F32),
        scratch_shapes=[pltpu.VMEM((tq, seq), I32), pltpu.VMEM((tq, 1), I32), pltpu.VMEM((tq, 1), F32),
                        pltpu.VMEM((tq, 1), F32), pltpu.VMEM((tq, tk), F32), pltpu.VMEM((D_HEADS, tq, 1), F32),
                        pltpu.VMEM((D_HEADS, tq, 1), F32), pltpu.VMEM((D_HEADS, tq, HEAD_DIM), F32),
                        pltpu.VMEM((2, tq, tk), F32), pltpu.VMEM((2, tq, tk), BF16), pltpu.VMEM((2, tq, 1), F32)],
        compiler_params=_cparams("parallel", "arbitrary"), name="dsa_prompt")(
            q, iq, misc, ikt, kt_, v, *([bias_tiles] * len(bias_specs)), tri)


def _dsa_step_keys_kernel(pt_ref, iq_ref, iw_ref, *rest, n_new, pg):
    page_refs, new_ref, keys_ref = rest[:pg], rest[pg], rest[pg + 1]
    p = pl.program_id(1)
    npage = pl.num_programs(1) * pg
    psz = new_ref.shape[2]
    rows = keys_ref.shape[1]

    def score(kt):
        r = jnp.maximum(_dot(iq_ref[0], kt.astype(BF16)), 0.0) * iw_ref[0][:, 0:1]
        sc = r[0:rows]
        for h in range(1, IDX_HEADS):
            sc = sc + r[h * rows:(h + 1) * rows]
        return _sortable(sc * IDX_SCALE)

    kt = jnp.concatenate([page_ref[0, 0] for page_ref in page_refs], axis=1)
    keys_ref[0, :, pl.ds(pl.multiple_of(p * (pg * psz), psz), pg * psz)] = score(kt)

    @pl.when(p == 0)
    def _():
        s_i = lax.broadcasted_iota(I32, (rows, psz), 0)
        j_i = lax.broadcasted_iota(I32, (rows, psz), 1)
        ok = (j_i <= s_i) & (j_i < n_new)
        keys_ref[0, :, pl.ds(pl.multiple_of(npage * psz, psz), psz)] = jnp.where(ok, score(new_ref[0]), INT_MIN)


def dsa_step_keys(page_table, iqm, iwm, cache_kidx, ik_new, layer, n_new, pg):
    nseq, npage = page_table.shape
    psz = cache_kidx.shape[3]
    rows = iqm.shape[1] // IDX_HEADS
    ncol = (npage + 1) * psz
    seqb = lambda n, p, pt: (n, 0, 0)
    page_spec = lambda i: pl.BlockSpec((1, 1, IDX_DIM, psz),
                                       lambda n, p, pt: (layer, pt[n * npage + p * pg + i], 0, 0))
    grid_spec = pltpu.PrefetchScalarGridSpec(
        num_scalar_prefetch=1, grid=(nseq, npage // pg),
        in_specs=[pl.BlockSpec((1,) + iqm.shape[1:], seqb), pl.BlockSpec((1,) + iwm.shape[1:], seqb)] +
                 [page_spec(i) for i in range(pg)] + [pl.BlockSpec((1,) + ik_new.shape[1:], seqb)],
        out_specs=pl.BlockSpec((1, rows, ncol), seqb))
    return pl.pallas_call(
        functools.partial(_dsa_step_keys_kernel, n_new=n_new, pg=pg), grid_spec=grid_spec,
        out_shape=jax.ShapeDtypeStruct((nseq, rows, ncol), I32),
        compiler_params=_cparams("parallel", "arbitrary"), name="dsa_step_keys")(
            page_table.reshape(-1), iqm, iwm, *([cache_kidx] * pg), ik_new)


def _topk_mask_kernel(keys_ref, tri_ref, neg_ref, *, tw, n_sel):
    thr, rem = _kth_largest(keys_ref, keys_ref.shape[1] // tw, tw, n_sel)
    w = tri_ref.shape[0]

    def chunk(c, cnt):
        off = pl.multiple_of(c * w, w)
        sel, cnt = _select(keys_ref[:, pl.ds(off, w)], thr, rem, cnt, tri_ref[...])
        neg_ref[:, pl.ds(off, w)] = jnp.where(sel, 0.0, NEG_INF)
        return cnt

    lax.fori_loop(0, keys_ref.shape[1] // w, chunk, jnp.zeros(thr.shape, F32))


def topk_mask_rows(keys, tri, tw, n_sel, tr):
    rows, ncol = keys.shape
    return pl.pallas_call(
        functools.partial(_topk_mask_kernel, tw=tw, n_sel=n_sel), grid=(rows // tr,),
        in_specs=[pl.BlockSpec((tr, ncol), lambda i: (i, 0)), pl.BlockSpec(tri.shape, lambda i: (0, 0))],
        out_specs=pl.BlockSpec((tr, ncol), lambda i: (i, 0)),
        out_shape=jax.ShapeDtypeStruct((rows, ncol), F32),
        compiler_params=_cparams("parallel"), name="topk_mask_rows")(keys, tri)


def _dsa_step_attn_kernel(pt_ref, q_ref, neg_ref, neg_new_ref, bias_ref, bias_new_ref, *rest, pg):
    k_refs, v_refs = rest[:pg], rest[pg:2 * pg]
    kn_ref, vn_ref, o_ref, m_s, l_s, acc_s = rest[2 * pg:]
    p = pl.program_id(1)
    nstep = pl.num_programs(1) - 1
    rows = neg_ref.shape[1]

    @pl.when(p == 0)
    def _():
        m_s[...] = jnp.full_like(m_s, NEG_INF)
        l_s[...] = jnp.zeros_like(l_s)
        acc_s[...] = jnp.zeros_like(acc_s)

    def process(kt, vt, neg, bias):
        s = _dot(q_ref[0], kt.astype(BF16)) * (LOG2E * HEAD_DIM ** -0.5) + bias
        s = s + jnp.concatenate([neg] * D_HEADS, axis=0)
        m_old = m_s[...]
        m_new = jnp.maximum(m_old, jnp.max(s, axis=1, keepdims=True))
        m_safe = jnp.where(m_new == NEG_INF, 0.0, m_new)
        pr = jnp.exp2(s - m_safe)
        alpha = jnp.exp2(m_old - m_safe)
        l_s[...] = alpha * l_s[...] + jnp.sum(pr, axis=1, keepdims=True)
        acc_s[...] = alpha * acc_s[...] + _dot_nt(pr.astype(BF16), vt.astype(BF16))
        m_s[...] = m_new

    @pl.when(p < nstep)
    def _():
        process(jnp.concatenate([r[0, 0, 0] for r in k_refs], axis=1),
                jnp.concatenate([r[0, 0, 0] for r in v_refs], axis=1), neg_ref[0], bias_ref[...])

    @pl.when(p == nstep)
    def _():
        process(kn_ref[0, 0], vn_ref[0, 0], neg_new_ref[0], bias_new_ref[...])
        accn = acc_s[...] / l_s[...]
        head = lax.broadcasted_iota(I32, (rows, D_WIDTH), 1) // HEAD_DIM
        out = jnp.zeros((rows, D_WIDTH), F32)
        for h in range(D_HEADS):
            out = out + jnp.where(head == h, accn[h * rows:(h + 1) * rows], 0.0)
        o_ref[0] = out


def dsa_step_attn(page_table, qbd, neg, bias, cache_kvt, kvt_new, layer, pg):
    nseq, npage = page_table.shape
    psz = cache_kvt.shape[4]
    rows = neg.shape[1]
    qr = qbd.shape[1]
    nstep = npage // pg
    seqb = lambda n, p, pt: (n, 0, 0)
    stp = lambda p: jnp.minimum(p, nstep - 1)

    def page_spec(i, part):
        return pl.BlockSpec((1, 1, 1, D_WIDTH, psz),
                            lambda n, p, pt: (layer, pt[n * npage + stp(p) * pg + i], part, 0, 0))

    def new_spec(part):
        return pl.BlockSpec((1, 1, D_WIDTH, psz), lambda n, p, pt: (n, part, 0, 0))

    grid_spec = pltpu.PrefetchScalarGridSpec(
        num_scalar_prefetch=1, grid=(nseq, nstep + 1),
        in_specs=[pl.BlockSpec((1, qr, D_WIDTH), seqb),
                  pl.BlockSpec((1, rows, pg * psz), lambda n, p, pt: (n, 0, stp(p))),
                  pl.BlockSpec((1, rows, psz), lambda n, p, pt: (n, 0, npage)),
                  pl.BlockSpec((qr, pg * psz), lambda n, p, pt: (0, stp(p))),
                  pl.BlockSpec((qr, psz), lambda n, p, pt: (0, npage))] +
                 [page_spec(i, 0) for i in range(pg)] + [page_spec(i, 1) for i in range(pg)] +
                 [new_spec(0), new_spec(1)],
        out_specs=pl.BlockSpec((1, rows, D_WIDTH), seqb),
        scratch_shapes=[pltpu.VMEM((qr, 1), F32), pltpu.VMEM((qr, 1), F32), pltpu.VMEM((qr, D_WIDTH), F32)])
    return pl.pallas_call(
        functools.partial(_dsa_step_attn_kernel, pg=pg), grid_spec=grid_spec,
        out_shape=jax.ShapeDtypeStruct((nseq, rows, D_WIDTH), F32),
        compiler_params=_cparams("parallel", "arbitrary"), name="dsa_step_attn")(
            page_table.reshape(-1), qbd, neg, neg, bias, bias, *([cache_kvt] * (2 * pg)), kvt_new, kvt_new)


def _prep_ab_w(w):
    d = w.shape[0]
    aq, ak, av, bq, bk, bv, bg = _split_cols(w, [A_WIDTH] * 3 + [B_QK_WIDTH] * 2 + [B_V_WIDTH] * 2)
    half = B_KEY_DIM // 2

    def rot(m):
        return m.reshape(d, B_HEADS, 2, half)[:, :, ::-1, :].reshape(d, B_QK_WIDTH)

    pad = jnp.zeros((d, B_QK_PAD - B_QK_WIDTH), w.dtype)
    cols = [aq, ak, av, bq, pad, rot(bq), pad, bk, pad, rot(bk), pad, bv, bg]
    return jnp.concatenate(cols, axis=1).astype(BF16)


AB_SEGS = ((0, 3 * A_WIDTH, F32), (3 * A_WIDTH, 4 * B_QK_PAD, F32),
           (3 * A_WIDTH + 4 * B_QK_PAD, B_V_WIDTH, F32), (3 * A_WIDTH + 4 * B_QK_PAD + B_V_WIDTH, B_V_WIDTH, F32))


def _split_cols(x, widths):
    parts, start = [], 0
    for wd in widths:
        parts.append(x[..., start:start + wd])
        start += wd
    return parts


def _prep_cd_w(w):
    d = w.shape[0]
    cu, dq, dk, dv, iq, iw, ik = _split_cols(w, [C_WIDTH, D_WIDTH, D_WIDTH, D_WIDTH, IDX_HEADS * IDX_DIM, IDX_HEADS, IDX_DIM])
    pad = jnp.zeros((d, LANES - IDX_DIM - IDX_HEADS), w.dtype)
    return jnp.concatenate([cu, dq, dk, dv, iq, ik, iw, pad], axis=1).astype(BF16)


CD_SEGS = ((0, C_WIDTH, F32), (C_WIDTH, D_WIDTH, BF16), (C_WIDTH + D_WIDTH, 2 * D_WIDTH, F32),
           (C_WIDTH + 3 * D_WIDTH, IDX_HEADS * IDX_DIM, BF16), (C_WIDTH + 3 * D_WIDTH + IDX_HEADS * IDX_DIM, LANES, F32))


def _bias_by_distance(t5_table, dist):
    onehot = (_t5_bucket(dist)[..., None] == jnp.arange(T5_BUCKETS)).astype(F32)
    return jnp.einsum('...b,bh->...h', onehot, t5_table.astype(F32), precision=lax.Precision.HIGHEST)


def _dil_group_bias(t5_table, g, dil, nkey):
    return _bias_by_distance(t5_table, jnp.arange(nkey + 1) * dil)[:, g * A_GROUP_HEADS:(g + 1) * A_GROUP_HEADS]


def _dil_prompt_bias(t5_table, g, dil, nkey):
    bias = _dil_group_bias(t5_table, g, dil, nkey)
    ext = jnp.concatenate([jnp.broadcast_to(bias[:1], (nkey - 1, A_GROUP_HEADS)), bias,
                           jnp.broadcast_to(bias[-1:], (nkey - 1, A_GROUP_HEADS))], axis=0)[::-1]
    rows = [lax.slice_in_dim(ext, nkey - 1 - qi, nkey - 1 - qi + 2 * nkey, axis=0) for qi in range(nkey)]
    return jnp.moveaxis(jnp.stack(rows, axis=0), -1, 0)


def _dil_step_bias(t5_table, g, dil, nkey, n_buf, n_new, rows, pad_new):
    assert n_buf == nkey * dil
    rev = _dil_group_bias(t5_table, g, dil, nkey)[::-1]
    width = n_buf + pad_new
    out = []
    for s in range(rows):
        if s < n_new:
            blockc = jnp.full((nkey + 1, dil, A_GROUP_HEADS), NEG_INF, F32).at[:, s % dil, :].set(rev)
            flat = blockc.reshape((nkey + 1) * dil, A_GROUP_HEADS)
            front = s - s % dil
            row = jnp.pad(flat, ((front, width - front - flat.shape[0]), (0, 0)), constant_values=NEG_INF)
        else:
            row = jnp.full((width, A_GROUP_HEADS), NEG_INF, F32).at[0].set(0.0)
        out.append(row)
    return jnp.stack(out, axis=0).transpose(2, 0, 1).reshape(A_GROUP_HEADS * rows, width)


def _dsa_prompt_table(t5_table):
    d = (DSA_DCAP + LANES - 1) - jnp.arange(DSA_TAB)
    return _bias_by_distance(t5_table, d)[:, :D_HEADS].T * LOG2E


def _tri(n):
    r = np.arange(n)
    return jnp.asarray((r[:, None] <= r[None, :]).astype(np.float32)).astype(BF16)


def _ffn_and_ple(h, p, i, prm, tm, final):
    h = moe(h, prm['norm_ffn'][i], prm['moe_wr'][i], prm['moe_br'][i], prm['moe_wg'][i], prm['moe_wu'][i],
            prm['moe_wd'][i], min(4 * tm, h.shape[0]))
    return ple(h, p, prm['norm_ple'][i], prm['ple_wg'][i], prm['ple_wp'][i], prm['norm_final'] if final else None, tm)


def _layer_ab_prompt(h, prm, j, tm):
    seq = h.shape[0]
    aqkv, bqk, bv, bg = norm_matmul(h, prm['norm_mix'][2 * j], prm['ab_w'][j], AB_SEGS, tm)
    os_, ls, bufs = [], [], []
    for g, (win, dil) in enumerate(A_PATTERNS):
        nkey = win // dil
        assert seq % (dil * nkey) == 0
        o, lse = dil_prompt(aqkv, _dil_prompt_bias(prm['t5_table'], g, dil, nkey), g, dil)
        os_.append(o)
        ls.append(lse)
        kcols = aqkv[seq - min(win, seq):, A_WIDTH + g * LANES:A_WIDTH + (g + 1) * LANES]
        vcols = aqkv[seq - min(win, seq):, 2 * A_WIDTH + g * LANES:2 * A_WIDTH + (g + 1) * LANES]
        bufs.append(jnp.stack([kcols, vcols], axis=1).reshape(1, -1, 2, A_GROUP_HEADS, HEAD_DIM))
    chunk = B_CHUNK if seq % B_CHUNK == 0 else seq
    cosv, sinv = _rope_tables(jnp.arange(seq))
    s0 = jnp.zeros((1, B_HEADS, B_KEY_DIM, B_VAL_DIM), F32)
    b_out, s_new = retention(bqk, cosv, sinv, bv, bg, _retention_tables(chunk, chunk), prm['ab_gn_gain'][j], s0,
                             chunk, seq // chunk)
    h = ab_out(h, os_, ls, b_out, prm['ab_w_out'][j], tm)
    return h, bufs, s_new


def _layer_ab_sample(h, prm, j, past, nseq, n_new, pos0):
    rows = 8
    aqkv, bqk, bv, bg = norm_matmul(h, prm['norm_mix'][2 * j], prm['ab_w'][j], AB_SEGS, h.shape[0])
    a3 = aqkv.reshape(nseq, n_new, 3 * A_WIDTH)
    pad_rows = lambda x, r: jnp.pad(x, ((0, 0), (0, r - x.shape[1]), (0, 0)))
    os_, ls, bufs = [], [], []
    for g, (win, dil) in enumerate(A_PATTERNS):
        nkey = win // dil
        buf = past['a'][g]
        n_buf = buf.shape[2]
        cs = lambda part: slice(part * A_WIDTH + g * LANES, part * A_WIDTH + (g + 1) * LANES)
        q = pad_rows(a3[:, :, cs(0)], rows)
        kv_new = jnp.concatenate([a3[:, :, cs(1)], a3[:, :, cs(2)]], axis=2)
        bias = _dil_step_bias(prm['t5_table'], g, dil, nkey, n_buf, n_new, rows, LANES)
        buf_t = buf.transpose(0, 1, 3, 4, 5, 2).reshape(buf.shape[0], nseq, 2, LANES, n_buf)
        new_t = jnp.pad(kv_new.reshape(nseq, n_new, 2, LANES).transpose(0, 2, 3, 1),
                        ((0, 0), (0, 0), (0, 0), (0, LANES - n_new)))
        o, lse = dil_step(q, buf_t, new_t, bias, j)
        os_.append(o[:, :n_new].reshape(nseq * n_new, LANES))
        ls.append(lse[:, :n_new].reshape(nseq * n_new, LANES))
        kv_all = jnp.concatenate([buf[j].reshape(nseq, n_buf, 2 * LANES), kv_new], axis=1)
        bufs.append(kv_all[:, n_new:].reshape(nseq, n_buf, 2, A_GROUP_HEADS, HEAD_DIM))
    chunk = B_CHUNK if n_new % B_CHUNK == 0 else n_new
    assert chunk == n_new
    cpad = 16
    cosv, sinv = _rope_tables(pos0 + jnp.arange(cpad))
    padc = lambda x: jnp.pad(x.reshape(nseq, n_new, -1), ((0, 0), (0, cpad - n_new), (0, 0))).reshape(nseq * cpad, -1)
    b_out, s_new = retention(padc(bqk), cosv, sinv, padc(bv), padc(bg), _retention_tables(chunk, cpad),
                             prm['ab_gn_gain'][j], past['b'][j].astype(F32), cpad, 1)
    b_out = b_out.reshape(nseq, cpad, -1)[:, :n_new].reshape(nseq * n_new, -1)
    h = ab_out(h, os_, ls, b_out, prm['ab_w_out'][j], h.shape[0])
    return h, bufs, s_new


def _cd_project(h, prm, j, tm):
    return norm_matmul(h, prm['norm_mix'][2 * j + 1], prm['cd_w'][j], CD_SEGS, tm)


def _layer_cd_prompt(h, prm, j, tm):
    seq = h.shape[0]
    cu, dq, kv, iq, misc = _cd_project(h, prm, j, tm)
    x0 = jnp.zeros((1, C_FLAT), F32)
    c_out, xr, xi = s5(cu, prm['s5'][j], x0, x0, 1, min(256, seq))
    sc = jnp.stack([xr.reshape(C_GROUPS, C_STATE), xi.reshape(C_GROUPS, C_STATE)], axis=-1)[None]
    n_sel = min(IDX_TOPK, seq // 4)
    heads_first = lambda x: x.reshape(seq, D_HEADS, HEAD_DIM).transpose(1, 0, 2)
    kb = kv[:, :D_WIDTH].astype(BF16).T
    vb = kv[:, D_WIDTH:].astype(BF16).T
    ikt = misc[:, :IDX_DIM].astype(BF16).T
    bias_tiles = dsa_bias_tiles(_dsa_prompt_table(prm['t5_table']))
    d_out = dsa_prompt(heads_first(dq), iq, misc, ikt, kb, vb, bias_tiles, _tri(DSA_TRI), n_sel)
    h = cd_out(h, c_out, d_out, prm['cd_w_out'][j], tm)
    return h, sc, kv, misc[:, :IDX_DIM]


def _layer_cd_sample(h, prm, j, past, nseq, n_new):
    rows = 8
    cu, dq, kv, iq, misc = _cd_project(h, prm, j, h.shape[0])
    tmaj = lambda x: x.reshape(nseq, n_new, -1).transpose(1, 0, 2).reshape(nseq * n_new, -1)
    x0 = past['c'][j].astype(F32)
    c_out, xr, xi = s5(tmaj(cu), prm['s5'][j], x0[..., 0].reshape(nseq, C_FLAT), x0[..., 1].reshape(nseq, C_FLAT),
                       nseq, n_new)
    c_out = c_out.reshape(n_new, nseq, -1).transpose(1, 0, 2).reshape(nseq * n_new, -1)
    sc = jnp.stack([xr.reshape(nseq, C_GROUPS, C_STATE), xi.reshape(nseq, C_GROUPS, C_STATE)], axis=-1)

    page_table = past['page_table']
    cache_kv, cache_kidx = past['d_kv'], past['d_kidx']
    npage = page_table.shape[1]
    psz = cache_kv.shape[2]
    n_past = npage * psz
    n_sel = min(IDX_TOPK, (n_past + n_new) // 4)
    pad_rows = lambda x, r: jnp.pad(x, ((0, 0), (0, r - x.shape[1]), (0, 0)))
    ik_new = misc[:, :IDX_DIM]
    iqm = pad_rows(iq.reshape(nseq, n_new, IDX_HEADS * IDX_DIM), rows).reshape(nseq, rows, IDX_HEADS, IDX_DIM)
    iqm = iqm.transpose(0, 2, 1, 3).reshape(nseq, IDX_HEADS * rows, IDX_DIM)
    iwm = pad_rows(misc[:, IDX_DIM:IDX_DIM + IDX_HEADS].reshape(nseq, n_new, IDX_HEADS), rows)
    iwm = jnp.broadcast_to(iwm.transpose(0, 2, 1).reshape(nseq, IDX_HEADS * rows, 1), (nseq, IDX_HEADS * rows, LANES))
    kidx_t = cache_kidx.transpose(0, 1, 3, 2)
    kv_t = cache_kv.transpose(0, 1, 3, 4, 5, 2).reshape(cache_kv.shape[0], cache_kv.shape[1], 2, D_WIDTH, psz)
    lane_pad = lambda x: jnp.pad(x, ((0, 0),) * (x.ndim - 1) + ((0, psz - x.shape[-1]),))
    ik_new_t = lane_pad(ik_new.reshape(nseq, n_new, IDX_DIM).transpose(0, 2, 1))
    keys = dsa_step_keys(page_table, iqm, iwm, kidx_t, ik_new_t, j, n_new, math.gcd(npage, 16))
    ncol = keys.shape[2]
    neg = topk_mask_rows(keys.reshape(nseq * rows, ncol), _tri(LANES),
                         ncol // 3 if (ncol // LANES) % 3 == 0 else LANES, n_sel, min(LANES, nseq * rows))
    qpos = n_past + jnp.minimum(jnp.arange(rows), n_new - 1)
    dist = qpos[:, None] - jnp.arange(ncol)[None, :]
    bias = _bias_by_distance(prm['t5_table'], dist)[..., :D_HEADS] * LOG2E
    bias = bias.transpose(2, 0, 1).reshape(D_HEADS * rows, ncol)
    q = pad_rows(dq.reshape(nseq, n_new, D_WIDTH), rows)
    col_head = jnp.arange(D_WIDTH) // HEAD_DIM
    qbd = jnp.where(col_head[None, None, None, :] == jnp.arange(D_HEADS)[None, :, None, None], q[:, None], 0)
    qbd = qbd.reshape(nseq, D_HEADS * rows, D_WIDTH).astype(dq.dtype)
    kvt_new = lane_pad(kv.reshape(nseq, n_new, 2, D_WIDTH).transpose(0, 2, 3, 1))
    d_out = dsa_step_attn(page_table, qbd, neg.reshape(nseq, rows, ncol), bias, kv_t, kvt_new, j, math.gcd(npage, 16))
    d_out = d_out[:, :n_new].reshape(nseq * n_new, D_WIDTH)
    h = cd_out(h, c_out, d_out, prm['cd_w_out'][j], h.shape[0])
    return h, sc, kv, ik_new


def _trunk(x, p, prm, past, nseq, n_new, pos0, tm):
    depth = p.shape[0]
    h = x
    a_new = [[] for _ in A_PATTERNS]
    b_new, c_new, kv_new, kidx_new = [], [], [], []
    for i in range(depth):
        j = i // 2
        if i % 2 == 0:
            if past is None:
                h, bufs, sb = _layer_ab_prompt(h, prm, j, tm)
            else:
                h, bufs, sb = _layer_ab_sample(h, prm, j, past, nseq, n_new, pos0)
            for g in range(len(A_PATTERNS)):
                a_new[g].append(bufs[g])
            b_new.append(sb)
        else:
            if past is None:
                h, sc, kv, kidx = _layer_cd_prompt(h, prm, j, tm)
            else:
                h, sc, kv, kidx = _layer_cd_sample(h, prm, j, past, nseq, n_new)
            c_new.append(sc)
            kv_new.append(kv.reshape(nseq, n_new, 2, D_HEADS, HEAD_DIM))
            kidx_new.append(kidx.reshape(nseq, n_new, IDX_DIM))
        h = _ffn_and_ple(h, p[i], i, prm, tm, i == depth - 1)
    stack = lambda t: jnp.stack(t, axis=0)
    return h, tuple(stack(t) for t in a_new), stack(b_new), stack(c_new), stack(kv_new), stack(kidx_new)


def kernel(x_prompt, x_sample, p_prompt, p_sample, cache_a_kv0, cache_a_kv1, cache_a_kv2, state_b, state_c, cache_d_kv, cache_d_kidx, page_table, t5_table, norm_mix, norm_ffn, norm_ple, norm_final, ab_w_in, ab_w_out, ab_gn_gain, cd_w_in, cd_w_out, c_lambda_re, c_lambda_im, c_log_dt, c_b_re, c_b_im, c_c_re, c_c_im, c_d, c_w_glu, c_b_glu, moe_w_group, moe_b_group, moe_w_expert, moe_b_expert, moe_w_gate, moe_w_up, moe_w_down, ple_w_gate, ple_w_proj):
    depth, d_model = norm_mix.shape
    n_ab, n_cd = ab_w_in.shape[0], cd_w_in.shape[0]
    rpad = jnp.zeros((depth, d_model, LANES - MOE_GROUPS - MOE_EXPERTS), F32)
    prm = dict(
        t5_table=t5_table, norm_mix=norm_mix, norm_ffn=norm_ffn, norm_ple=norm_ple, norm_final=norm_final,
        ab_w=[_prep_ab_w(ab_w_in[j]) for j in range(n_ab)], ab_w_out=ab_w_out.astype(BF16), ab_gn_gain=ab_gn_gain,
        cd_w=[_prep_cd_w(cd_w_in[j]) for j in range(n_cd)], cd_w_out=cd_w_out.astype(BF16),
        s5=[_s5_prep(c_lambda_re[j], c_lambda_im[j], c_log_dt[j], c_b_re[j], c_b_im[j], c_c_re[j], c_c_im[j],
                     c_d[j], c_w_glu[j], c_b_glu[j]) for j in range(n_cd)],
        moe_wr=jnp.concatenate([moe_w_group, moe_w_expert.reshape(depth, d_model, MOE_EXPERTS), rpad],
                               axis=2).astype(BF16),
        moe_br=jnp.concatenate([moe_b_group, moe_b_expert.reshape(depth, MOE_EXPERTS),
                                jnp.zeros((depth, LANES - MOE_GROUPS - MOE_EXPERTS), F32)], axis=1)[:, None, :],
        moe_wg=moe_w_gate.astype(BF16), moe_wu=moe_w_up.astype(BF16), moe_wd=moe_w_down.astype(BF16),
        ple_wg=ple_w_gate.astype(BF16), ple_wp=ple_w_proj.astype(BF16))

    bsz, seq, _ = x_prompt.shape
    assert bsz == 1
    y_p, a_p, sb_p, sc_p, dkv_p, dki_p = _trunk(x_prompt[0], p_prompt[:, 0], prm, None, 1, seq, 0, min(256, seq))
    y_p = y_p[None]

    nseq, n_new, _ = x_sample.shape
    past = dict(a=(cache_a_kv0, cache_a_kv1, cache_a_kv2), b=state_b, c=state_c, d_kv=cache_d_kv, d_kidx=cache_d_kidx,
                page_table=page_table)
    n_past = page_table.shape[1] * cache_d_kv.shape[2]
    rows_s = nseq * n_new
    y_s, a_s, sb_s, sc_s, dkv_s, dki_s = _trunk(x_sample.reshape(rows_s, d_model),
                                                p_sample.reshape(depth, rows_s, -1), prm, past, nseq, n_new, n_past,
                                                rows_s)
    y_s = y_s.reshape(nseq, n_new, d_model)
    return (y_p, y_s, a_p[0], a_s[0], a_p[1], a_s[1], a_p[2], a_s[2], sb_p, sb_s, sc_p, sc_s, dkv_p, dkv_s, dki_p, dki_s)
```

```python
import functools
import math

import numpy as np
import jax
import jax.numpy as jnp
from jax import lax
from jax.experimental import pallas as pl
from jax.experimental.pallas import tpu as pltpu

F32 = jnp.float32
BF16 = jnp.bfloat16
I32 = jnp.int32

EPS = 1e-6
HEAD_DIM = 64
A_PATTERNS = ((128, 1), (512, 4), (2048, 16))
A_GROUP_HEADS = 2
A_WIDTH = 384
B_HEADS = 5
B_KEY_DIM = 64
B_VAL_DIM = 128
B_QK_WIDTH = 320
B_QK_PAD = 384
B_V_WIDTH = 640
B_CHUNK = 128
ROPE_BASE = 10000.0
C_GROUPS = 32
C_GROUP_SIZE = 16
C_WIDTH = 512
C_STATE = 64
C_FLAT = C_GROUPS * C_STATE
C_SUPER = 8
D_HEADS = 8
D_WIDTH = 512
IDX_HEADS = 4
IDX_DIM = 64
IDX_TOPK = 256
IDX_SCALE = (IDX_HEADS * IDX_DIM) ** -0.5
T5_BUCKETS = 32
T5_MAX_DIST = 2048
MOE_GROUPS = 4
MOE_GROUP_EXPERTS = 4
MOE_EXPERTS = 16
D_EXPERT = 512
LANES = 128
INT_MIN = -(2 ** 31)
NEG_INF = float("-inf")
LOG2E = math.log2(math.e)

VMEM_LIMIT_BYTES = 52 * 1024 * 1024

DSA_TQ = 256
DSA_TK = 1024
DSA_BQ = 128
DSA_BK = 512
DSA_DCAP = T5_MAX_DIST + DSA_BK
DSA_TAB = DSA_DCAP + DSA_BK + 2 * LANES
DSA_TRI = 256
DSA_RC = 32
KTH_ROWS = 128
DIL_BLOCKS_PER_STEP = 4


def _cparams(*sem):
    return pltpu.CompilerParams(dimension_semantics=sem, vmem_limit_bytes=VMEM_LIMIT_BYTES)


def _rms(x, g):
    return x * lax.rsqrt(jnp.mean(x * x, axis=-1, keepdims=True) + EPS) * g


def _dot(a, b):
    return jnp.dot(a, b, preferred_element_type=F32)


def _dot_nt(a, b):
    return lax.dot_general(a, b, (((1,), (1,)), ((), ())), preferred_element_type=F32)


def _dot_tn(a, b):
    return lax.dot_general(a, b, (((0,), (0,)), ((), ())), preferred_element_type=F32)


def _sigmoid(x):
    return 1.0 / (1.0 + jnp.exp(-x))


def _t5_bucket(dist):
    max_exact = T5_BUCKETS // 2
    d = jnp.maximum(dist, 0)
    scaled = jnp.log(jnp.maximum(d, 1).astype(F32) / max_exact) / math.log(T5_MAX_DIST / max_exact)
    large = jnp.minimum(max_exact + (scaled * (T5_BUCKETS - max_exact)).astype(I32), T5_BUCKETS - 1)
    return jnp.where(d < max_exact, d, large)


def _norm_matmul_kernel(x_ref, g_ref, w_ref, *out_refs, segs):
    xn = _rms(x_ref[...], g_ref[...]).astype(BF16)
    for o_ref, (start, width) in zip(out_refs, segs):
        o_ref[...] = _dot(xn, w_ref[:, start:start + width]).astype(o_ref.dtype)


def norm_matmul(x, gain, w, segs, tm):
    t, d = x.shape
    n = w.shape[1]
    kern = functools.partial(_norm_matmul_kernel, segs=tuple((s, wd) for s, wd, _ in segs))
    return pl.pallas_call(
        kern, grid=(t // tm,),
        in_specs=[pl.BlockSpec((tm, d), lambda i: (i, 0)),
                  pl.BlockSpec((1, d), lambda i: (0, 0)),
                  pl.BlockSpec((d, n), lambda i: (0, 0))],
        out_specs=[pl.BlockSpec((tm, wd), lambda i: (i, 0)) for _, wd, _ in segs],
        out_shape=[jax.ShapeDtypeStruct((t, wd), dt) for _, wd, dt in segs],
        compiler_params=_cparams("parallel"), name="norm_matmul")(x, gain.reshape(1, d), w)


def _dil_prompt_kernel(q_ref, kc_ref, kp_ref, vc_ref, vp_ref, bias_ref, o_ref, lse_ref):
    n = pl.program_id(1)
    blk = kp_ref.shape[0]
    nblk = q_ref.shape[0] // blk
    qi = lax.broadcasted_iota(I32, (blk, 2 * blk), 0)
    kj = lax.broadcasted_iota(I32, (blk, 2 * blk), 1)
    m = blk + qi - kj
    band = (m >= 0) & (m <= blk)
    for i in range(nblk):
        rows = slice(i * blk, (i + 1) * blk)
        q = q_ref[rows, :]
        if i == 0:
            k_prev, v_prev = kp_ref[...], vp_ref[...]
            mask = band & (kj >= jnp.where(n > 0, 0, blk))
        else:
            prev_rows = slice((i - 1) * blk, i * blk)
            k_prev, v_prev = kc_ref[prev_rows, :], vc_ref[prev_rows, :]
            mask = band
        kw = jnp.concatenate([k_prev, kc_ref[rows, :]], axis=0)
        vw = jnp.concatenate([v_prev, vc_ref[rows, :]], axis=0)
        outs, lses = [], []
        for h in range(A_GROUP_HEADS):
            cs = slice(h * HEAD_DIM, (h + 1) * HEAD_DIM)
            s = _dot_nt(q[:, cs].astype(BF16), kw[:, cs].astype(BF16)) * (HEAD_DIM ** -0.5) + bias_ref[h]
            s = jnp.where(mask, s, NEG_INF)
            mx = jnp.max(s, axis=1, keepdims=True)
            p = jnp.exp(s - mx)
            l = jnp.sum(p, axis=1, keepdims=True)
            outs.append(_dot((p / l).astype(BF16), vw[:, cs].astype(BF16)))
            lses.append(jnp.broadcast_to(mx + jnp.log(l), (blk, HEAD_DIM)))
        o_ref[rows, :] = jnp.concatenate(outs, axis=1)
        lse_ref[rows, :] = jnp.concatenate(lses, axis=1)


def dil_prompt(aqkv, bias_qk, g, dil):
    seq = aqkv.shape[0]
    blk = bias_qk.shape[1]
    nb = seq // (blk * dil)
    per = math.gcd(nb, DIL_BLOCKS_PER_STEP)
    ncol = aqkv.shape[1] // LANES
    a = aqkv.reshape(seq // dil, dil * aqkv.shape[1])
    ng = len(A_PATTERNS)

    def cur(off):
        return pl.BlockSpec((per * blk, LANES), lambda r, n: (n, r * ncol + off + g))

    def prev(off):
        return pl.BlockSpec((blk, LANES), lambda r, n: (jnp.maximum(n * per - 1, 0), r * ncol + off + g))

    o, lse = pl.pallas_call(
        _dil_prompt_kernel, grid=(dil, nb // per),
        in_specs=[cur(0), cur(ng), prev(ng), cur(2 * ng), prev(2 * ng),
                  pl.BlockSpec(bias_qk.shape, lambda r, n: (0, 0, 0))],
        out_specs=[pl.BlockSpec((per * blk, LANES), lambda r, n: (n, r))] * 2,
        out_shape=[jax.ShapeDtypeStruct((seq // dil, dil * LANES), F32)] * 2,
        compiler_params=_cparams("parallel", "parallel"), name="dil_prompt")(a, a, a, a, a, bias_qk)
    return o.reshape(seq, LANES), lse.reshape(seq, LANES)


def _dil_step_kernel(q_ref, buf_ref, new_ref, bias_ref, o_ref, lse_ref):
    q = q_ref[0]
    rows = q.shape[0]
    head = lax.broadcasted_iota(I32, q.shape, 1) // HEAD_DIM
    qbd = jnp.concatenate([jnp.where(head == h, q, 0.0) for h in range(A_GROUP_HEADS)], axis=0).astype(BF16)
    kt = jnp.concatenate([buf_ref[0, 0, 0], new_ref[0, 0]], axis=1).astype(BF16)
    vt = jnp.concatenate([buf_ref[0, 0, 1], new_ref[0, 1]], axis=1).astype(BF16)
    s = _dot(qbd, kt) * (HEAD_DIM ** -0.5) + bias_ref[...]
    mx = jnp.max(s, axis=1, keepdims=True)
    p = jnp.exp(s - mx)
    l = jnp.sum(p, axis=1, keepdims=True)
    o = _dot_nt((p / l).astype(BF16), vt)
    lse = jnp.broadcast_to(mx + jnp.log(l), o.shape)
    o_ref[0] = jnp.where(head == 0, o[:rows], o[rows:])
    lse_ref[0] = jnp.where(head == 0, lse[:rows], lse[rows:])


def dil_step(q, buf, new, bias, layer):
    nseq, rows, _ = q.shape
    win = buf.shape[4]
    return pl.pallas_call(
        _dil_step_kernel, grid=(nseq,),
        in_specs=[pl.BlockSpec((1, rows, LANES), lambda n: (n, 0, 0)),
                  pl.BlockSpec((1, 1, 2, LANES, win), lambda n: (layer, n, 0, 0, 0)),
                  pl.BlockSpec((1,) + new.shape[1:], lambda n: (n, 0, 0, 0)),
                  pl.BlockSpec(bias.shape, lambda n: (0, 0))],
        out_specs=[pl.BlockSpec((1, rows, LANES), lambda n: (n, 0, 0))] * 2,
        out_shape=[jax.ShapeDtypeStruct((nseq, rows, LANES), F32)] * 2,
        compiler_params=_cparams("parallel"), name="dil_step")(q, buf, new, bias)


def _retention_kernel(bqk_ref, cos_ref, sin_ref, bv_ref, bg_ref, decay_ref, win_ref, wend_ref, gch_ref, gain_ref,
                      s0_ref, out_ref, s_ref):
    c = pl.program_id(1)

    @pl.when(c == 0)
    def _():
        s_ref[...] = s0_ref[...]

    bqk = bqk_ref[...]
    cosv, sinv = cos_ref[...], sin_ref[...]
    w = B_QK_PAD
    q = (bqk[:, 0:w] * cosv + bqk[:, w:2 * w] * sinv) * (B_KEY_DIM ** -0.5)
    k = bqk[:, 2 * w:3 * w] * cosv + bqk[:, 3 * w:4 * w] * sinv
    q_in = q * win_ref[...]
    k_end = k * wend_ref[...]
    bv, bg, gain = bv_ref[...], bg_ref[...], gain_ref[...]
    outs = []
    for h in range(B_HEADS):
        ks = slice(h * B_KEY_DIM, (h + 1) * B_KEY_DIM)
        vs = slice(h * B_VAL_DIM, (h + 1) * B_VAL_DIM)
        vh = bv[:, vs].astype(BF16)
        state = s_ref[0, h]
        intra = _dot_nt(q[:, ks].astype(BF16), k[:, ks].astype(BF16)) * decay_ref[h]
        o = _dot(intra.astype(BF16), vh) + _dot(q_in[:, ks].astype(BF16), state.astype(BF16))
        s_ref[0, h] = gch_ref[h] * state + _dot_tn(k_end[:, ks].astype(BF16), vh)
        mu = jnp.mean(o, axis=1, keepdims=True)
        var = jnp.mean((o - mu) ** 2, axis=1, keepdims=True)
        on = (o - mu) * lax.rsqrt(var + EPS) * gain[:, vs]
        g = bg[:, vs]
        outs.append(g * _sigmoid(g) * on)
    out_ref[...] = jnp.concatenate(outs, axis=1)


def retention(bqk, cosv, sinv, bv, bg, tabs, gain, s0, chunk, nchunk):
    nseq = s0.shape[0]
    rows = bqk.shape[0]
    decay, w_in, w_end, g_chunk = tabs
    row = lambda b, c: (b * nchunk + c, 0)
    pos = lambda b, c: (c, 0)
    fix2 = lambda b, c: (0, 0)
    fix3 = lambda b, c: (0, 0, 0)
    return pl.pallas_call(
        _retention_kernel, grid=(nseq, nchunk),
        in_specs=[pl.BlockSpec((chunk, 4 * B_QK_PAD), row),
                  pl.BlockSpec((chunk, B_QK_PAD), pos), pl.BlockSpec((chunk, B_QK_PAD), pos),
                  pl.BlockSpec((chunk, B_V_WIDTH), row), pl.BlockSpec((chunk, B_V_WIDTH), row),
                  pl.BlockSpec(decay.shape, fix3), pl.BlockSpec(w_in.shape, fix2), pl.BlockSpec(w_end.shape, fix2),
                  pl.BlockSpec(g_chunk.shape, fix3), pl.BlockSpec((1, B_V_WIDTH), fix2),
                  pl.BlockSpec((1,) + s0.shape[1:], lambda b, c: (b, 0, 0, 0))],
        out_specs=[pl.BlockSpec((chunk, B_V_WIDTH), row),
                   pl.BlockSpec((1,) + s0.shape[1:], lambda b, c: (b, 0, 0, 0))],
        out_shape=[jax.ShapeDtypeStruct((rows, B_V_WIDTH), F32), jax.ShapeDtypeStruct(s0.shape, F32)],
        compiler_params=_cparams("parallel", "arbitrary"), name="retention")(
            bqk, cosv, sinv, bv, bg, decay, w_in, w_end, g_chunk, gain.reshape(1, B_V_WIDTH), s0)


def _retention_tables(chunk, pad):
    lg = np.log(1.0 - 2.0 ** (-5.0 - np.arange(B_HEADS, dtype=np.float32))).astype(np.float32)
    lg = jnp.asarray(lg)
    i = jnp.arange(pad, dtype=F32)
    diff = i[:, None] - i[None, :]
    decay = jnp.where(diff >= 0, jnp.exp(jnp.maximum(diff, 0.0)[None] * lg[:, None, None]), 0.0)
    w_end = jnp.exp((chunk - 1.0 - i)[None, :] * lg[:, None])
    w_in = jnp.exp((i + 1.0)[None, :] * lg[:, None])
    g_chunk = jnp.exp(chunk * lg)

    def widen(t):
        t = jnp.repeat(t.T, B_KEY_DIM, axis=1)
        return jnp.pad(t, ((0, 0), (0, B_QK_PAD - B_QK_WIDTH)))

    return decay, widen(w_in), widen(w_end), jnp.broadcast_to(g_chunk[:, None, None], (B_HEADS, 1, B_VAL_DIM))


def _rope_tables(pos):
    half = B_KEY_DIM // 2
    freq = ROPE_BASE ** (-jnp.arange(half, dtype=F32) / half)
    ang = pos.astype(F32)[:, None] * freq[None, :]
    cos, sin = jnp.cos(ang), jnp.sin(ang)
    cos_h = jnp.concatenate([cos, cos], axis=1)
    sin_h = jnp.concatenate([-sin, sin], axis=1)
    padw = ((0, 0), (0, B_QK_PAD - B_QK_WIDTH))
    return jnp.pad(jnp.tile(cos_h, (1, B_HEADS)), padw), jnp.pad(jnp.tile(sin_h, (1, B_HEADS)), padw)


def _ab_out_kernel(h_ref, o0_ref, o1_ref, o2_ref, l0_ref, l1_ref, l2_ref, b_ref, w_ref, out_ref):
    ls = [l0_ref[...], l1_ref[...], l2_ref[...]]
    os_ = [o0_ref[...], o1_ref[...], o2_ref[...]]
    mx = jnp.maximum(jnp.maximum(ls[0], ls[1]), ls[2])
    es = [jnp.exp(l - mx) for l in ls]
    den = es[0] + es[1] + es[2]
    acc = h_ref[...]
    for g in range(3):
        acc = acc + _dot((os_[g] * (es[g] / den)).astype(BF16), w_ref[g * LANES:(g + 1) * LANES, :])
    acc = acc + _dot(b_ref[...].astype(BF16), w_ref[A_WIDTH:, :])
    out_ref[...] = acc


def ab_out(h, os_, ls, b_out, w, tm):
    t, d = h.shape
    row = lambda i: (i, 0)
    small = pl.BlockSpec((tm, LANES), row)
    return pl.pallas_call(
        _ab_out_kernel, grid=(t // tm,),
        in_specs=[pl.BlockSpec((tm, d), row)] + [small] * 6 +
                 [pl.BlockSpec((tm, B_V_WIDTH), row), pl.BlockSpec(w.shape, lambda i: (0, 0))],
        out_specs=pl.BlockSpec((tm, d), row),
        out_shape=jax.ShapeDtypeStruct((t, d), F32),
        compiler_params=_cparams("parallel"), name="ab_out")(h, *os_, *ls, b_out, w)


def _cd_out_kernel(h_ref, c_ref, d_ref, w_ref, out_ref):
    acc = h_ref[...] + _dot(c_ref[...].astype(BF16), w_ref[:C_WIDTH, :])
    out_ref[...] = acc + _dot(d_ref[...].astype(BF16), w_ref[C_WIDTH:, :])


def cd_out(h, c_out, d_out, w, tm):
    t, d = h.shape
    row = lambda i: (i, 0)
    return pl.pallas_call(
        _cd_out_kernel, grid=(t // tm,),
        in_specs=[pl.BlockSpec((tm, d), row), pl.BlockSpec((tm, C_WIDTH), row), pl.BlockSpec((tm, D_WIDTH), row),
                  pl.BlockSpec(w.shape, lambda i: (0, 0))],
        out_specs=pl.BlockSpec((tm, d), row),
        out_shape=jax.ShapeDtypeStruct((t, d), F32),
        compiler_params=_cparams("parallel"), name="cd_out")(h, c_out, d_out, w)


def _route(logits):
    lane = lax.broadcasted_iota(I32, logits.shape, 1).astype(F32)
    gmask = lane < MOE_GROUPS
    gl = jnp.where(gmask, logits, NEG_INF)
    gmax = jnp.max(gl, axis=1, keepdims=True)
    gsum = jnp.sum(jnp.where(gmask, jnp.exp(gl - gmax), 0.0), axis=1, keepdims=True)
    g_w = 1.0 / gsum
    g_idx = jnp.min(jnp.where(gl == gmax, lane, LANES), axis=1, keepdims=True)
    lo = MOE_GROUPS + MOE_GROUP_EXPERTS * g_idx
    emask = (lane >= lo) & (lane < lo + MOE_GROUP_EXPERTS)
    el = jnp.where(emask, logits, NEG_INF)
    emax = jnp.max(el, axis=1, keepdims=True)
    eexp = jnp.where(emask, jnp.exp(el - emax), 0.0)
    p = eexp / jnp.sum(eexp, axis=1, keepdims=True)
    pm = jnp.where(emask, p, -1.0)
    p1 = jnp.max(pm, axis=1, keepdims=True)
    i1 = jnp.min(jnp.where(pm == p1, lane, LANES), axis=1, keepdims=True)
    pm2 = jnp.where(lane == i1, -1.0, pm)
    p2 = jnp.max(pm2, axis=1, keepdims=True)
    i2 = jnp.min(jnp.where(pm2 == p2, lane, LANES), axis=1, keepdims=True)
    den = p1 + p2
    return jnp.where(lane == i1, g_w * (p1 / den), jnp.where(lane == i2, g_w * (p2 / den), 0.0))


def _moe_kernel(h_ref, g_ref, wr_ref, br_ref, wg_ref, wu_ref, wd_ref, out_ref, xn_s, comb_s, acc_s):
    e = pl.program_id(1)

    @pl.when(e == 0)
    def _():
        xn = _rms(h_ref[...], g_ref[...]).astype(BF16)
        xn_s[...] = xn
        comb_s[...] = _route(_dot(xn, wr_ref[...]) + br_ref[...])
        acc_s[...] = jnp.zeros_like(acc_s)

    xb = xn_s[...]
    hg = _dot(xb, wg_ref[0])
    hu = _dot(xb, wu_ref[0])
    hdn = (hg * _sigmoid(hg)) * hu
    y = _dot(hdn.astype(BF16), wd_ref[0])
    comb = comb_s[...]
    lane = lax.broadcasted_iota(I32, comb.shape, 1)
    cw = jnp.sum(jnp.where(lane == MOE_GROUPS + e, comb, 0.0), axis=1, keepdims=True)
    acc_s[...] += cw * y

    @pl.when(e == MOE_EXPERTS - 1)
    def _():
        out_ref[...] = h_ref[...] + acc_s[...]


def moe(h, gain, wr, br, wg, wu, wd, tm):
    t, d = h.shape
    row = lambda i, e: (i, 0)
    fix = lambda i, e: (0, 0)
    ex = lambda i, e: (e, 0, 0)
    return pl.pallas_call(
        _moe_kernel, grid=(t // tm, MOE_EXPERTS),
        in_specs=[pl.BlockSpec((tm, d), row), pl.BlockSpec((1, d), fix),
                  pl.BlockSpec((d, LANES), fix), pl.BlockSpec((1, LANES), fix),
                  pl.BlockSpec((1, d, D_EXPERT), ex), pl.BlockSpec((1, d, D_EXPERT), ex),
                  pl.BlockSpec((1, D_EXPERT, d), ex)],
        out_specs=pl.BlockSpec((tm, d), row),
        out_shape=jax.ShapeDtypeStruct((t, d), F32),
        scratch_shapes=[pltpu.VMEM((tm, d), BF16), pltpu.VMEM((tm, LANES), F32), pltpu.VMEM((tm, d), F32)],
        compiler_params=_cparams("parallel", "arbitrary"), name="moe")(
            h, gain.reshape(1, d), wr, br, wg, wu, wd)


def _ple_kernel(h_ref, p_ref, g_ref, wg_ref, wp_ref, *rest, final):
    h = h_ref[...]
    gate = _sigmoid(_dot(_rms(h, g_ref[...]).astype(BF16), wg_ref[...]))
    hn = h + gate * _dot(p_ref[...].astype(BF16), wp_ref[...])
    if final:
        gf_ref, out_ref = rest
        out_ref[...] = _rms(hn, gf_ref[...])
    else:
        rest[0][...] = hn


def ple(h, p, gain, wg, wp, gain_final, tm):
    t, d = h.shape
    row = lambda i: (i, 0)
    fix = lambda i: (0, 0)
    final = gain_final is not None
    in_specs = [pl.BlockSpec((tm, d), row), pl.BlockSpec((tm, p.shape[1]), row), pl.BlockSpec((1, d), fix),
                pl.BlockSpec(wg.shape, fix), pl.BlockSpec(wp.shape, fix)]
    args = [h, p, gain.reshape(1, d), wg, wp]
    if final:
        in_specs.append(pl.BlockSpec((1, d), fix))
        args.append(gain_final.reshape(1, d))
    return pl.pallas_call(
        functools.partial(_ple_kernel, final=final), grid=(t // tm,),
        in_specs=in_specs, out_specs=pl.BlockSpec((tm, d), row),
        out_shape=jax.ShapeDtypeStruct((t, d), F32),
        compiler_params=_cparams("parallel"), name="ple")(*args)


def _s5_kernel(u_ref, a_ref, bre_ref, bim_ref, cre_ref, cim_ref, d_ref, wglu_ref, bglu_ref, x0r_ref, x0i_ref,
               out_ref, xr_ref, xi_ref, bur_s, bui_s, *, nbatch, steps):
    c = pl.program_id(0)

    @pl.when(c == 0)
    def _():
        xr_ref[...] = x0r_ref[...]
        xi_ref[...] = x0i_ref[...]

    u = u_ref[...]
    ub = u.astype(BF16)
    nsb = bre_ref.shape[0]
    wi, ws = C_WIDTH // nsb, C_FLAT // nsb
    for j in range(nsb):
        ubj = ub[:, j * wi:(j + 1) * wi]
        bur_s[:, j * ws:(j + 1) * ws] = _dot(ubj, bre_ref[j])
        bui_s[:, j * ws:(j + 1) * ws] = _dot(ubj, bim_ref[j])
    ar = a_ref[0:1, :]
    ai = a_ref[1:2, :]

    def step(t, carry):
        xr, xi = carry
        r0 = pl.multiple_of(t * nbatch, nbatch)
        nr = ar * xr - ai * xi + bur_s[pl.ds(r0, nbatch), :]
        ni = ar * xi + ai * xr + bui_s[pl.ds(r0, nbatch), :]
        bur_s[pl.ds(r0, nbatch), :] = nr
        bui_s[pl.ds(r0, nbatch), :] = ni
        return nr, ni

    xr, xi = lax.fori_loop(0, steps, step, (xr_ref[...], xi_ref[...]), unroll=min(steps, 8))
    xr_ref[...] = xr
    xi_ref[...] = xi
    y = jnp.concatenate(
        [_dot(bur_s[:, j * ws:(j + 1) * ws].astype(BF16), cre_ref[j]) -
         _dot(bui_s[:, j * ws:(j + 1) * ws].astype(BF16), cim_ref[j]) for j in range(nsb)], axis=1) + d_ref[...] * u
    z = 0.5 * y * (1.0 + jnp.tanh(math.sqrt(2.0 / math.pi) * (y + 0.044715 * (y * y * y))))
    out_ref[...] = z * _sigmoid(_dot(z.astype(BF16), wglu_ref[...]) + bglu_ref[...])


def s5(u, prm, x0r, x0i, nbatch, steps):
    rows = u.shape[0]
    a, bre, bim, cre, cim, dskip, wglu, bglu = prm
    nchunk = rows // (nbatch * steps)
    tr = nbatch * steps
    full = lambda arr: pl.BlockSpec(arr.shape, lambda c: (0,) * arr.ndim)
    return pl.pallas_call(
        functools.partial(_s5_kernel, nbatch=nbatch, steps=steps), grid=(nchunk,),
        in_specs=[pl.BlockSpec((tr, C_WIDTH), lambda c: (c, 0))] +
                 [full(t) for t in (a, bre, bim, cre, cim, dskip, wglu, bglu, x0r, x0i)],
        out_specs=[pl.BlockSpec((tr, C_WIDTH), lambda c: (c, 0)), full(x0r), full(x0i)],
        out_shape=[jax.ShapeDtypeStruct((rows, C_WIDTH), F32), jax.ShapeDtypeStruct(x0r.shape, F32),
                   jax.ShapeDtypeStruct(x0i.shape, F32)],
        scratch_shapes=[pltpu.VMEM((tr, C_FLAT), F32), pltpu.VMEM((tr, C_FLAT), F32)],
        compiler_params=_cparams("arbitrary"), name="s5")(u, a, bre, bim, cre, cim, dskip, wglu, bglu, x0r, x0i)


def _s5_prep(lam_re, lam_im, log_dt, b_re, b_im, c_re, c_im, d_skip, w_glu, b_glu):
    dt = jnp.exp(log_dt)[:, None]
    mag = jnp.exp(lam_re * dt)
    ab_re, ab_im = mag * jnp.cos(lam_im * dt), mag * jnp.sin(lam_im * dt)
    den = lam_re * lam_re + lam_im * lam_im
    zr = ((ab_re - 1.0) * lam_re + ab_im * lam_im) / den
    zi = (ab_im * lam_re - (ab_re - 1.0) * lam_im) / den
    bb_re = zr[..., None] * b_re - zi[..., None] * b_im
    bb_im = zr[..., None] * b_im + zi[..., None] * b_re
    eye = jnp.eye(C_GROUPS, dtype=F32)
    nsb = C_GROUPS // C_SUPER
    wi, ws = C_WIDTH // nsb, C_FLAT // nsb

    def diag_blocks(m, rows, cols):
        return jnp.stack([m[j * rows:(j + 1) * rows, j * cols:(j + 1) * cols] for j in range(nsb)], axis=0)

    b_dense = lambda bb: diag_blocks(jnp.einsum('gnc,gh->gchn', bb, eye).reshape(C_WIDTH, C_FLAT), wi, ws).astype(BF16)
    c_dense = lambda cc: diag_blocks(jnp.einsum('gcn,gh->gnhc', cc, eye).reshape(C_FLAT, C_WIDTH), ws, wi).astype(BF16)
    a = jnp.stack([ab_re.reshape(C_FLAT), ab_im.reshape(C_FLAT)], axis=0)
    return (a, b_dense(bb_re), b_dense(bb_im), c_dense(c_re), c_dense(c_im), d_skip.reshape(1, C_WIDTH),
            w_glu.astype(BF16), b_glu.reshape(1, C_WIDTH))


def _sortable(x):
    bits = lax.bitcast_convert_type(x, I32)
    return bits ^ ((bits >> 31) & jnp.int32(0x7FFFFFFF))


def _kth_largest(keys_ref, ntiles, tw, n_sel):
    total_rows = keys_ref.shape[0]
    rows = min(total_rows, KTH_ROWS)
    thrs, rems = [], []
    for r0 in range(0, total_rows, rows):
        def count_ge(cand, r0=r0):
            def body(t, cv):
                off = pl.multiple_of(t * tw, LANES)
                kk = keys_ref[r0:r0 + rows, pl.ds(off, tw)]
                for j in range(tw // LANES):
                    cv = cv + jnp.where(kk[:, j * LANES:(j + 1) * LANES] >= cand, 1.0, 0.0)
                return cv
            cv = lax.fori_loop(0, ntiles, body, jnp.zeros((rows, LANES), F32))
            return jnp.sum(cv, axis=1, keepdims=True)

        def bit(i, thr, count_ge=count_ge):
            cand = thr + lax.shift_left(jnp.int32(1), 31 - i)
            return jnp.where(count_ge(cand) >= n_sel, cand, thr)

        thr = lax.fori_loop(0, 32, bit, jnp.full((rows, 1), INT_MIN, I32))
        thrs.append(thr)
        rems.append(jnp.where(thr == INT_MIN, 0.0, n_sel - count_ge(thr + 1)))
    if len(thrs) == 1:
        return thrs[0], rems[0]
    return jnp.concatenate(thrs, axis=0), jnp.concatenate(rems, axis=0)


def _select(key, thr, rem, cnt, tri):
    eq = key == thr
    eqb = jnp.where(eq, 1.0, 0.0).astype(BF16)
    w = tri.shape[0]
    ranks = []
    for c in range(key.shape[1] // w):
        pc = _dot(eqb[:, c * w:(c + 1) * w], tri)
        ranks.append(cnt + pc)
        cnt = cnt + pc[:, -1:]
    rank = ranks[0] if len(ranks) == 1 else jnp.concatenate(ranks, axis=1)
    sel = (key > thr) | (eq & (rank <= rem))
    return sel, cnt


def _dsa_bias_kernel(tbr_ref, out_ref):
    tq, tk = DSA_BQ, DSA_BK
    start = pl.multiple_of(DSA_DCAP - pl.program_id(0) * LANES, LANES)
    wtab = tbr_ref[:, pl.ds(start, tk + 2 * LANES)]
    for h in range(D_HEADS):
        wb = jnp.broadcast_to(wtab[h:h + 1, :], (tq, tk + 2 * LANES))
        out_ref[0, h] = pltpu.roll(wb, 1, 1, stride=1, stride_axis=0)[:, LANES:LANES + tk]


def dsa_bias_tiles(tbr):
    nd = DSA_DCAP // LANES + 1
    return pl.pallas_call(
        _dsa_bias_kernel, grid=(nd,),
        in_specs=[pl.BlockSpec(tbr.shape, lambda d: (0, 0))],
        out_specs=pl.BlockSpec((1, D_HEADS, DSA_BQ, DSA_BK), lambda d: (d, 0, 0, 0)),
        out_shape=jax.ShapeDtypeStruct((nd, D_HEADS, DSA_BQ, DSA_BK), F32),
        compiler_params=_cparams("parallel"), name="dsa_bias_tiles")(tbr)


def _dsa_prompt_kernel(q_ref, iq_ref, misc_ref, ikt_ref, kt_ref, vt_ref, *rest, n_sel):
    tq, tk = DSA_TQ, DSA_TK
    nqa, nkb = tq // DSA_BQ, tk // DSA_BK
    bias_refs = rest[:nqa * nkb]
    tri_ref, o_ref, keys_s, thr_s, rem_s, cnt_s, neg_s, m_s, l_s, acc_s, s_s, p_s, alpha_s = rest[nqa * nkb:]
    qb = pl.program_id(0)
    kt = pl.program_id(1)
    kl = (qb * tq + tq - 1) // tk

    @pl.when(kt == 0)
    def _():
        iq = iq_ref[...]
        iw = misc_ref[:, IDX_DIM:IDX_DIM + IDX_HEADS]
        qpos = qb * tq + lax.broadcasted_iota(I32, (tq, tk), 0)

        def scores(t, carry):
            off = pl.multiple_of(t * tk, tk)
            ikt = ikt_ref[:, pl.ds(off, tk)]
            sc = jnp.zeros((tq, tk), F32)
            for h in range(IDX_HEADS):
                s = _dot(iq[:, h * IDX_DIM:(h + 1) * IDX_DIM], ikt)
                sc = sc + jnp.maximum(s, 0.0) * iw[:, h:h + 1]
            kpos = off + lax.broadcasted_iota(I32, (tq, tk), 1)
            keys_s[:, pl.ds(off, tk)] = jnp.where(kpos <= qpos, _sortable(sc * IDX_SCALE), INT_MIN)
            return carry

        lax.fori_loop(0, kl + 1, scores, 0)
        thr, rem = _kth_largest(keys_s, kl + 1, tk, n_sel)
        thr_s[...] = thr
        rem_s[...] = rem
        cnt_s[...] = jnp.zeros_like(cnt_s)
        m_s[...] = jnp.full_like(m_s, NEG_INF)
        l_s[...] = jnp.zeros_like(l_s)
        acc_s[...] = jnp.zeros_like(acc_s)

    @pl.when(kt <= kl)
    def _():
        off = pl.multiple_of(kt * tk, tk)
        sel, cnt = _select(keys_s[:, pl.ds(off, tk)], thr_s[...], rem_s[...], cnt_s[...], tri_ref[...])
        cnt_s[...] = cnt
        neg_s[...] = jnp.where(sel, 0.0, NEG_INF)
        for h in range(D_HEADS):
            sb, pb = s_s.at[h % 2], p_s.at[h % 2]
            sb[...] = _dot(q_ref[h], kt_ref[h * HEAD_DIM:(h + 1) * HEAD_DIM, :])
            for r in range(0, tq, DSA_RC):
                rs = slice(r, r + DSA_RC)
                a, ro = r // DSA_BQ, r % DSA_BQ
                bias = jnp.concatenate([bias_refs[a * nkb + b][0, h, ro:ro + DSA_RC, :] for b in range(nkb)], axis=1)
                s = sb[rs, :] * (LOG2E * HEAD_DIM ** -0.5) + bias + neg_s[rs, :]
                m_old = m_s[h, rs]
                m_new = jnp.maximum(m_old, jnp.max(s, axis=1, keepdims=True))
                m_safe = jnp.where(m_new == NEG_INF, 0.0, m_new)
                p = jnp.exp2(s - m_safe)
                alpha = jnp.exp2(m_old - m_safe)
                l_s[h, rs] = alpha * l_s[h, rs] + jnp.sum(p, axis=1, keepdims=True)
                alpha_s[h % 2, rs] = alpha
                pb[rs, :] = p.astype(BF16)
                m_s[h, rs] = m_new
            acc_s[h] = alpha_s[h % 2] * acc_s[h] + _dot_nt(pb[...], vt_ref[h * HEAD_DIM:(h + 1) * HEAD_DIM, :])

    @pl.when(kt == kl)
    def _():
        o_ref[...] = jnp.concatenate([acc_s[h] / l_s[h] for h in range(D_HEADS)], axis=1)


def dsa_prompt(q, iq, misc, ikt, kt_, v, bias_tiles, tri, n_sel):
    seq = q.shape[1]
    tq, tk = DSA_TQ, DSA_TK
    nb, nkt = seq // tq, seq // tk
    last = lambda qb: (qb * tq + tq - 1) // tk
    qrow = lambda qb, kt: (qb, 0)
    fix = lambda qb, kt: (0, 0)

    def bias_spec(a, b):
        def idx(qb, kt):
            d = (qb * tq + a * DSA_BQ - jnp.minimum(kt, last(qb)) * tk - b * DSA_BK) // LANES
            return (jnp.clip(d, 0, DSA_DCAP // LANES), 0, 0, 0)
        return pl.BlockSpec((1, D_HEADS, DSA_BQ, DSA_BK), idx)

    bias_specs = [bias_spec(a, b) for a in range(tq // DSA_BQ) for b in range(tk // DSA_BK)]
    return pl.pallas_call(
        functools.partial(_dsa_prompt_kernel, n_sel=n_sel), grid=(nb, nkt),
        in_specs=[pl.BlockSpec((D_HEADS, tq, HEAD_DIM), lambda qb, kt: (0, qb, 0)),
                  pl.BlockSpec((tq, IDX_HEADS * IDX_DIM), qrow),
                  pl.BlockSpec((tq, LANES), qrow), pl.BlockSpec(ikt.shape, fix),
                  pl.BlockSpec((D_WIDTH, tk), lambda qb, kt: (0, jnp.minimum(kt, last(qb)))),
                  pl.BlockSpec((D_WIDTH, tk), lambda qb, kt: (0, jnp.minimum(kt, last(qb))))] +
                 bias_specs + [pl.BlockSpec(tri.shape, fix)],
        out_specs=pl.BlockSpec((tq, D_WIDTH), qrow),
        out_shape=jax.ShapeDtypeStruct((seq, D_WIDTH), F32),
        scratch_shapes=[pltpu.VMEM((tq, seq), I32), pltpu.VMEM((tq, 1), I32), pltpu.VMEM((tq, 1), F32),
                        pltpu.VMEM((tq, 1), F32), pltpu.VMEM((tq, tk), F32), pltpu.VMEM((D_HEADS, tq, 1), F32),
                        pltpu.VMEM((D_HEADS, tq, 1), F32), pltpu.VMEM((D_HEADS, tq, HEAD_DIM), F32),
                        pltpu.VMEM((2, tq, tk), F32), pltpu.VMEM((2, tq, tk), BF16), pltpu.VMEM((2, tq, 1), F32)],
        compiler_params=_cparams("parallel", "arbitrary"), name="dsa_prompt")(
            q, iq, misc, ikt, kt_, v, *([bias_tiles] * len(bias_specs)), tri)


def _dsa_step_keys_kernel(pt_ref, iq_ref, iw_ref, *rest, n_new, pg):
    page_refs, new_ref, keys_ref = rest[:pg], rest[pg], rest[pg + 1]
    p = pl.program_id(1)
    npage = pl.num_programs(1) * pg
    psz = new_ref.shape[2]
    rows = keys_ref.shape[1]

    def score(kt):
        r = jnp.maximum(_dot(iq_ref[0], kt.astype(BF16)), 0.0) * iw_ref[0][:, 0:1]
        sc = r[0:rows]
        for h in range(1, IDX_HEADS):
            sc = sc + r[h * rows:(h + 1) * rows]
        return _sortable(sc * IDX_SCALE)

    kt = jnp.concatenate([page_ref[0, 0] for page_ref in page_refs], axis=1)
    keys_ref[0, :, pl.ds(pl.multiple_of(p * (pg * psz), psz), pg * psz)] = score(kt)

    @pl.when(p == 0)
    def _():
        s_i = lax.broadcasted_iota(I32, (rows, psz), 0)
        j_i = lax.broadcasted_iota(I32, (rows, psz), 1)
        ok = (j_i <= s_i) & (j_i < n_new)
        keys_ref[0, :, pl.ds(pl.multiple_of(npage * psz, psz), psz)] = jnp.where(ok, score(new_ref[0]), INT_MIN)


def dsa_step_keys(page_table, iqm, iwm, cache_kidx, ik_new, layer, n_new, pg):
    nseq, npage = page_table.shape
    psz = cache_kidx.shape[3]
    rows = iqm.shape[1] // IDX_HEADS
    ncol = (npage + 1) * psz
    seqb = lambda n, p, pt: (n, 0, 0)
    page_spec = lambda i: pl.BlockSpec((1, 1, IDX_DIM, psz),
                                       lambda n, p, pt: (layer, pt[n * npage + p * pg + i], 0, 0))
    grid_spec = pltpu.PrefetchScalarGridSpec(
        num_scalar_prefetch=1, grid=(nseq, npage // pg),
        in_specs=[pl.BlockSpec((1,) + iqm.shape[1:], seqb), pl.BlockSpec((1,) + iwm.shape[1:], seqb)] +
                 [page_spec(i) for i in range(pg)] + [pl.BlockSpec((1,) + ik_new.shape[1:], seqb)],
        out_specs=pl.BlockSpec((1, rows, ncol), seqb))
    return pl.pallas_call(
        functools.partial(_dsa_step_keys_kernel, n_new=n_new, pg=pg), grid_spec=grid_spec,
        out_shape=jax.ShapeDtypeStruct((nseq, rows, ncol), I32),
        compiler_params=_cparams("parallel", "arbitrary"), name="dsa_step_keys")(
            page_table.reshape(-1), iqm, iwm, *([cache_kidx] * pg), ik_new)


def _topk_mask_kernel(keys_ref, tri_ref, neg_ref, *, tw, n_sel):
    thr, rem = _kth_largest(keys_ref, keys_ref.shape[1] // tw, tw, n_sel)
    w = tri_ref.shape[0]

    def chunk(c, cnt):
        off = pl.multiple_of(c * w, w)
        sel, cnt = _select(keys_ref[:, pl.ds(off, w)], thr, rem, cnt, tri_ref[...])
        neg_ref[:, pl.ds(off, w)] = jnp.where(sel, 0.0, NEG_INF)
        return cnt

    lax.fori_loop(0, keys_ref.shape[1] // w, chunk, jnp.zeros(thr.shape, F32))


def topk_mask_rows(keys, tri, tw, n_sel, tr):
    rows, ncol = keys.shape
    return pl.pallas_call(
        functools.partial(_topk_mask_kernel, tw=tw, n_sel=n_sel), grid=(rows // tr,),
        in_specs=[pl.BlockSpec((tr, ncol), lambda i: (i, 0)), pl.BlockSpec(tri.shape, lambda i: (0, 0))],
        out_specs=pl.BlockSpec((tr, ncol), lambda i: (i, 0)),
        out_shape=jax.ShapeDtypeStruct((rows, ncol), F32),
        compiler_params=_cparams("parallel"), name="topk_mask_rows")(keys, tri)


def _dsa_step_attn_kernel(pt_ref, q_ref, neg_ref, neg_new_ref, bias_ref, bias_new_ref, *rest, pg):
    k_refs, v_refs = rest[:pg], rest[pg:2 * pg]
    kn_ref, vn_ref, o_ref, m_s, l_s, acc_s = rest[2 * pg:]
    p = pl.program_id(1)
    nstep = pl.num_programs(1) - 1
    rows = neg_ref.shape[1]

    @pl.when(p == 0)
    def _():
        m_s[...] = jnp.full_like(m_s, NEG_INF)
        l_s[...] = jnp.zeros_like(l_s)
        acc_s[...] = jnp.zeros_like(acc_s)

    def process(kt, vt, neg, bias):
        s = _dot(q_ref[0], kt.astype(BF16)) * (LOG2E * HEAD_DIM ** -0.5) + bias
        s = s + jnp.concatenate([neg] * D_HEADS, axis=0)
        m_old = m_s[...]
        m_new = jnp.maximum(m_old, jnp.max(s, axis=1, keepdims=True))
        m_safe = jnp.where(m_new == NEG_INF, 0.0, m_new)
        pr = jnp.exp2(s - m_safe)
        alpha = jnp.exp2(m_old - m_safe)
        l_s[...] = alpha * l_s[...] + jnp.sum(pr, axis=1, keepdims=True)
        acc_s[...] = alpha * acc_s[...] + _dot_nt(pr.astype(BF16), vt.astype(BF16))
        m_s[...] = m_new

    @pl.when(p < nstep)
    def _():
        process(jnp.concatenate([r[0, 0, 0] for r in k_refs], axis=1),
                jnp.concatenate([r[0, 0, 0] for r in v_refs], axis=1), neg_ref[0], bias_ref[...])

    @pl.when(p == nstep)
    def _():
        process(kn_ref[0, 0], vn_ref[0, 0], neg_new_ref[0], bias_new_ref[...])
        accn = acc_s[...] / l_s[...]
        head = lax.broadcasted_iota(I32, (rows, D_WIDTH), 1) // HEAD_DIM
        out = jnp.zeros((rows, D_WIDTH), F32)
        for h in range(D_HEADS):
            out = out + jnp.where(head == h, accn[h * rows:(h + 1) * rows], 0.0)
        o_ref[0] = out


def dsa_step_attn(page_table, qbd, neg, bias, cache_kvt, kvt_new, layer, pg):
    nseq, npage = page_table.shape
    psz = cache_kvt.shape[4]
    rows = neg.shape[1]
    qr = qbd.shape[1]
    nstep = npage // pg
    seqb = lambda n, p, pt: (n, 0, 0)
    stp = lambda p: jnp.minimum(p, nstep - 1)

    def page_spec(i, part):
        return pl.BlockSpec((1, 1, 1, D_WIDTH, psz),
                            lambda n, p, pt: (layer, pt[n * npage + stp(p) * pg + i], part, 0, 0))

    def new_spec(part):
        return pl.BlockSpec((1, 1, D_WIDTH, psz), lambda n, p, pt: (n, part, 0, 0))

    grid_spec = pltpu.PrefetchScalarGridSpec(
        num_scalar_prefetch=1, grid=(nseq, nstep + 1),
        in_specs=[pl.BlockSpec((1, qr, D_WIDTH), seqb),
                  pl.BlockSpec((1, rows, pg * psz), lambda n, p, pt: (n, 0, stp(p))),
                  pl.BlockSpec((1, rows, psz), lambda n, p, pt: (n, 0, npage)),
                  pl.BlockSpec((qr, pg * psz), lambda n, p, pt: (0, stp(p))),
                  pl.BlockSpec((qr, psz), lambda n, p, pt: (0, npage))] +
                 [page_spec(i, 0) for i in range(pg)] + [page_spec(i, 1) for i in range(pg)] +
                 [new_spec(0), new_spec(1)],
        out_specs=pl.BlockSpec((1, rows, D_WIDTH), seqb),
        scratch_shapes=[pltpu.VMEM((qr, 1), F32), pltpu.VMEM((qr, 1), F32), pltpu.VMEM((qr, D_WIDTH), F32)])
    return pl.pallas_call(
        functools.partial(_dsa_step_attn_kernel, pg=pg), grid_spec=grid_spec,
        out_shape=jax.ShapeDtypeStruct((nseq, rows, D_WIDTH), F32),
        compiler_params=_cparams("parallel", "arbitrary"), name="dsa_step_attn")(
            page_table.reshape(-1), qbd, neg, neg, bias, bias, *([cache_kvt] * (2 * pg)), kvt_new, kvt_new)


def _prep_ab_w(w):
    d = w.shape[0]
    aq, ak, av, bq, bk, bv, bg = _split_cols(w, [A_WIDTH] * 3 + [B_QK_WIDTH] * 2 + [B_V_WIDTH] * 2)
    half = B_KEY_DIM // 2

    def rot(m):
        return m.reshape(d, B_HEADS, 2, half)[:, :, ::-1, :].reshape(d, B_QK_WIDTH)

    pad = jnp.zeros((d, B_QK_PAD - B_QK_WIDTH), w.dtype)
    cols = [aq, ak, av, bq, pad, rot(bq), pad, bk, pad, rot(bk), pad, bv, bg]
    return jnp.concatenate(cols, axis=1).astype(BF16)


AB_SEGS = ((0, 3 * A_WIDTH, F32), (3 * A_WIDTH, 4 * B_QK_PAD, F32),
           (3 * A_WIDTH + 4 * B_QK_PAD, B_V_WIDTH, F32), (3 * A_WIDTH + 4 * B_QK_PAD + B_V_WIDTH, B_V_WIDTH, F32))


def _split_cols(x, widths):
    parts, start = [], 0
    for wd in widths:
        parts.append(x[..., start:start + wd])
        start += wd
    return parts


def _prep_cd_w(w):
    d = w.shape[0]
    cu, dq, dk, dv, iq, iw, ik = _split_cols(w, [C_WIDTH, D_WIDTH, D_WIDTH, D_WIDTH, IDX_HEADS * IDX_DIM, IDX_HEADS, IDX_DIM])
    pad = jnp.zeros((d, LANES - IDX_DIM - IDX_HEADS), w.dtype)
    return jnp.concatenate([cu, dq, dk, dv, iq, ik, iw, pad], axis=1).astype(BF16)


CD_SEGS = ((0, C_WIDTH, F32), (C_WIDTH, D_WIDTH, BF16), (C_WIDTH + D_WIDTH, 2 * D_WIDTH, F32),
           (C_WIDTH + 3 * D_WIDTH, IDX_HEADS * IDX_DIM, BF16), (C_WIDTH + 3 * D_WIDTH + IDX_HEADS * IDX_DIM, LANES, F32))


def _bias_by_distance(t5_table, dist):
    onehot = (_t5_bucket(dist)[..., None] == jnp.arange(T5_BUCKETS)).astype(F32)
    return jnp.einsum('...b,bh->...h', onehot, t5_table.astype(F32), precision=lax.Precision.HIGHEST)


def _dil_group_bias(t5_table, g, dil, nkey):
    return _bias_by_distance(t5_table, jnp.arange(nkey + 1) * dil)[:, g * A_GROUP_HEADS:(g + 1) * A_GROUP_HEADS]


def _dil_prompt_bias(t5_table, g, dil, nkey):
    bias = _dil_group_bias(t5_table, g, dil, nkey)
    ext = jnp.concatenate([jnp.broadcast_to(bias[:1], (nkey - 1, A_GROUP_HEADS)), bias,
                           jnp.broadcast_to(bias[-1:], (nkey - 1, A_GROUP_HEADS))], axis=0)[::-1]
    rows = [lax.slice_in_dim(ext, nkey - 1 - qi, nkey - 1 - qi + 2 * nkey, axis=0) for qi in range(nkey)]
    return jnp.moveaxis(jnp.stack(rows, axis=0), -1, 0)


def _dil_step_bias(t5_table, g, dil, nkey, n_buf, n_new, rows, pad_new):
    assert n_buf == nkey * dil
    rev = _dil_group_bias(t5_table, g, dil, nkey)[::-1]
    width = n_buf + pad_new
    out = []
    for s in range(rows):
        if s < n_new:
            blockc = jnp.full((nkey + 1, dil, A_GROUP_HEADS), NEG_INF, F32).at[:, s % dil, :].set(rev)
            flat = blockc.reshape((nkey + 1) * dil, A_GROUP_HEADS)
            front = s - s % dil
            row = jnp.pad(flat, ((front, width - front - flat.shape[0]), (0, 0)), constant_values=NEG_INF)
        else:
            row = jnp.full((width, A_GROUP_HEADS), NEG_INF, F32).at[0].set(0.0)
        out.append(row)
    return jnp.stack(out, axis=0).transpose(2, 0, 1).reshape(A_GROUP_HEADS * rows, width)


def _dsa_prompt_table(t5_table):
    d = (DSA_DCAP + LANES - 1) - jnp.arange(DSA_TAB)
    return _bias_by_distance(t5_table, d)[:, :D_HEADS].T * LOG2E


def _tri(n):
    r = np.arange(n)
    return jnp.asarray((r[:, None] <= r[None, :]).astype(np.float32)).astype(BF16)


def _ffn_and_ple(h, p, i, prm, tm, final):
    h = moe(h, prm['norm_ffn'][i], prm['moe_wr'][i], prm['moe_br'][i], prm['moe_wg'][i], prm['moe_wu'][i],
            prm['moe_wd'][i], min(4 * tm, h.shape[0]))
    return ple(h, p, prm['norm_ple'][i], prm['ple_wg'][i], prm['ple_wp'][i], prm['norm_final'] if final else None, tm)


def _layer_ab_prompt(h, prm, j, tm):
    seq = h.shape[0]
    aqkv, bqk, bv, bg = norm_matmul(h, prm['norm_mix'][2 * j], prm['ab_w'][j], AB_SEGS, tm)
    os_, ls, bufs = [], [], []
    for g, (win, dil) in enumerate(A_PATTERNS):
        nkey = win // dil
        assert seq % (dil * nkey) == 0
        o, lse = dil_prompt(aqkv, _dil_prompt_bias(prm['t5_table'], g, dil, nkey), g, dil)
        os_.append(o)
        ls.append(lse)
        kcols = aqkv[seq - min(win, seq):, A_WIDTH + g * LANES:A_WIDTH + (g + 1) * LANES]
        vcols = aqkv[seq - min(win, seq):, 2 * A_WIDTH + g * LANES:2 * A_WIDTH + (g + 1) * LANES]
        bufs.append(jnp.stack([kcols, vcols], axis=1).reshape(1, -1, 2, A_GROUP_HEADS, HEAD_DIM))
    chunk = B_CHUNK if seq % B_CHUNK == 0 else seq
    cosv, sinv = _rope_tables(jnp.arange(seq))
    s0 = jnp.zeros((1, B_HEADS, B_KEY_DIM, B_VAL_DIM), F32)
    b_out, s_new = retention(bqk, cosv, sinv, bv, bg, _retention_tables(chunk, chunk), prm['ab_gn_gain'][j], s0,
                             chunk, seq // chunk)
    h = ab_out(h, os_, ls, b_out, prm['ab_w_out'][j], tm)
    return h, bufs, s_new


def _layer_ab_sample(h, prm, j, past, nseq, n_new, pos0):
    rows = 8
    aqkv, bqk, bv, bg = norm_matmul(h, prm['norm_mix'][2 * j], prm['ab_w'][j], AB_SEGS, h.shape[0])
    a3 = aqkv.reshape(nseq, n_new, 3 * A_WIDTH)
    pad_rows = lambda x, r: jnp.pad(x, ((0, 0), (0, r - x.shape[1]), (0, 0)))
    os_, ls, bufs = [], [], []
    for g, (win, dil) in enumerate(A_PATTERNS):
        nkey = win // dil
        buf = past['a'][g]
        n_buf = buf.shape[2]
        cs = lambda part: slice(part * A_WIDTH + g * LANES, part * A_WIDTH + (g + 1) * LANES)
        q = pad_rows(a3[:, :, cs(0)], rows)
        kv_new = jnp.concatenate([a3[:, :, cs(1)], a3[:, :, cs(2)]], axis=2)
        bias = _dil_step_bias(prm['t5_table'], g, dil, nkey, n_buf, n_new, rows, LANES)
        buf_t = buf.transpose(0, 1, 3, 4, 5, 2).reshape(buf.shape[0], nseq, 2, LANES, n_buf)
        new_t = jnp.pad(kv_new.reshape(nseq, n_new, 2, LANES).transpose(0, 2, 3, 1),
                        ((0, 0), (0, 0), (0, 0), (0, LANES - n_new)))
        o, lse = dil_step(q, buf_t, new_t, bias, j)
        os_.append(o[:, :n_new].reshape(nseq * n_new, LANES))
        ls.append(lse[:, :n_new].reshape(nseq * n_new, LANES))
        kv_all = jnp.concatenate([buf[j].reshape(nseq, n_buf, 2 * LANES), kv_new], axis=1)
        bufs.append(kv_all[:, n_new:].reshape(nseq, n_buf, 2, A_GROUP_HEADS, HEAD_DIM))
    chunk = B_CHUNK if n_new % B_CHUNK == 0 else n_new
    assert chunk == n_new
    cpad = 16
    cosv, sinv = _rope_tables(pos0 + jnp.arange(cpad))
    padc = lambda x: jnp.pad(x.reshape(nseq, n_new, -1), ((0, 0), (0, cpad - n_new), (0, 0))).reshape(nseq * cpad, -1)
    b_out, s_new = retention(padc(bqk), cosv, sinv, padc(bv), padc(bg), _retention_tables(chunk, cpad),
                             prm['ab_gn_gain'][j], past['b'][j].astype(F32), cpad, 1)
    b_out = b_out.reshape(nseq, cpad, -1)[:, :n_new].reshape(nseq * n_new, -1)
    h = ab_out(h, os_, ls, b_out, prm['ab_w_out'][j], h.shape[0])
    return h, bufs, s_new


def _cd_project(h, prm, j, tm):
    return norm_matmul(h, prm['norm_mix'][2 * j + 1], prm['cd_w'][j], CD_SEGS, tm)


def _layer_cd_prompt(h, prm, j, tm):
    seq = h.shape[0]
    cu, dq, kv, iq, misc = _cd_project(h, prm, j, tm)
    x0 = jnp.zeros((1, C_FLAT), F32)
    c_out, xr, xi = s5(cu, prm['s5'][j], x0, x0, 1, min(256, seq))
    sc = jnp.stack([xr.reshape(C_GROUPS, C_STATE), xi.reshape(C_GROUPS, C_STATE)], axis=-1)[None]
    n_sel = min(IDX_TOPK, seq // 4)
    heads_first = lambda x: x.reshape(seq, D_HEADS, HEAD_DIM).transpose(1, 0, 2)
    kb = kv[:, :D_WIDTH].astype(BF16).T
    vb = kv[:, D_WIDTH:].astype(BF16).T
    ikt = misc[:, :IDX_DIM].astype(BF16).T
    bias_tiles = dsa_bias_tiles(_dsa_prompt_table(prm['t5_table']))
    d_out = dsa_prompt(heads_first(dq), iq, misc, ikt, kb, vb, bias_tiles, _tri(DSA_TRI), n_sel)
    h = cd_out(h, c_out, d_out, prm['cd_w_out'][j], tm)
    return h, sc, kv, misc[:, :IDX_DIM]


def _layer_cd_sample(h, prm, j, past, nseq, n_new):
    rows = 8
    cu, dq, kv, iq, misc = _cd_project(h, prm, j, h.shape[0])
    tmaj = lambda x: x.reshape(nseq, n_new, -1).transpose(1, 0, 2).reshape(nseq * n_new, -1)
    x0 = past['c'][j].astype(F32)
    c_out, xr, xi = s5(tmaj(cu), prm['s5'][j], x0[..., 0].reshape(nseq, C_FLAT), x0[..., 1].reshape(nseq, C_FLAT),
                       nseq, n_new)
    c_out = c_out.reshape(n_new, nseq, -1).transpose(1, 0, 2).reshape(nseq * n_new, -1)
    sc = jnp.stack([xr.reshape(nseq, C_GROUPS, C_STATE), xi.reshape(nseq, C_GROUPS, C_STATE)], axis=-1)

    page_table = past['page_table']
    cache_kv, cache_kidx = past['d_kv'], past['d_kidx']
    npage = page_table.shape[1]
    psz = cache_kv.shape[2]
    n_past = npage * psz
    n_sel = min(IDX_TOPK, (n_past + n_new) // 4)
    pad_rows = lambda x, r: jnp.pad(x, ((0, 0), (0, r - x.shape[1]), (0, 0)))
    ik_new = misc[:, :IDX_DIM]
    iqm = pad_rows(iq.reshape(nseq, n_new, IDX_HEADS * IDX_DIM), rows).reshape(nseq, rows, IDX_HEADS, IDX_DIM)
    iqm = iqm.transpose(0, 2, 1, 3).reshape(nseq, IDX_HEADS * rows, IDX_DIM)
    iwm = pad_rows(misc[:, IDX_DIM:IDX_DIM + IDX_HEADS].reshape(nseq, n_new, IDX_HEADS), rows)
    iwm = jnp.broadcast_to(iwm.transpose(0, 2, 1).reshape(nseq, IDX_HEADS * rows, 1), (nseq, IDX_HEADS * rows, LANES))
    kidx_t = cache_kidx.transpose(0, 1, 3, 2)
    kv_t = cache_kv.transpose(0, 1, 3, 4, 5, 2).reshape(cache_kv.shape[0], cache_kv.shape[1], 2, D_WIDTH, psz)
    lane_pad = lambda x: jnp.pad(x, ((0, 0),) * (x.ndim - 1) + ((0, psz - x.shape[-1]),))
    ik_new_t = lane_pad(ik_new.reshape(nseq, n_new, IDX_DIM).transpose(0, 2, 1))
    keys = dsa_step_keys(page_table, iqm, iwm, kidx_t, ik_new_t, j, n_new, math.gcd(npage, 16))
    ncol = keys.shape[2]
    neg = topk_mask_rows(keys.reshape(nseq * rows, ncol), _tri(LANES),
                         ncol // 3 if (ncol // LANES) % 3 == 0 else LANES, n_sel, min(LANES, nseq * rows))
    qpos = n_past + jnp.minimum(jnp.arange(rows), n_new - 1)
    dist = qpos[:, None] - jnp.arange(ncol)[None, :]
    bias = _bias_by_distance(prm['t5_table'], dist)[..., :D_HEADS] * LOG2E
    bias = bias.transpose(2, 0, 1).reshape(D_HEADS * rows, ncol)
    q = pad_rows(dq.reshape(nseq, n_new, D_WIDTH), rows)
    col_head = jnp.arange(D_WIDTH) // HEAD_DIM
    qbd = jnp.where(col_head[None, None, None, :] == jnp.arange(D_HEADS)[None, :, None, None], q[:, None], 0)
    qbd = qbd.reshape(nseq, D_HEADS * rows, D_WIDTH).astype(dq.dtype)
    kvt_new = lane_pad(kv.reshape(nseq, n_new, 2, D_WIDTH).transpose(0, 2, 3, 1))
    d_out = dsa_step_attn(page_table, qbd, neg.reshape(nseq, rows, ncol), bias, kv_t, kvt_new, j, math.gcd(npage, 16))
    d_out = d_out[:, :n_new].reshape(nseq * n_new, D_WIDTH)
    h = cd_out(h, c_out, d_out, prm['cd_w_out'][j], h.shape[0])
    return h, sc, kv, ik_new


def _trunk(x, p, prm, past, nseq, n_new, pos0, tm):
    depth = p.shape[0]
    h = x
    a_new = [[] for _ in A_PATTERNS]
    b_new, c_new, kv_new, kidx_new = [], [], [], []
    for i in range(depth):
        j = i // 2
        if i % 2 == 0:
            if past is None:
                h, bufs, sb = _layer_ab_prompt(h, prm, j, tm)
            else:
                h, bufs, sb = _layer_ab_sample(h, prm, j, past, nseq, n_new, pos0)
            for g in range(len(A_PATTERNS)):
                a_new[g].append(bufs[g])
            b_new.append(sb)
        else:
            if past is None:
                h, sc, kv, kidx = _layer_cd_prompt(h, prm, j, tm)
            else:
                h, sc, kv, kidx = _layer_cd_sample(h, prm, j, past, nseq, n_new)
            c_new.append(sc)
            kv_new.append(kv.reshape(nseq, n_new, 2, D_HEADS, HEAD_DIM))
            kidx_new.append(kidx.reshape(nseq, n_new, IDX_DIM))
        h = _ffn_and_ple(h, p[i], i, prm, tm, i == depth - 1)
    stack = lambda t: jnp.stack(t, axis=0)
    return h, tuple(stack(t) for t in a_new), stack(b_new), stack(c_new), stack(kv_new), stack(kidx_new)


def kernel(x_prompt, x_sample, p_prompt, p_sample, cache_a_kv0, cache_a_kv1, cache_a_kv2, state_b, state_c, cache_d_kv, cache_d_kidx, page_table, t5_table, norm_mix, norm_ffn, norm_ple, norm_final, ab_w_in, ab_w_out, ab_gn_gain, cd_w_in, cd_w_out, c_lambda_re, c_lambda_im, c_log_dt, c_b_re, c_b_im, c_c_re, c_c_im, c_d, c_w_glu, c_b_glu, moe_w_group, moe_b_group, moe_w_expert, moe_b_expert, moe_w_gate, moe_w_up, moe_w_down, ple_w_gate, ple_w_proj):
    depth, d_model = norm_mix.shape
    n_ab, n_cd = ab_w_in.shape[0], cd_w_in.shape[0]
    rpad = jnp.zeros((depth, d_model, LANES - MOE_GROUPS - MOE_EXPERTS), F32)
    prm = dict(
        t5_table=t5_table, norm_mix=norm_mix, norm_ffn=norm_ffn, norm_ple=norm_ple, norm_final=norm_final,
        ab_w=[_prep_ab_w(ab_w_in[j]) for j in range(n_ab)], ab_w_out=ab_w_out.astype(BF16), ab_gn_gain=ab_gn_gain,
        cd_w=[_prep_cd_w(cd_w_in[j]) for j in range(n_cd)], cd_w_out=cd_w_out.astype(BF16),
        s5=[_s5_prep(c_lambda_re[j], c_lambda_im[j], c_log_dt[j], c_b_re[j], c_b_im[j], c_c_re[j], c_c_im[j],
                     c_d[j], c_w_glu[j], c_b_glu[j]) for j in range(n_cd)],
        moe_wr=jnp.concatenate([moe_w_group, moe_w_expert.reshape(depth, d_model, MOE_EXPERTS), rpad],
                               axis=2).astype(BF16),
        moe_br=jnp.concatenate([moe_b_group, moe_b_expert.reshape(depth, MOE_EXPERTS),
                                jnp.zeros((depth, LANES - MOE_GROUPS - MOE_EXPERTS), F32)], axis=1)[:, None, :],
        moe_wg=moe_w_gate.astype(BF16), moe_wu=moe_w_up.astype(BF16), moe_wd=moe_w_down.astype(BF16),
        ple_wg=ple_w_gate.astype(BF16), ple_wp=ple_w_proj.astype(BF16))

    bsz, seq, _ = x_prompt.shape
    assert bsz == 1
    y_p, a_p, sb_p, sc_p, dkv_p, dki_p = _trunk(x_prompt[0], p_prompt[:, 0], prm, None, 1, seq, 0, min(256, seq))
    y_p = y_p[None]

    nseq, n_new, _ = x_sample.shape
    past = dict(a=(cache_a_kv0, cache_a_kv1, cache_a_kv2), b=state_b, c=state_c, d_kv=cache_d_kv, d_kidx=cache_d_kidx,
                page_table=page_table)
    n_past = page_table.shape[1] * cache_d_kv.shape[2]
    rows_s = nseq * n_new
    y_s, a_s, sb_s, sc_s, dkv_s, dki_s = _trunk(x_sample.reshape(rows_s, d_model),
                                                p_sample.reshape(depth, rows_s, -1), prm, past, nseq, n_new, n_past,
                                                rows_s)
    y_s = y_s.reshape(nseq, n_new, d_model)
    return (y_p, y_s, a_p[0], a_s[0], a_p[1], a_s[1], a_p[2], a_s[2], sb_p, sb_s, sc_p, sc_s, dkv_p, dkv_s, dki_p, dki_s)
```

```python
import functools
import math

import numpy as np
import jax
import jax.numpy as jnp
from jax import lax
from jax.experimental import pallas as pl
from jax.experimental.pallas import tpu as pltpu

F32 = jnp.float32
BF16 = jnp.bfloat16
I32 = jnp.int32

EPS = 1e-6
HEAD_DIM = 64
A_PATTERNS = ((128, 1), (512, 4), (2048, 16))
A_GROUP_HEADS = 2
A_WIDTH = 384
B_HEADS = 5
B_KEY_DIM = 64
B_VAL_DIM = 128
B_QK_WIDTH = 320
B_QK_PAD = 384
B_V_WIDTH = 640
B_CHUNK = 128
ROPE_BASE = 10000.0
C_GROUPS = 32
C_GROUP_SIZE = 16
C_WIDTH = 512
C_STATE = 64
C_FLAT = C_GROUPS * C_STATE
C_SUPER = 8
D_HEADS = 8
D_WIDTH = 512
IDX_HEADS = 4
IDX_DIM = 64
IDX_TOPK = 256
IDX_SCALE = (IDX_HEADS * IDX_DIM) ** -0.5
T5_BUCKETS = 32
T5_MAX_DIST = 2048
MOE_GROUPS = 4
MOE_GROUP_EXPERTS = 4
MOE_EXPERTS = 16
D_EXPERT = 512
LANES = 128
INT_MIN = -(2 ** 31)
NEG_INF = float("-inf")
LOG2E = math.log2(math.e)

VMEM_LIMIT_BYTES = 52 * 1024 * 1024

DSA_TQ = 256
DSA_TK = 1024
DSA_BQ = 128
DSA_BK = 512
DSA_DCAP = T5_MAX_DIST + DSA_BK
DSA_TAB = DSA_DCAP + DSA_BK + 2 * LANES
DSA_TRI = 256
DSA_RC = 32
KTH_ROWS = 128
DIL_BLOCKS_PER_STEP = 8


def _cparams(*sem):
    return pltpu.CompilerParams(dimension_semantics=sem, vmem_limit_bytes=VMEM_LIMIT_BYTES)


def _rms(x, g):
    return x * lax.rsqrt(jnp.mean(x * x, axis=-1, keepdims=True) + EPS) * g


def _dot(a, b):
    return jnp.dot(a, b, preferred_element_type=F32)


def _dot_nt(a, b):
    return lax.dot_general(a, b, (((1,), (1,)), ((), ())), preferred_element_type=F32)


def _dot_tn(a, b):
    return lax.dot_general(a, b, (((0,), (0,)), ((), ())), preferred_element_type=F32)


def _sigmoid(x):
    return 1.0 / (1.0 + jnp.exp(-x))


def _t5_bucket(dist):
    max_exact = T5_BUCKETS // 2
    d = jnp.maximum(dist, 0)
    scaled = jnp.log(jnp.maximum(d, 1).astype(F32) / max_exact) / math.log(T5_MAX_DIST / max_exact)
    large = jnp.minimum(max_exact + (scaled * (T5_BUCKETS - max_exact)).astype(I32), T5_BUCKETS - 1)
    return jnp.where(d < max_exact, d, large)


def _norm_matmul_kernel(x_ref, g_ref, w_ref, *out_refs, segs):
    xn = _rms(x_ref[...], g_ref[...]).astype(BF16)
    for o_ref, (start, width) in zip(out_refs, segs):
        o_ref[...] = _dot(xn, w_ref[:, start:start + width]).astype(o_ref.dtype)


def norm_matmul(x, gain, w, segs, tm):
    t, d = x.shape
    n = w.shape[1]
    kern = functools.partial(_norm_matmul_kernel, segs=tuple((s, wd) for s, wd, _ in segs))
    return pl.pallas_call(
        kern, grid=(t // tm,),
        in_specs=[pl.BlockSpec((tm, d), lambda i: (i, 0)),
                  pl.BlockSpec((1, d), lambda i: (0, 0)),
                  pl.BlockSpec((d, n), lambda i: (0, 0))],
        out_specs=[pl.BlockSpec((tm, wd), lambda i: (i, 0)) for _, wd, _ in segs],
        out_shape=[jax.ShapeDtypeStruct((t, wd), dt) for _, wd, dt in segs],
        compiler_params=_cparams("parallel"), name="norm_matmul")(x, gain.reshape(1, d), w)


def _dil_prompt_kernel(q_ref, kc_ref, kp_ref, vc_ref, vp_ref, bias_ref, o_ref, lse_ref):
    n = pl.program_id(1)
    blk = kp_ref.shape[0]
    nblk = q_ref.shape[0] // blk
    qi = lax.broadcasted_iota(I32, (blk, 2 * blk), 0)
    kj = lax.broadcasted_iota(I32, (blk, 2 * blk), 1)
    m = blk + qi - kj
    band = (m >= 0) & (m <= blk)
    for i in range(nblk):
        rows = slice(i * blk, (i + 1) * blk)
        q = q_ref[rows, :]
        if i == 0:
            k_prev, v_prev = kp_ref[...], vp_ref[...]
            mask = band & (kj >= jnp.where(n > 0, 0, blk))
        else:
            prev_rows = slice((i - 1) * blk, i * blk)
            k_prev, v_prev = kc_ref[prev_rows, :], vc_ref[prev_rows, :]
            mask = band
        kw = jnp.concatenate([k_prev, kc_ref[rows, :]], axis=0)
        vw = jnp.concatenate([v_prev, vc_ref[rows, :]], axis=0)
        outs, lses = [], []
        for h in range(A_GROUP_HEADS):
            cs = slice(h * HEAD_DIM, (h + 1) * HEAD_DIM)
            s = _dot_nt(q[:, cs].astype(BF16), kw[:, cs].astype(BF16)) * (HEAD_DIM ** -0.5) + bias_ref[h]
            s = jnp.where(mask, s, NEG_INF)
            mx = jnp.max(s, axis=1, keepdims=True)
            p = jnp.exp(s - mx)
            l = jnp.sum(p, axis=1, keepdims=True)
            outs.append(_dot((p / l).astype(BF16), vw[:, cs].astype(BF16)))
            lses.append(jnp.broadcast_to(mx + jnp.log(l), (blk, HEAD_DIM)))
        o_ref[rows, :] = jnp.concatenate(outs, axis=1)
        lse_ref[rows, :] = jnp.concatenate(lses, axis=1)


def dil_prompt(aqkv, bias_qk, g, dil):
    seq = aqkv.shape[0]
    blk = bias_qk.shape[1]
    nb = seq // (blk * dil)
    per = math.gcd(nb, DIL_BLOCKS_PER_STEP)
    ncol = aqkv.shape[1] // LANES
    a = aqkv.reshape(seq // dil, dil * aqkv.shape[1])
    ng = len(A_PATTERNS)

    def cur(off):
        return pl.BlockSpec((per * blk, LANES), lambda r, n: (n, r * ncol + off + g))

    def prev(off):
        return pl.BlockSpec((blk, LANES), lambda r, n: (jnp.maximum(n * per - 1, 0), r * ncol + off + g))

    o, lse = pl.pallas_call(
        _dil_prompt_kernel, grid=(dil, nb // per),
        in_specs=[cur(0), cur(ng), prev(ng), cur(2 * ng), prev(2 * ng),
                  pl.BlockSpec(bias_qk.shape, lambda r, n: (0, 0, 0))],
        out_specs=[pl.BlockSpec((per * blk, LANES), lambda r, n: (n, r))] * 2,
        out_shape=[jax.ShapeDtypeStruct((seq // dil, dil * LANES), F32)] * 2,
        compiler_params=_cparams("parallel", "parallel"), name="dil_prompt")(a, a, a, a, a, bias_qk)
    return o.reshape(seq, LANES), lse.reshape(seq, LANES)


def _dil_step_kernel(q_ref, buf_ref, new_ref, bias_ref, o_ref, lse_ref):
    q = q_ref[0]
    rows = q.shape[0]
    head = lax.broadcasted_iota(I32, q.shape, 1) // HEAD_DIM
    qbd = jnp.concatenate([jnp.where(head == h, q, 0.0) for h in range(A_GROUP_HEADS)], axis=0).astype(BF16)
    kt = jnp.concatenate([buf_ref[0, 0, 0], new_ref[0, 0]], axis=1).astype(BF16)
    vt = jnp.concatenate([buf_ref[0, 0, 1], new_ref[0, 1]], axis=1).astype(BF16)
    s = _dot(qbd, kt) * (HEAD_DIM ** -0.5) + bias_ref[...]
    mx = jnp.max(s, axis=1, keepdims=True)
    p = jnp.exp(s - mx)
    l = jnp.sum(p, axis=1, keepdims=True)
    o = _dot_nt((p / l).astype(BF16), vt)
    lse = jnp.broadcast_to(mx + jnp.log(l), o.shape)
    o_ref[0] = jnp.where(head == 0, o[:rows], o[rows:])
    lse_ref[0] = jnp.where(head == 0, lse[:rows], lse[rows:])


def dil_step(q, buf, new, bias, layer):
    nseq, rows, _ = q.shape
    win = buf.shape[4]
    return pl.pallas_call(
        _dil_step_kernel, grid=(nseq,),
        in_specs=[pl.BlockSpec((1, rows, LANES), lambda n: (n, 0, 0)),
                  pl.BlockSpec((1, 1, 2, LANES, win), lambda n: (layer, n, 0, 0, 0)),
                  pl.BlockSpec((1,) + new.shape[1:], lambda n: (n, 0, 0, 0)),
                  pl.BlockSpec(bias.shape, lambda n: (0, 0))],
        out_specs=[pl.BlockSpec((1, rows, LANES), lambda n: (n, 0, 0))] * 2,
        out_shape=[jax.ShapeDtypeStruct((nseq, rows, LANES), F32)] * 2,
        compiler_params=_cparams("parallel"), name="dil_step")(q, buf, new, bias)


def _retention_kernel(bqk_ref, cos_ref, sin_ref, bv_ref, bg_ref, decay_ref, win_ref, wend_ref, gch_ref, gain_ref,
                      s0_ref, out_ref, s_ref):
    c = pl.program_id(1)

    @pl.when(c == 0)
    def _():
        s_ref[...] = s0_ref[...]

    bqk = bqk_ref[...]
    cosv, sinv = cos_ref[...], sin_ref[...]
    w = B_QK_PAD
    q = (bqk[:, 0:w] * cosv + bqk[:, w:2 * w] * sinv) * (B_KEY_DIM ** -0.5)
    k = bqk[:, 2 * w:3 * w] * cosv + bqk[:, 3 * w:4 * w] * sinv
    q_in = q * win_ref[...]
    k_end = k * wend_ref[...]
    bv, bg, gain = bv_ref[...], bg_ref[...], gain_ref[...]
    outs = []
    for h in range(B_HEADS):
        ks = slice(h * B_KEY_DIM, (h + 1) * B_KEY_DIM)
        vs = slice(h * B_VAL_DIM, (h + 1) * B_VAL_DIM)
        vh = bv[:, vs].astype(BF16)
        state = s_ref[0, h]
        intra = _dot_nt(q[:, ks].astype(BF16), k[:, ks].astype(BF16)) * decay_ref[h]
        o = _dot(intra.astype(BF16), vh) + _dot(q_in[:, ks].astype(BF16), state.astype(BF16))
        s_ref[0, h] = gch_ref[h] * state + _dot_tn(k_end[:, ks].astype(BF16), vh)
        mu = jnp.mean(o, axis=1, keepdims=True)
        var = jnp.mean((o - mu) ** 2, axis=1, keepdims=True)
        on = (o - mu) * lax.rsqrt(var + EPS) * gain[:, vs]
        g = bg[:, vs]
        outs.append(g * _sigmoid(g) * on)
    out_ref[...] = jnp.concatenate(outs, axis=1)


def retention(bqk, cosv, sinv, bv, bg, tabs, gain, s0, chunk, nchunk):
    nseq = s0.shape[0]
    rows = bqk.shape[0]
    decay, w_in, w_end, g_chunk = tabs
    row = lambda b, c: (b * nchunk + c, 0)
    pos = lambda b, c: (c, 0)
    fix2 = lambda b, c: (0, 0)
    fix3 = lambda b, c: (0, 0, 0)
    return pl.pallas_call(
        _retention_kernel, grid=(nseq, nchunk),
        in_specs=[pl.BlockSpec((chunk, 4 * B_QK_PAD), row),
                  pl.BlockSpec((chunk, B_QK_PAD), pos), pl.BlockSpec((chunk, B_QK_PAD), pos),
                  pl.BlockSpec((chunk, B_V_WIDTH), row), pl.BlockSpec((chunk, B_V_WIDTH), row),
                  pl.BlockSpec(decay.shape, fix3), pl.BlockSpec(w_in.shape, fix2), pl.BlockSpec(w_end.shape, fix2),
                  pl.BlockSpec(g_chunk.shape, fix3), pl.BlockSpec((1, B_V_WIDTH), fix2),
                  pl.BlockSpec((1,) + s0.shape[1:], lambda b, c: (b, 0, 0, 0))],
        out_specs=[pl.BlockSpec((chunk, B_V_WIDTH), row),
                   pl.BlockSpec((1,) + s0.shape[1:], lambda b, c: (b, 0, 0, 0))],
        out_shape=[jax.ShapeDtypeStruct((rows, B_V_WIDTH), F32), jax.ShapeDtypeStruct(s0.shape, F32)],
        compiler_params=_cparams("parallel", "arbitrary"), name="retention")(
            bqk, cosv, sinv, bv, bg, decay, w_in, w_end, g_chunk, gain.reshape(1, B_V_WIDTH), s0)


def _retention_tables(chunk, pad):
    lg = np.log(1.0 - 2.0 ** (-5.0 - np.arange(B_HEADS, dtype=np.float32))).astype(np.float32)
    lg = jnp.asarray(lg)
    i = jnp.arange(pad, dtype=F32)
    diff = i[:, None] - i[None, :]
    decay = jnp.where(diff >= 0, jnp.exp(jnp.maximum(diff, 0.0)[None] * lg[:, None, None]), 0.0)
    w_end = jnp.exp((chunk - 1.0 - i)[None, :] * lg[:, None])
    w_in = jnp.exp((i + 1.0)[None, :] * lg[:, None])
    g_chunk = jnp.exp(chunk * lg)

    def widen(t):
        t = jnp.repeat(t.T, B_KEY_DIM, axis=1)
        return jnp.pad(t, ((0, 0), (0, B_QK_PAD - B_QK_WIDTH)))

    return decay, widen(w_in), widen(w_end), jnp.broadcast_to(g_chunk[:, None, None], (B_HEADS, 1, B_VAL_DIM))


def _rope_tables(pos):
    half = B_KEY_DIM // 2
    freq = ROPE_BASE ** (-jnp.arange(half, dtype=F32) / half)
    ang = pos.astype(F32)[:, None] * freq[None, :]
    cos, sin = jnp.cos(ang), jnp.sin(ang)
    cos_h = jnp.concatenate([cos, cos], axis=1)
    sin_h = jnp.concatenate([-sin, sin], axis=1)
    padw = ((0, 0), (0, B_QK_PAD - B_QK_WIDTH))
    return jnp.pad(jnp.tile(cos_h, (1, B_HEADS)), padw), jnp.pad(jnp.tile(sin_h, (1, B_HEADS)), padw)


def _ab_out_kernel(h_ref, o0_ref, o1_ref, o2_ref, l0_ref, l1_ref, l2_ref, b_ref, w_ref, out_ref):
    ls = [l0_ref[...], l1_ref[...], l2_ref[...]]
    os_ = [o0_ref[...], o1_ref[...], o2_ref[...]]
    mx = jnp.maximum(jnp.maximum(ls[0], ls[1]), ls[2])
    es = [jnp.exp(l - mx) for l in ls]
    den = es[0] + es[1] + es[2]
    acc = h_ref[...]
    for g in range(3):
        acc = acc + _dot((os_[g] * (es[g] / den)).astype(BF16), w_ref[g * LANES:(g + 1) * LANES, :])
    acc = acc + _dot(b_ref[...].astype(BF16), w_ref[A_WIDTH:, :])
    out_ref[...] = acc


def ab_out(h, os_, ls, b_out, w, tm):
    t, d = h.shape
    row = lambda i: (i, 0)
    small = pl.BlockSpec((tm, LANES), row)
    return pl.pallas_call(
        _ab_out_kernel, grid=(t // tm,),
        in_specs=[pl.BlockSpec((tm, d), row)] + [small] * 6 +
                 [pl.BlockSpec((tm, B_V_WIDTH), row), pl.BlockSpec(w.shape, lambda i: (0, 0))],
        out_specs=pl.BlockSpec((tm, d), row),
        out_shape=jax.ShapeDtypeStruct((t, d), F32),
        compiler_params=_cparams("parallel"), name="ab_out")(h, *os_, *ls, b_out, w)


def _cd_out_kernel(h_ref, c_ref, d_ref, w_ref, out_ref):
    acc = h_ref[...] + _dot(c_ref[...].astype(BF16), w_ref[:C_WIDTH, :])
    out_ref[...] = acc + _dot(d_ref[...].astype(BF16), w_ref[C_WIDTH:, :])


def cd_out(h, c_out, d_out, w, tm):
    t, d = h.shape
    row = lambda i: (i, 0)
    return pl.pallas_call(
        _cd_out_kernel, grid=(t // tm,),
        in_specs=[pl.BlockSpec((tm, d), row), pl.BlockSpec((tm, C_WIDTH), row), pl.BlockSpec((tm, D_WIDTH), row),
                  pl.BlockSpec(w.shape, lambda i: (0, 0))],
        out_specs=pl.BlockSpec((tm, d), row),
        out_shape=jax.ShapeDtypeStruct((t, d), F32),
        compiler_params=_cparams("parallel"), name="cd_out")(h, c_out, d_out, w)


def _route(logits):
    lane = lax.broadcasted_iota(I32, logits.shape, 1).astype(F32)
    gmask = lane < MOE_GROUPS
    gl = jnp.where(gmask, logits, NEG_INF)
    gmax = jnp.max(gl, axis=1, keepdims=True)
    gsum = jnp.sum(jnp.where(gmask, jnp.exp(gl - gmax), 0.0), axis=1, keepdims=True)
    g_w = 1.0 / gsum
    g_idx = jnp.min(jnp.where(gl == gmax, lane, LANES), axis=1, keepdims=True)
    lo = MOE_GROUPS + MOE_GROUP_EXPERTS * g_idx
    emask = (lane >= lo) & (lane < lo + MOE_GROUP_EXPERTS)
    el = jnp.where(emask, logits, NEG_INF)
    emax = jnp.max(el, axis=1, keepdims=True)
    eexp = jnp.where(emask, jnp.exp(el - emax), 0.0)
    p = eexp / jnp.sum(eexp, axis=1, keepdims=True)
    pm = jnp.where(emask, p, -1.0)
    p1 = jnp.max(pm, axis=1, keepdims=True)
    i1 = jnp.min(jnp.where(pm == p1, lane, LANES), axis=1, keepdims=True)
    pm2 = jnp.where(lane == i1, -1.0, pm)
    p2 = jnp.max(pm2, axis=1, keepdims=True)
    i2 = jnp.min(jnp.where(pm2 == p2, lane, LANES), axis=1, keepdims=True)
    den = p1 + p2
    return jnp.where(lane == i1, g_w * (p1 / den), jnp.where(lane == i2, g_w * (p2 / den), 0.0))


def _moe_kernel(h_ref, g_ref, wr_ref, br_ref, wg_ref, wu_ref, wd_ref, out_ref, xn_s, comb_s, acc_s):
    e = pl.program_id(1)

    @pl.when(e == 0)
    def _():
        xn = _rms(h_ref[...], g_ref[...]).astype(BF16)
        xn_s[...] = xn
        comb_s[...] = _route(_dot(xn, wr_ref[...]) + br_ref[...])
        acc_s[...] = jnp.zeros_like(acc_s)

    xb = xn_s[...]
    hg = _dot(xb, wg_ref[0])
    hu = _dot(xb, wu_ref[0])
    hdn = (hg * _sigmoid(hg)) * hu
    y = _dot(hdn.astype(BF16), wd_ref[0])
    comb = comb_s[...]
    lane = lax.broadcasted_iota(I32, comb.shape, 1)
    cw = jnp.sum(jnp.where(lane == MOE_GROUPS + e, comb, 0.0), axis=1, keepdims=True)
    acc_s[...] += cw * y

    @pl.when(e == MOE_EXPERTS - 1)
    def _():
        out_ref[...] = h_ref[...] + acc_s[...]


def moe(h, gain, wr, br, wg, wu, wd, tm):
    t, d = h.shape
    row = lambda i, e: (i, 0)
    fix = lambda i, e: (0, 0)
    ex = lambda i, e: (e, 0, 0)
    return pl.pallas_call(
        _moe_kernel, grid=(t // tm, MOE_EXPERTS),
        in_specs=[pl.BlockSpec((tm, d), row), pl.BlockSpec((1, d), fix),
                  pl.BlockSpec((d, LANES), fix), pl.BlockSpec((1, LANES), fix),
                  pl.BlockSpec((1, d, D_EXPERT), ex), pl.BlockSpec((1, d, D_EXPERT), ex),
                  pl.BlockSpec((1, D_EXPERT, d), ex)],
        out_specs=pl.BlockSpec((tm, d), row),
        out_shape=jax.ShapeDtypeStruct((t, d), F32),
        scratch_shapes=[pltpu.VMEM((tm, d), BF16), pltpu.VMEM((tm, LANES), F32), pltpu.VMEM((tm, d), F32)],
        compiler_params=_cparams("parallel", "arbitrary"), name="moe")(
            h, gain.reshape(1, d), wr, br, wg, wu, wd)


def _ple_kernel(h_ref, p_ref, g_ref, wg_ref, wp_ref, *rest, final):
    h = h_ref[...]
    gate = _sigmoid(_dot(_rms(h, g_ref[...]).astype(BF16), wg_ref[...]))
    hn = h + gate * _dot(p_ref[...].astype(BF16), wp_ref[...])
    if final:
        gf_ref, out_ref = rest
        out_ref[...] = _rms(hn, gf_ref[...])
    else:
        rest[0][...] = hn


def ple(h, p, gain, wg, wp, gain_final, tm):
    t, d = h.shape
    row = lambda i: (i, 0)
    fix = lambda i: (0, 0)
    final = gain_final is not None
    in_specs = [pl.BlockSpec((tm, d), row), pl.BlockSpec((tm, p.shape[1]), row), pl.BlockSpec((1, d), fix),
                pl.BlockSpec(wg.shape, fix), pl.BlockSpec(wp.shape, fix)]
    args = [h, p, gain.reshape(1, d), wg, wp]
    if final:
        in_specs.append(pl.BlockSpec((1, d), fix))
        args.append(gain_final.reshape(1, d))
    return pl.pallas_call(
        functools.partial(_ple_kernel, final=final), grid=(t // tm,),
        in_specs=in_specs, out_specs=pl.BlockSpec((tm, d), row),
        out_shape=jax.ShapeDtypeStruct((t, d), F32),
        compiler_params=_cparams("parallel"), name="ple")(*args)


def _s5_kernel(u_ref, a_ref, bre_ref, bim_ref, cre_ref, cim_ref, d_ref, wglu_ref, bglu_ref, x0r_ref, x0i_ref,
               out_ref, xr_ref, xi_ref, bur_s, bui_s, *, nbatch, steps):
    c = pl.program_id(0)

    @pl.when(c == 0)
    def _():
        xr_ref[...] = x0r_ref[...]
        xi_ref[...] = x0i_ref[...]

    u = u_ref[...]
    ub = u.astype(BF16)
    nsb = bre_ref.shape[0]
    wi, ws = C_WIDTH // nsb, C_FLAT // nsb
    for j in range(nsb):
        ubj = ub[:, j * wi:(j + 1) * wi]
        bur_s[:, j * ws:(j + 1) * ws] = _dot(ubj, bre_ref[j])
        bui_s[:, j * ws:(j + 1) * ws] = _dot(ubj, bim_ref[j])
    ar = a_ref[0:1, :]
    ai = a_ref[1:2, :]

    def step(t, carry):
        xr, xi = carry
        r0 = pl.multiple_of(t * nbatch, nbatch)
        nr = ar * xr - ai * xi + bur_s[pl.ds(r0, nbatch), :]
        ni = ar * xi + ai * xr + bui_s[pl.ds(r0, nbatch), :]
        bur_s[pl.ds(r0, nbatch), :] = nr
        bui_s[pl.ds(r0, nbatch), :] = ni
        return nr, ni

    xr, xi = lax.fori_loop(0, steps, step, (xr_ref[...], xi_ref[...]), unroll=min(steps, 8))
    xr_ref[...] = xr
    xi_ref[...] = xi
    y = jnp.concatenate(
        [_dot(bur_s[:, j * ws:(j + 1) * ws].astype(BF16), cre_ref[j]) -
         _dot(bui_s[:, j * ws:(j + 1) * ws].astype(BF16), cim_ref[j]) for j in range(nsb)], axis=1) + d_ref[...] * u
    z = 0.5 * y * (1.0 + jnp.tanh(math.sqrt(2.0 / math.pi) * (y + 0.044715 * (y * y * y))))
    out_ref[...] = z * _sigmoid(_dot(z.astype(BF16), wglu_ref[...]) + bglu_ref[...])


def s5(u, prm, x0r, x0i, nbatch, steps):
    rows = u.shape[0]
    a, bre, bim, cre, cim, dskip, wglu, bglu = prm
    nchunk = rows // (nbatch * steps)
    tr = nbatch * steps
    full = lambda arr: pl.BlockSpec(arr.shape, lambda c: (0,) * arr.ndim)
    return pl.pallas_call(
        functools.partial(_s5_kernel, nbatch=nbatch, steps=steps), grid=(nchunk,),
        in_specs=[pl.BlockSpec((tr, C_WIDTH), lambda c: (c, 0))] +
                 [full(t) for t in (a, bre, bim, cre, cim, dskip, wglu, bglu, x0r, x0i)],
        out_specs=[pl.BlockSpec((tr, C_WIDTH), lambda c: (c, 0)), full(x0r), full(x0i)],
        out_shape=[jax.ShapeDtypeStruct((rows, C_WIDTH), F32), jax.ShapeDtypeStruct(x0r.shape, F32),
                   jax.ShapeDtypeStruct(x0i.shape, F32)],
        scratch_shapes=[pltpu.VMEM((tr, C_FLAT), F32), pltpu.VMEM((tr, C_FLAT), F32)],
        compiler_params=_cparams("arbitrary"), name="s5")(u, a, bre, bim, cre, cim, dskip, wglu, bglu, x0r, x0i)


def _s5_prep(lam_re, lam_im, log_dt, b_re, b_im, c_re, c_im, d_skip, w_glu, b_glu):
    dt = jnp.exp(log_dt)[:, None]
    mag = jnp.exp(lam_re * dt)
    ab_re, ab_im = mag * jnp.cos(lam_im * dt), mag * jnp.sin(lam_im * dt)
    den = lam_re * lam_re + lam_im * lam_im
    zr = ((ab_re - 1.0) * lam_re + ab_im * lam_im) / den
    zi = (ab_im * lam_re - (ab_re - 1.0) * lam_im) / den
    bb_re = zr[..., None] * b_re - zi[..., None] * b_im
    bb_im = zr[..., None] * b_im + zi[..., None] * b_re
    eye = jnp.eye(C_GROUPS, dtype=F32)
    nsb = C_GROUPS // C_SUPER
    wi, ws = C_WIDTH // nsb, C_FLAT // nsb

    def diag_blocks(m, rows, cols):
        return jnp.stack([m[j * rows:(j + 1) * rows, j * cols:(j + 1) * cols] for j in range(nsb)], axis=0)

    b_dense = lambda bb: diag_blocks(jnp.einsum('gnc,gh->gchn', bb, eye).reshape(C_WIDTH, C_FLAT), wi, ws).astype(BF16)
    c_dense = lambda cc: diag_blocks(jnp.einsum('gcn,gh->gnhc', cc, eye).reshape(C_FLAT, C_WIDTH), ws, wi).astype(BF16)
    a = jnp.stack([ab_re.reshape(C_FLAT), ab_im.reshape(C_FLAT)], axis=0)
    return (a, b_dense(bb_re), b_dense(bb_im), c_dense(c_re), c_dense(c_im), d_skip.reshape(1, C_WIDTH),
            w_glu.astype(BF16), b_glu.reshape(1, C_WIDTH))


def _sortable(x):
    bits = lax.bitcast_convert_type(x, I32)
    return bits ^ ((bits >> 31) & jnp.int32(0x7FFFFFFF))


def _kth_largest(keys_ref, ntiles, tw, n_sel):
    total_rows = keys_ref.shape[0]
    rows = min(total_rows, KTH_ROWS)
    thrs, rems = [], []
    for r0 in range(0, total_rows, rows):
        def count_ge(cand, r0=r0):
            def body(t, cv):
                off = pl.multiple_of(t * tw, LANES)
                kk = keys_ref[r0:r0 + rows, pl.ds(off, tw)]
                for j in range(tw // LANES):
                    cv = cv + jnp.where(kk[:, j * LANES:(j + 1) * LANES] >= cand, 1.0, 0.0)
                return cv
            cv = lax.fori_loop(0, ntiles, body, jnp.zeros((rows, LANES), F32))
            return jnp.sum(cv, axis=1, keepdims=True)

        def bit(i, thr, count_ge=count_ge):
            cand = thr + lax.shift_left(jnp.int32(1), 31 - i)
            return jnp.where(count_ge(cand) >= n_sel, cand, thr)

        thr = lax.fori_loop(0, 32, bit, jnp.full((rows, 1), INT_MIN, I32))
        thrs.append(thr)
        rems.append(jnp.where(thr == INT_MIN, 0.0, n_sel - count_ge(thr + 1)))
    if len(thrs) == 1:
        return thrs[0], rems[0]
    return jnp.concatenate(thrs, axis=0), jnp.concatenate(rems, axis=0)


def _select(key, thr, rem, cnt, tri):
    eq = key == thr
    eqb = jnp.where(eq, 1.0, 0.0).astype(BF16)
    w = tri.shape[0]
    ranks = []
    for c in range(key.shape[1] // w):
        pc = _dot(eqb[:, c * w:(c + 1) * w], tri)
        ranks.append(cnt + pc)
        cnt = cnt + pc[:, -1:]
    rank = ranks[0] if len(ranks) == 1 else jnp.concatenate(ranks, axis=1)
    sel = (key > thr) | (eq & (rank <= rem))
    return sel, cnt


def _dsa_bias_kernel(tbr_ref, out_ref):
    tq, tk = DSA_BQ, DSA_BK
    start = pl.multiple_of(DSA_DCAP - pl.program_id(0) * LANES, LANES)
    wtab = tbr_ref[:, pl.ds(start, tk + 2 * LANES)]
    for h in range(D_HEADS):
        wb = jnp.broadcast_to(wtab[h:h + 1, :], (tq, tk + 2 * LANES))
        out_ref[0, h] = pltpu.roll(wb, 1, 1, stride=1, stride_axis=0)[:, LANES:LANES + tk]


def dsa_bias_tiles(tbr):
    nd = DSA_DCAP // LANES + 1
    return pl.pallas_call(
        _dsa_bias_kernel, grid=(nd,),
        in_specs=[pl.BlockSpec(tbr.shape, lambda d: (0, 0))],
        out_specs=pl.BlockSpec((1, D_HEADS, DSA_BQ, DSA_BK), lambda d: (d, 0, 0, 0)),
        out_shape=jax.ShapeDtypeStruct((nd, D_HEADS, DSA_BQ, DSA_BK), F32),
        compiler_params=_cparams("parallel"), name="dsa_bias_tiles")(tbr)


def _dsa_prompt_kernel(qb_ref, ktile_ref, q_ref, iq_ref, misc_ref, ikt_ref, kt_ref, vt_ref, *rest, n_sel):
    tq, tk = DSA_TQ, DSA_TK
    nqa, nkb = tq // DSA_BQ, tk // DSA_BK
    bias_refs = rest[:nqa * nkb]
    tri_ref, o_ref, keys_s, thr_s, rem_s, cnt_s, neg_s, m_s, l_s, acc_s, s_s, p_s, alpha_s = rest[nqa * nkb:]
    qb = qb_ref[pl.program_id(0)]
    kt = ktile_ref[pl.program_id(0)]
    kl = (qb * tq + tq - 1) // tk

    @pl.when(kt == 0)
    def _():
        iq = iq_ref[...]
        iw = misc_ref[:, IDX_DIM:IDX_DIM + IDX_HEADS]
        qpos = qb * tq + lax.broadcasted_iota(I32, (tq, tk), 0)

        def scores(t, carry):
            off = pl.multiple_of(t * tk, tk)
            ikt = ikt_ref[:, pl.ds(off, tk)]
            sc = jnp.zeros((tq, tk), F32)
            for h in range(IDX_HEADS):
                s = _dot(iq[:, h * IDX_DIM:(h + 1) * IDX_DIM], ikt)
                sc = sc + jnp.maximum(s, 0.0) * iw[:, h:h + 1]
            kpos = off + lax.broadcasted_iota(I32, (tq, tk), 1)
            keys_s[:, pl.ds(off, tk)] = jnp.where(kpos <= qpos, _sortable(sc * IDX_SCALE), INT_MIN)
            return carry

        lax.fori_loop(0, kl + 1, scores, 0)
        thr, rem = _kth_largest(keys_s, kl + 1, tk, n_sel)
        thr_s[...] = thr
        rem_s[...] = rem
        cnt_s[...] = jnp.zeros_like(cnt_s)
        m_s[...] = jnp.full_like(m_s, NEG_INF)
        l_s[...] = jnp.zeros_like(l_s)
        acc_s[...] = jnp.zeros_like(acc_s)

    def attend():
        off = pl.multiple_of(kt * tk, tk)
        sel, cnt = _select(keys_s[:, pl.ds(off, tk)], thr_s[...], rem_s[...], cnt_s[...], tri_ref[...])
        cnt_s[...] = cnt
        neg_s[...] = jnp.where(sel, 0.0, NEG_INF)
        for h in range(D_HEADS):
            sb, pb = s_s.at[h % 2], p_s.at[h % 2]
            sb[...] = _dot(q_ref[h], kt_ref[h * HEAD_DIM:(h + 1) * HEAD_DIM, :])
            for r in range(0, tq, DSA_RC):
                rs = slice(r, r + DSA_RC)
                a, ro = r // DSA_BQ, r % DSA_BQ
                bias = jnp.concatenate([bias_refs[a * nkb + b][0, h, ro:ro + DSA_RC, :] for b in range(nkb)], axis=1)
                s = sb[rs, :] * (LOG2E * HEAD_DIM ** -0.5) + bias + neg_s[rs, :]
                m_old = m_s[h, rs]
                m_new = jnp.maximum(m_old, jnp.max(s, axis=1, keepdims=True))
                m_safe = jnp.where(m_new == NEG_INF, 0.0, m_new)
                p = jnp.exp2(s - m_safe)
                alpha = jnp.exp2(m_old - m_safe)
                l_s[h, rs] = alpha * l_s[h, rs] + jnp.sum(p, axis=1, keepdims=True)
                alpha_s[h % 2, rs] = alpha
                pb[rs, :] = p.astype(BF16)
                m_s[h, rs] = m_new
            acc_s[h] = alpha_s[h % 2] * acc_s[h] + _dot_nt(pb[...], vt_ref[h * HEAD_DIM:(h + 1) * HEAD_DIM, :])

    attend()

    @pl.when(kt == kl)
    def _():
        o_ref[...] = jnp.concatenate([acc_s[h] / l_s[h] for h in range(D_HEADS)], axis=1)


def dsa_prompt(q, iq, misc, ikt, kt_, v, bias_tiles, tri, n_sel):
    seq = q.shape[1]
    tq, tk = DSA_TQ, DSA_TK
    nb = seq // tq
    pairs = [(b, t) for b in range(nb) for t in range((b * tq + tq - 1) // tk + 1)]
    qb_tab = jnp.asarray(np.array([p[0] for p in pairs], np.int32))
    kt_tab = jnp.asarray(np.array([p[1] for p in pairs], np.int32))
    qrow = lambda i, qbt, ktt: (qbt[i], 0)
    fix = lambda i, qbt, ktt: (0, 0)
    ktile = lambda i, qbt, ktt: (0, ktt[i])

    def bias_spec(a, b):
        def idx(i, qbt, ktt):
            d = (qbt[i] * tq + a * DSA_BQ - ktt[i] * tk - b * DSA_BK) // LANES
            return (jnp.clip(d, 0, DSA_DCAP // LANES), 0, 0, 0)
        return pl.BlockSpec((1, D_HEADS, DSA_BQ, DSA_BK), idx)

    bias_specs = [bias_spec(a, b) for a in range(tq // DSA_BQ) for b in range(tk // DSA_BK)]
    grid_spec = pltpu.PrefetchScalarGridSpec(
        num_scalar_prefetch=2, grid=(len(pairs),),
        in_specs=[pl.BlockSpec((D_HEADS, tq, HEAD_DIM), lambda i, qbt, ktt: (0, qbt[i], 0)),
                  pl.BlockSpec((tq, IDX_HEADS * IDX_DIM), qrow),
                  pl.BlockSpec((tq, LANES), qrow), pl.BlockSpec(ikt.shape, fix),
                  pl.BlockSpec((D_WIDTH, tk), ktile), pl.BlockSpec((D_WIDTH, tk), ktile)] +
                 bias_specs + [pl.BlockSpec(tri.shape, fix)],
        out_specs=pl.BlockSpec((tq, D_WIDTH), qrow),
        scratch_shapes=[pltpu.VMEM((tq, seq), I32), pltpu.VMEM((tq, 1), I32), pltpu.VMEM((tq, 1), F32),
                        pltpu.VMEM((tq, 1), F32), pltpu.VMEM((tq, tk), F32), pltpu.VMEM((D_HEADS, tq, 1), F32),
                        pltpu.VMEM((D_HEADS, tq, 1), F32), pltpu.VMEM((D_HEADS, tq, HEAD_DIM), F32),
                        pltpu.VMEM((2, tq, tk), F32), pltpu.VMEM((2, tq, tk), BF16), pltpu.VMEM((2, tq, 1), F32)])
    return pl.pallas_call(
        functools.partial(_dsa_prompt_kernel, n_sel=n_sel), grid_spec=grid_spec,
        out_shape=jax.ShapeDtypeStruct((seq, D_WIDTH), F32),
        compiler_params=_cparams("arbitrary"), name="dsa_prompt")(
            qb_tab, kt_tab, q, iq, misc, ikt, kt_, v, *([bias_tiles] * len(bias_specs)), tri)


def _dsa_step_keys_kernel(pt_ref, iq_ref, iw_ref, *rest, n_new, pg):
    page_refs, new_ref, keys_ref = rest[:pg], rest[pg], rest[pg + 1]
    p = pl.program_id(1)
    npage = pl.num_programs(1) * pg
    psz = new_ref.shape[2]
    rows = keys_ref.shape[1]

    def score(kt):
        r = jnp.maximum(_dot(iq_ref[0], kt.astype(BF16)), 0.0) * iw_ref[0][:, 0:1]
        sc = r[0:rows]
        for h in range(1, IDX_HEADS):
            sc = sc + r[h * rows:(h + 1) * rows]
        return _sortable(sc * IDX_SCALE)

    kt = jnp.concatenate([page_ref[0, 0] for page_ref in page_refs], axis=1)
    keys_ref[0, :, pl.ds(pl.multiple_of(p * (pg * psz), psz), pg * psz)] = score(kt)

    @pl.when(p == 0)
    def _():
        s_i = lax.broadcasted_iota(I32, (rows, psz), 0)
        j_i = lax.broadcasted_iota(I32, (rows, psz), 1)
        ok = (j_i <= s_i) & (j_i < n_new)
        keys_ref[0, :, pl.ds(pl.multiple_of(npage * psz, psz), psz)] = jnp.where(ok, score(new_ref[0]), INT_MIN)


def dsa_step_keys(page_table, iqm, iwm, cache_kidx, ik_new, layer, n_new, pg):
    nseq, npage = page_table.shape
    psz = cache_kidx.shape[3]
    rows = iqm.shape[1] // IDX_HEADS
    ncol = (npage + 1) * psz
    seqb = lambda n, p, pt: (n, 0, 0)
    page_spec = lambda i: pl.BlockSpec((1, 1, IDX_DIM, psz),
                                       lambda n, p, pt: (layer, pt[n * npage + p * pg + i], 0, 0))
    grid_spec = pltpu.PrefetchScalarGridSpec(
        num_scalar_prefetch=1, grid=(nseq, npage // pg),
        in_specs=[pl.BlockSpec((1,) + iqm.shape[1:], seqb), pl.BlockSpec((1,) + iwm.shape[1:], seqb)] +
                 [page_spec(i) for i in range(pg)] + [pl.BlockSpec((1,) + ik_new.shape[1:], seqb)],
        out_specs=pl.BlockSpec((1, rows, ncol), seqb))
    return pl.pallas_call(
        functools.partial(_dsa_step_keys_kernel, n_new=n_new, pg=pg), grid_spec=grid_spec,
        out_shape=jax.ShapeDtypeStruct((nseq, rows, ncol), I32),
        compiler_params=_cparams("parallel", "arbitrary"), name="dsa_step_keys")(
            page_table.reshape(-1), iqm, iwm, *([cache_kidx] * pg), ik_new)


def _topk_mask_kernel(keys_ref, tri_ref, neg_ref, *, tw, n_sel):
    thr, rem = _kth_largest(keys_ref, keys_ref.shape[1] // tw, tw, n_sel)
    w = tri_ref.shape[0]

    def chunk(c, cnt):
        off = pl.multiple_of(c * w, w)
        sel, cnt = _select(keys_ref[:, pl.ds(off, w)], thr, rem, cnt, tri_ref[...])
        neg_ref[:, pl.ds(off, w)] = jnp.where(sel, 0.0, NEG_INF)
        return cnt

    lax.fori_loop(0, keys_ref.shape[1] // w, chunk, jnp.zeros(thr.shape, F32))


def topk_mask_rows(keys, tri, tw, n_sel, tr):
    rows, ncol = keys.shape
    return pl.pallas_call(
        functools.partial(_topk_mask_kernel, tw=tw, n_sel=n_sel), grid=(rows // tr,),
        in_specs=[pl.BlockSpec((tr, ncol), lambda i: (i, 0)), pl.BlockSpec(tri.shape, lambda i: (0, 0))],
        out_specs=pl.BlockSpec((tr, ncol), lambda i: (i, 0)),
        out_shape=jax.ShapeDtypeStruct((rows, ncol), F32),
        compiler_params=_cparams("parallel"), name="topk_mask_rows")(keys, tri)


def _dsa_step_attn_kernel(pt_ref, q_ref, neg_ref, neg_new_ref, bias_ref, bias_new_ref, *rest, pg):
    k_refs, v_refs = rest[:pg], rest[pg:2 * pg]
    kn_ref, vn_ref, o_ref, m_s, l_s, acc_s = rest[2 * pg:]
    p = pl.program_id(1)
    nstep = pl.num_programs(1) - 1
    rows = neg_ref.shape[1]

    @pl.when(p == 0)
    def _():
        m_s[...] = jnp.full_like(m_s, NEG_INF)
        l_s[...] = jnp.zeros_like(l_s)
        acc_s[...] = jnp.zeros_like(acc_s)

    def process(kt, vt, neg, bias):
        s = _dot(q_ref[0], kt.astype(BF16)) * (LOG2E * HEAD_DIM ** -0.5) + bias
        s = s + jnp.concatenate([neg] * D_HEADS, axis=0)
        m_old = m_s[...]
        m_new = jnp.maximum(m_old, jnp.max(s, axis=1, keepdims=True))
        m_safe = jnp.where(m_new == NEG_INF, 0.0, m_new)
        pr = jnp.exp2(s - m_safe)
        alpha = jnp.exp2(m_old - m_safe)
        l_s[...] = alpha * l_s[...] + jnp.sum(pr, axis=1, keepdims=True)
        acc_s[...] = alpha * acc_s[...] + _dot_nt(pr.astype(BF16), vt.astype(BF16))
        m_s[...] = m_new

    @pl.when(p < nstep)
    def _():
        process(jnp.concatenate([r[0, 0, 0] for r in k_refs], axis=1),
                jnp.concatenate([r[0, 0, 0] for r in v_refs], axis=1), neg_ref[0], bias_ref[...])

    @pl.when(p == nstep)
    def _():
        process(kn_ref[0, 0], vn_ref[0, 0], neg_new_ref[0], bias_new_ref[...])
        accn = acc_s[...] / l_s[...]
        head = lax.broadcasted_iota(I32, (rows, D_WIDTH), 1) // HEAD_DIM
        out = jnp.zeros((rows, D_WIDTH), F32)
        for h in range(D_HEADS):
            out = out + jnp.where(head == h, accn[h * rows:(h + 1) * rows], 0.0)
        o_ref[0] = out


def dsa_step_attn(page_table, qbd, neg, bias, cache_kvt, kvt_new, layer, pg):
    nseq, npage = page_table.shape
    psz = cache_kvt.shape[4]
    rows = neg.shape[1]
    qr = qbd.shape[1]
    nstep = npage // pg
    seqb = lambda n, p, pt: (n, 0, 0)
    stp = lambda p: jnp.minimum(p, nstep - 1)

    def page_spec(i, part):
        return pl.BlockSpec((1, 1, 1, D_WIDTH, psz),
                            lambda n, p, pt: (layer, pt[n * npage + stp(p) * pg + i], part, 0, 0))

    def new_spec(part):
        return pl.BlockSpec((1, 1, D_WIDTH, psz), lambda n, p, pt: (n, part, 0, 0))

    grid_spec = pltpu.PrefetchScalarGridSpec(
        num_scalar_prefetch=1, grid=(nseq, nstep + 1),
        in_specs=[pl.BlockSpec((1, qr, D_WIDTH), seqb),
                  pl.BlockSpec((1, rows, pg * psz), lambda n, p, pt: (n, 0, stp(p))),
                  pl.BlockSpec((1, rows, psz), lambda n, p, pt: (n, 0, npage)),
                  pl.BlockSpec((qr, pg * psz), lambda n, p, pt: (0, stp(p))),
                  pl.BlockSpec((qr, psz), lambda n, p, pt: (0, npage))] +
                 [page_spec(i, 0) for i in range(pg)] + [page_spec(i, 1) for i in range(pg)] +
                 [new_spec(0), new_spec(1)],
        out_specs=pl.BlockSpec((1, rows, D_WIDTH), seqb),
        scratch_shapes=[pltpu.VMEM((qr, 1), F32), pltpu.VMEM((qr, 1), F32), pltpu.VMEM((qr, D_WIDTH), F32)])
    return pl.pallas_call(
        functools.partial(_dsa_step_attn_kernel, pg=pg), grid_spec=grid_spec,
        out_shape=jax.ShapeDtypeStruct((nseq, rows, D_WIDTH), F32),
        compiler_params=_cparams("parallel", "arbitrary"), name="dsa_step_attn")(
            page_table.reshape(-1), qbd, neg, neg, bias, bias, *([cache_kvt] * (2 * pg)), kvt_new, kvt_new)


def _prep_ab_w(w):
    d = w.shape[0]
    aq, ak, av, bq, bk, bv, bg = _split_cols(w, [A_WIDTH] * 3 + [B_QK_WIDTH] * 2 + [B_V_WIDTH] * 2)
    half = B_KEY_DIM // 2

    def rot(m):
        return m.reshape(d, B_HEADS, 2, half)[:, :, ::-1, :].reshape(d, B_QK_WIDTH)

    pad = jnp.zeros((d, B_QK_PAD - B_QK_WIDTH), w.dtype)
    cols = [aq, ak, av, bq, pad, rot(bq), pad, bk, pad, rot(bk), pad, bv, bg]
    return jnp.concatenate(cols, axis=1).astype(BF16)


AB_SEGS = ((0, 3 * A_WIDTH, F32), (3 * A_WIDTH, 4 * B_QK_PAD, F32),
           (3 * A_WIDTH + 4 * B_QK_PAD, B_V_WIDTH, F32), (3 * A_WIDTH + 4 * B_QK_PAD + B_V_WIDTH, B_V_WIDTH, F32))


def _split_cols(x, widths):
    parts, start = [], 0
    for wd in widths:
        parts.append(x[..., start:start + wd])
        start += wd
    return parts


def _prep_cd_w(w):
    d = w.shape[0]
    cu, dq, dk, dv, iq, iw, ik = _split_cols(w, [C_WIDTH, D_WIDTH, D_WIDTH, D_WIDTH, IDX_HEADS * IDX_DIM, IDX_HEADS, IDX_DIM])
    pad = jnp.zeros((d, LANES - IDX_DIM - IDX_HEADS), w.dtype)
    return jnp.concatenate([cu, dq, dk, dv, iq, ik, iw, pad], axis=1).astype(BF16)


CD_SEGS = ((0, C_WIDTH, F32), (C_WIDTH, D_WIDTH, BF16), (C_WIDTH + D_WIDTH, 2 * D_WIDTH, F32),
           (C_WIDTH + 3 * D_WIDTH, IDX_HEADS * IDX_DIM, BF16), (C_WIDTH + 3 * D_WIDTH + IDX_HEADS * IDX_DIM, LANES, F32))


def _bias_by_distance(t5_table, dist):
    onehot = (_t5_bucket(dist)[..., None] == jnp.arange(T5_BUCKETS)).astype(F32)
    return jnp.einsum('...b,bh->...h', onehot, t5_table.astype(F32), precision=lax.Precision.HIGHEST)


def _dil_group_bias(t5_table, g, dil, nkey):
    return _bias_by_distance(t5_table, jnp.arange(nkey + 1) * dil)[:, g * A_GROUP_HEADS:(g + 1) * A_GROUP_HEADS]


def _dil_prompt_bias(t5_table, g, dil, nkey):
    bias = _dil_group_bias(t5_table, g, dil, nkey)
    ext = jnp.concatenate([jnp.broadcast_to(bias[:1], (nkey - 1, A_GROUP_HEADS)), bias,
                           jnp.broadcast_to(bias[-1:], (nkey - 1, A_GROUP_HEADS))], axis=0)[::-1]
    rows = [lax.slice_in_dim(ext, nkey - 1 - qi, nkey - 1 - qi + 2 * nkey, axis=0) for qi in range(nkey)]
    return jnp.moveaxis(jnp.stack(rows, axis=0), -1, 0)


def _dil_step_bias(t5_table, g, dil, nkey, n_buf, n_new, rows, pad_new):
    assert n_buf == nkey * dil
    rev = _dil_group_bias(t5_table, g, dil, nkey)[::-1]
    width = n_buf + pad_new
    out = []
    for s in range(rows):
        if s < n_new:
            blockc = jnp.full((nkey + 1, dil, A_GROUP_HEADS), NEG_INF, F32).at[:, s % dil, :].set(rev)
            flat = blockc.reshape((nkey + 1) * dil, A_GROUP_HEADS)
            front = s - s % dil
            row = jnp.pad(flat, ((front, width - front - flat.shape[0]), (0, 0)), constant_values=NEG_INF)
        else:
            row = jnp.full((width, A_GROUP_HEADS), NEG_INF, F32).at[0].set(0.0)
        out.append(row)
    return jnp.stack(out, axis=0).transpose(2, 0, 1).reshape(A_GROUP_HEADS * rows, width)


def _dsa_prompt_table(t5_table):
    d = (DSA_DCAP + LANES - 1) - jnp.arange(DSA_TAB)
    return _bias_by_distance(t5_table, d)[:, :D_HEADS].T * LOG2E


def _tri(n):
    r = np.arange(n)
    return jnp.asarray((r[:, None] <= r[None, :]).astype(np.float32)).astype(BF16)


def _ffn_and_ple(h, p, i, prm, tm, final):
    h = moe(h, prm['norm_ffn'][i], prm['moe_wr'][i], prm['moe_br'][i], prm['moe_wg'][i], prm['moe_wu'][i],
            prm['moe_wd'][i], min(4 * tm, h.shape[0]))
    return ple(h, p, prm['norm_ple'][i], prm['ple_wg'][i], prm['ple_wp'][i], prm['norm_final'] if final else None, tm)


def _layer_ab_prompt(h, prm, j, tm):
    seq = h.shape[0]
    aqkv, bqk, bv, bg = norm_matmul(h, prm['norm_mix'][2 * j], prm['ab_w'][j], AB_SEGS, tm)
    os_, ls, bufs = [], [], []
    for g, (win, dil) in enumerate(A_PATTERNS):
        nkey = win // dil
        assert seq % (dil * nkey) == 0
        o, lse = dil_prompt(aqkv, _dil_prompt_bias(prm['t5_table'], g, dil, nkey), g, dil)
        os_.append(o)
        ls.append(lse)
        kcols = aqkv[seq - min(win, seq):, A_WIDTH + g * LANES:A_WIDTH + (g + 1) * LANES]
        vcols = aqkv[seq - min(win, seq):, 2 * A_WIDTH + g * LANES:2 * A_WIDTH + (g + 1) * LANES]
        bufs.append(jnp.stack([kcols, vcols], axis=1).reshape(1, -1, 2, A_GROUP_HEADS, HEAD_DIM))
    chunk = B_CHUNK if seq % B_CHUNK == 0 else seq
    cosv, sinv = _rope_tables(jnp.arange(seq))
    s0 = jnp.zeros((1, B_HEADS, B_KEY_DIM, B_VAL_DIM), F32)
    b_out, s_new = retention(bqk, cosv, sinv, bv, bg, _retention_tables(chunk, chunk), prm['ab_gn_gain'][j], s0,
                             chunk, seq // chunk)
    h = ab_out(h, os_, ls, b_out, prm['ab_w_out'][j], tm)
    return h, bufs, s_new


def _layer_ab_sample(h, prm, j, past, nseq, n_new, pos0):
    rows = 8
    aqkv, bqk, bv, bg = norm_matmul(h, prm['norm_mix'][2 * j], prm['ab_w'][j], AB_SEGS, h.shape[0])
    a3 = aqkv.reshape(nseq, n_new, 3 * A_WIDTH)
    pad_rows = lambda x, r: jnp.pad(x, ((0, 0), (0, r - x.shape[1]), (0, 0)))
    os_, ls, bufs = [], [], []
    for g, (win, dil) in enumerate(A_PATTERNS):
        nkey = win // dil
        buf = past['a'][g]
        n_buf = buf.shape[2]
        cs = lambda part: slice(part * A_WIDTH + g * LANES, part * A_WIDTH + (g + 1) * LANES)
        q = pad_rows(a3[:, :, cs(0)], rows)
        kv_new = jnp.concatenate([a3[:, :, cs(1)], a3[:, :, cs(2)]], axis=2)
        bias = _dil_step_bias(prm['t5_table'], g, dil, nkey, n_buf, n_new, rows, LANES)
        buf_t = buf.transpose(0, 1, 3, 4, 5, 2).reshape(buf.shape[0], nseq, 2, LANES, n_buf)
        new_t = jnp.pad(kv_new.reshape(nseq, n_new, 2, LANES).transpose(0, 2, 3, 1),
                        ((0, 0), (0, 0), (0, 0), (0, LANES - n_new)))
        o, lse = dil_step(q, buf_t, new_t, bias, j)
        os_.append(o[:, :n_new].reshape(nseq * n_new, LANES))
        ls.append(lse[:, :n_new].reshape(nseq * n_new, LANES))
        kv_all = jnp.concatenate([buf[j].reshape(nseq, n_buf, 2 * LANES), kv_new], axis=1)
        bufs.append(kv_all[:, n_new:].reshape(nseq, n_buf, 2, A_GROUP_HEADS, HEAD_DIM))
    chunk = B_CHUNK if n_new % B_CHUNK == 0 else n_new
    assert chunk == n_new
    cpad = 16
    cosv, sinv = _rope_tables(pos0 + jnp.arange(cpad))
    padc = lambda x: jnp.pad(x.reshape(nseq, n_new, -1), ((0, 0), (0, cpad - n_new), (0, 0))).reshape(nseq * cpad, -1)
    b_out, s_new = retention(padc(bqk), cosv, sinv, padc(bv), padc(bg), _retention_tables(chunk, cpad),
                             prm['ab_gn_gain'][j], past['b'][j].astype(F32), cpad, 1)
    b_out = b_out.reshape(nseq, cpad, -1)[:, :n_new].reshape(nseq * n_new, -1)
    h = ab_out(h, os_, ls, b_out, prm['ab_w_out'][j], h.shape[0])
    return h, bufs, s_new


def _cd_project(h, prm, j, tm):
    return norm_matmul(h, prm['norm_mix'][2 * j + 1], prm['cd_w'][j], CD_SEGS, tm)


def _layer_cd_prompt(h, prm, j, tm):
    seq = h.shape[0]
    cu, dq, kv, iq, misc = _cd_project(h, prm, j, tm)
    x0 = jnp.zeros((1, C_FLAT), F32)
    c_out, xr, xi = s5(cu, prm['s5'][j], x0, x0, 1, min(256, seq))
    sc = jnp.stack([xr.reshape(C_GROUPS, C_STATE), xi.reshape(C_GROUPS, C_STATE)], axis=-1)[None]
    n_sel = min(IDX_TOPK, seq // 4)
    heads_first = lambda x: x.reshape(seq, D_HEADS, HEAD_DIM).transpose(1, 0, 2)
    kb = kv[:, :D_WIDTH].astype(BF16).T
    vb = kv[:, D_WIDTH:].astype(BF16).T
    ikt = misc[:, :IDX_DIM].astype(BF16).T
    bias_tiles = dsa_bias_tiles(_dsa_prompt_table(prm['t5_table']))
    d_out = dsa_prompt(heads_first(dq), iq, misc, ikt, kb, vb, bias_tiles, _tri(DSA_TRI), n_sel)
    h = cd_out(h, c_out, d_out, prm['cd_w_out'][j], tm)
    return h, sc, kv, misc[:, :IDX_DIM]


def _layer_cd_sample(h, prm, j, past, nseq, n_new):
    rows = 8
    cu, dq, kv, iq, misc = _cd_project(h, prm, j, h.shape[0])
    tmaj = lambda x: x.reshape(nseq, n_new, -1).transpose(1, 0, 2).reshape(nseq * n_new, -1)
    x0 = past['c'][j].astype(F32)
    c_out, xr, xi = s5(tmaj(cu), prm['s5'][j], x0[..., 0].reshape(nseq, C_FLAT), x0[..., 1].reshape(nseq, C_FLAT),
                       nseq, n_new)
    c_out = c_out.reshape(n_new, nseq, -1).transpose(1, 0, 2).reshape(nseq * n_new, -1)
    sc = jnp.stack([xr.reshape(nseq, C_GROUPS, C_STATE), xi.reshape(nseq, C_GROUPS, C_STATE)], axis=-1)

    page_table = past['page_table']
    cache_kv, cache_kidx = past['d_kv'], past['d_kidx']
    npage = page_table.shape[1]
    psz = cache_kv.shape[2]
    n_past = npage * psz
    n_sel = min(IDX_TOPK, (n_past + n_new) // 4)
    pad_rows = lambda x, r: jnp.pad(x, ((0, 0), (0, r - x.shape[1]), (0, 0)))
    ik_new = misc[:, :IDX_DIM]
    iqm = pad_rows(iq.reshape(nseq, n_new, IDX_HEADS * IDX_DIM), rows).reshape(nseq, rows, IDX_HEADS, IDX_DIM)
    iqm = iqm.transpose(0, 2, 1, 3).reshape(nseq, IDX_HEADS * rows, IDX_DIM)
    iwm = pad_rows(misc[:, IDX_DIM:IDX_DIM + IDX_HEADS].reshape(nseq, n_new, IDX_HEADS), rows)
    iwm = jnp.broadcast_to(iwm.transpose(0, 2, 1).reshape(nseq, IDX_HEADS * rows, 1), (nseq, IDX_HEADS * rows, LANES))
    kidx_t = cache_kidx.transpose(0, 1, 3, 2)
    kv_t = cache_kv.transpose(0, 1, 3, 4, 5, 2).reshape(cache_kv.shape[0], cache_kv.shape[1], 2, D_WIDTH, psz)
    lane_pad = lambda x: jnp.pad(x, ((0, 0),) * (x.ndim - 1) + ((0, psz - x.shape[-1]),))
    ik_new_t = lane_pad(ik_new.reshape(nseq, n_new, IDX_DIM).transpose(0, 2, 1))
    keys = dsa_step_keys(page_table, iqm, iwm, kidx_t, ik_new_t, j, n_new, math.gcd(npage, 32))
    ncol = keys.shape[2]
    neg = topk_mask_rows(keys.reshape(nseq * rows, ncol), _tri(LANES),
                         ncol // 3 if (ncol // LANES) % 3 == 0 else LANES, n_sel, min(LANES, nseq * rows))
    qpos = n_past + jnp.minimum(jnp.arange(rows), n_new - 1)
    dist = qpos[:, None] - jnp.arange(ncol)[None, :]
    bias = _bias_by_distance(prm['t5_table'], dist)[..., :D_HEADS] * LOG2E
    bias = bias.transpose(2, 0, 1).reshape(D_HEADS * rows, ncol)
    q = pad_rows(dq.reshape(nseq, n_new, D_WIDTH), rows)
    col_head = jnp.arange(D_WIDTH) // HEAD_DIM
    qbd = jnp.where(col_head[None, None, None, :] == jnp.arange(D_HEADS)[None, :, None, None], q[:, None], 0)
    qbd = qbd.reshape(nseq, D_HEADS * rows, D_WIDTH).astype(dq.dtype)
    kvt_new = lane_pad(kv.reshape(nseq, n_new, 2, D_WIDTH).transpose(0, 2, 3, 1))
    d_out = dsa_step_attn(page_table, qbd, neg.reshape(nseq, rows, ncol), bias, kv_t, kvt_new, j, math.gcd(npage, 16))
    d_out = d_out[:, :n_new].reshape(nseq * n_new, D_WIDTH)
    h = cd_out(h, c_out, d_out, prm['cd_w_out'][j], h.shape[0])
    return h, sc, kv, ik_new


def _trunk(x, p, prm, past, nseq, n_new, pos0, tm):
    depth = p.shape[0]
    h = x
    a_new = [[] for _ in A_PATTERNS]
    b_new, c_new, kv_new, kidx_new = [], [], [], []
    for i in range(depth):
        j = i // 2
        if i % 2 == 0:
            if past is None:
                h, bufs, sb = _layer_ab_prompt(h, prm, j, tm)
            else:
                h, bufs, sb = _layer_ab_sample(h, prm, j, past, nseq, n_new, pos0)
            for g in range(len(A_PATTERNS)):
                a_new[g].append(bufs[g])
            b_new.append(sb)
        else:
            if past is None:
                h, sc, kv, kidx = _layer_cd_prompt(h, prm, j, tm)
            else:
                h, sc, kv, kidx = _layer_cd_sample(h, prm, j, past, nseq, n_new)
            c_new.append(sc)
            kv_new.append(kv.reshape(nseq, n_new, 2, D_HEADS, HEAD_DIM))
            kidx_new.append(kidx.reshape(nseq, n_new, IDX_DIM))
        h = _ffn_and_ple(h, p[i], i, prm, tm, i == depth - 1)
    stack = lambda t: jnp.stack(t, axis=0)
    return h, tuple(stack(t) for t in a_new), stack(b_new), stack(c_new), stack(kv_new), stack(kidx_new)


def kernel(x_prompt, x_sample, p_prompt, p_sample, cache_a_kv0, cache_a_kv1, cache_a_kv2, state_b, state_c, cache_d_kv, cache_d_kidx, page_table, t5_table, norm_mix, norm_ffn, norm_ple, norm_final, ab_w_in, ab_w_out, ab_gn_gain, cd_w_in, cd_w_out, c_lambda_re, c_lambda_im, c_log_dt, c_b_re, c_b_im, c_c_re, c_c_im, c_d, c_w_glu, c_b_glu, moe_w_group, moe_b_group, moe_w_expert, moe_b_expert, moe_w_gate, moe_w_up, moe_w_down, ple_w_gate, ple_w_proj):
    depth, d_model = norm_mix.shape
    n_ab, n_cd = ab_w_in.shape[0], cd_w_in.shape[0]
    rpad = jnp.zeros((depth, d_model, LANES - MOE_GROUPS - MOE_EXPERTS), F32)
    prm = dict(
        t5_table=t5_table, norm_mix=norm_mix, norm_ffn=norm_ffn, norm_ple=norm_ple, norm_final=norm_final,
        ab_w=[_prep_ab_w(ab_w_in[j]) for j in range(n_ab)], ab_w_out=ab_w_out.astype(BF16), ab_gn_gain=ab_gn_gain,
        cd_w=[_prep_cd_w(cd_w_in[j]) for j in range(n_cd)], cd_w_out=cd_w_out.astype(BF16),
        s5=[_s5_prep(c_lambda_re[j], c_lambda_im[j], c_log_dt[j], c_b_re[j], c_b_im[j], c_c_re[j], c_c_im[j],
                     c_d[j], c_w_glu[j], c_b_glu[j]) for j in range(n_cd)],
        moe_wr=jnp.concatenate([moe_w_group, moe_w_expert.reshape(depth, d_model, MOE_EXPERTS), rpad],
                               axis=2).astype(BF16),
        moe_br=jnp.concatenate([moe_b_group, moe_b_expert.reshape(depth, MOE_EXPERTS),
                                jnp.zeros((depth, LANES - MOE_GROUPS - MOE_EXPERTS), F32)], axis=1)[:, None, :],
        moe_wg=moe_w_gate.astype(BF16), moe_wu=moe_w_up.astype(BF16), moe_wd=moe_w_down.astype(BF16),
        ple_wg=ple_w_gate.astype(BF16), ple_wp=ple_w_proj.astype(BF16))

    bsz, seq, _ = x_prompt.shape
    assert bsz == 1
    y_p, a_p, sb_p, sc_p, dkv_p, dki_p = _trunk(x_prompt[0], p_prompt[:, 0], prm, None, 1, seq, 0, min(256, seq))
    y_p = y_p[None]

    nseq, n_new, _ = x_sample.shape
    past = dict(a=(cache_a_kv0, cache_a_kv1, cache_a_kv2), b=state_b, c=state_c, d_kv=cache_d_kv, d_kidx=cache_d_kidx,
                page_table=page_table)
    n_past = page_table.shape[1] * cache_d_kv.shape[2]
    rows_s = nseq * n_new
    y_s, a_s, sb_s, sc_s, dkv_s, dki_s = _trunk(x_sample.reshape(rows_s, d_model),
                                                p_sample.reshape(depth, rows_s, -1), prm, past, nseq, n_new, n_past,
                                                rows_s)
    y_s = y_s.reshape(nseq, n_new, d_model)
    return (y_p, y_s, a_p[0], a_s[0], a_p[1], a_s[1], a_p[2], a_s[2], sb_p, sb_s, sc_p, sc_s, dkv_p, dkv_s, dki_p, dki_s)
```

```python
import functools
import math

import numpy as np
import jax
import jax.numpy as jnp
from jax import lax
from jax.experimental import pallas as pl
from jax.experimental.pallas import tpu as pltpu

F32 = jnp.float32
BF16 = jnp.bfloat16
I32 = jnp.int32

EPS = 1e-6
HEAD_DIM = 64
A_PATTERNS = ((128, 1), (512, 4), (2048, 16))
A_GROUP_HEADS = 2
A_WIDTH = 384
B_HEADS = 5
B_KEY_DIM = 64
B_VAL_DIM = 128
B_QK_WIDTH = 320
B_QK_PAD = 384
B_V_WIDTH = 640
B_CHUNK = 128
ROPE_BASE = 10000.0
C_GROUPS = 32
C_GROUP_SIZE = 16
C_WIDTH = 512
C_STATE = 64
C_FLAT = C_GROUPS * C_STATE
C_SUPER = 8
D_HEADS = 8
D_WIDTH = 512
IDX_HEADS = 4
IDX_DIM = 64
IDX_TOPK = 256
IDX_SCALE = (IDX_HEADS * IDX_DIM) ** -0.5
T5_BUCKETS = 32
T5_MAX_DIST = 2048
MOE_GROUPS = 4
MOE_GROUP_EXPERTS = 4
MOE_EXPERTS = 16
D_EXPERT = 512
LANES = 128
INT_MIN = -(2 ** 31)
NEG_INF = float("-inf")
LOG2E = math.log2(math.e)

VMEM_LIMIT_BYTES = 52 * 1024 * 1024

DSA_TQ = 256
DSA_TK = 1024
DSA_BQ = 128
DSA_BK = 512
DSA_DCAP = T5_MAX_DIST + DSA_BK
DSA_TAB = DSA_DCAP + DSA_BK + 2 * LANES
DSA_TRI = 256
DSA_RC = 32
KTH_ROWS = 128
MOE_EXPERTS_PER_STEP = 2
DIL_BLOCKS_PER_STEP = 8


def _cparams(*sem):
    return pltpu.CompilerParams(dimension_semantics=sem, vmem_limit_bytes=VMEM_LIMIT_BYTES)


def _rms(x, g):
    return x * lax.rsqrt(jnp.mean(x * x, axis=-1, keepdims=True) + EPS) * g


def _dot(a, b):
    return jnp.dot(a, b, preferred_element_type=F32)


def _dot_nt(a, b):
    return lax.dot_general(a, b, (((1,), (1,)), ((), ())), preferred_element_type=F32)


def _dot_tn(a, b):
    return lax.dot_general(a, b, (((0,), (0,)), ((), ())), preferred_element_type=F32)


def _sigmoid(x):
    return 1.0 / (1.0 + jnp.exp(-x))


def _t5_bucket(dist):
    max_exact = T5_BUCKETS // 2
    d = jnp.maximum(dist, 0)
    scaled = jnp.log(jnp.maximum(d, 1).astype(F32) / max_exact) / math.log(T5_MAX_DIST / max_exact)
    large = jnp.minimum(max_exact + (scaled * (T5_BUCKETS - max_exact)).astype(I32), T5_BUCKETS - 1)
    return jnp.where(d < max_exact, d, large)


def _norm_matmul_kernel(x_ref, g_ref, w_ref, *out_refs, segs):
    xn = _rms(x_ref[...], g_ref[...]).astype(BF16)
    for o_ref, (start, width) in zip(out_refs, segs):
        o_ref[...] = _dot(xn, w_ref[:, start:start + width]).astype(o_ref.dtype)


def norm_matmul(x, gain, w, segs, tm):
    t, d = x.shape
    n = w.shape[1]
    kern = functools.partial(_norm_matmul_kernel, segs=tuple((s, wd) for s, wd, _ in segs))
    return pl.pallas_call(
        kern, grid=(t // tm,),
        in_specs=[pl.BlockSpec((tm, d), lambda i: (i, 0)),
                  pl.BlockSpec((1, d), lambda i: (0, 0)),
                  pl.BlockSpec((d, n), lambda i: (0, 0))],
        out_specs=[pl.BlockSpec((tm, wd), lambda i: (i, 0)) for _, wd, _ in segs],
        out_shape=[jax.ShapeDtypeStruct((t, wd), dt) for _, wd, dt in segs],
        compiler_params=_cparams("parallel"), name="norm_matmul")(x, gain.reshape(1, d), w)


def _dil_prompt_kernel(q_ref, kc_ref, kp_ref, vc_ref, vp_ref, bias_ref, o_ref, lse_ref):
    n = pl.program_id(1)
    blk = kp_ref.shape[0]
    nblk = q_ref.shape[0] // blk
    qi = lax.broadcasted_iota(I32, (blk, 2 * blk), 0)
    kj = lax.broadcasted_iota(I32, (blk, 2 * blk), 1)
    m = blk + qi - kj
    band = (m >= 0) & (m <= blk)
    for i in range(nblk):
        rows = slice(i * blk, (i + 1) * blk)
        q = q_ref[rows, :]
        if i == 0:
            k_prev, v_prev = kp_ref[...], vp_ref[...]
            mask = band & (kj >= jnp.where(n > 0, 0, blk))
        else:
            prev_rows = slice((i - 1) * blk, i * blk)
            k_prev, v_prev = kc_ref[prev_rows, :], vc_ref[prev_rows, :]
            mask = band
        kw = jnp.concatenate([k_prev, kc_ref[rows, :]], axis=0)
        vw = jnp.concatenate([v_prev, vc_ref[rows, :]], axis=0)
        outs, lses = [], []
        for h in range(A_GROUP_HEADS):
            cs = slice(h * HEAD_DIM, (h + 1) * HEAD_DIM)
            s = _dot_nt(q[:, cs].astype(BF16), kw[:, cs].astype(BF16)) * (HEAD_DIM ** -0.5) + bias_ref[h]
            s = jnp.where(mask, s, NEG_INF)
            mx = jnp.max(s, axis=1, keepdims=True)
            p = jnp.exp(s - mx)
            l = jnp.sum(p, axis=1, keepdims=True)
            outs.append(_dot((p / l).astype(BF16), vw[:, cs].astype(BF16)))
            lses.append(jnp.broadcast_to(mx + jnp.log(l), (blk, HEAD_DIM)))
        o_ref[rows, :] = jnp.concatenate(outs, axis=1)
        lse_ref[rows, :] = jnp.concatenate(lses, axis=1)


def dil_prompt(aqkv, bias_qk, g, dil):
    seq = aqkv.shape[0]
    blk = bias_qk.shape[1]
    nb = seq // (blk * dil)
    per = math.gcd(nb, DIL_BLOCKS_PER_STEP)
    ncol = aqkv.shape[1] // LANES
    a = aqkv.reshape(seq // dil, dil * aqkv.shape[1])
    ng = len(A_PATTERNS)

    def cur(off):
        return pl.BlockSpec((per * blk, LANES), lambda r, n: (n, r * ncol + off + g))

    def prev(off):
        return pl.BlockSpec((blk, LANES), lambda r, n: (jnp.maximum(n * per - 1, 0), r * ncol + off + g))

    o, lse = pl.pallas_call(
        _dil_prompt_kernel, grid=(dil, nb // per),
        in_specs=[cur(0), cur(ng), prev(ng), cur(2 * ng), prev(2 * ng),
                  pl.BlockSpec(bias_qk.shape, lambda r, n: (0, 0, 0))],
        out_specs=[pl.BlockSpec((per * blk, LANES), lambda r, n: (n, r))] * 2,
        out_shape=[jax.ShapeDtypeStruct((seq // dil, dil * LANES), F32)] * 2,
        compiler_params=_cparams("parallel", "parallel"), name="dil_prompt")(a, a, a, a, a, bias_qk)
    return o.reshape(seq, LANES), lse.reshape(seq, LANES)


def _dil_step_kernel(q_ref, buf_ref, new_ref, bias_ref, o_ref, lse_ref):
    q = q_ref[0]
    rows = q.shape[0]
    head = lax.broadcasted_iota(I32, q.shape, 1) // HEAD_DIM
    qbd = jnp.concatenate([jnp.where(head == h, q, 0.0) for h in range(A_GROUP_HEADS)], axis=0).astype(BF16)
    kt = jnp.concatenate([buf_ref[0, 0, 0], new_ref[0, 0]], axis=1).astype(BF16)
    vt = jnp.concatenate([buf_ref[0, 0, 1], new_ref[0, 1]], axis=1).astype(BF16)
    s = _dot(qbd, kt) * (HEAD_DIM ** -0.5) + bias_ref[...]
    mx = jnp.max(s, axis=1, keepdims=True)
    p = jnp.exp(s - mx)
    l = jnp.sum(p, axis=1, keepdims=True)
    o = _dot_nt((p / l).astype(BF16), vt)
    lse = jnp.broadcast_to(mx + jnp.log(l), o.shape)
    o_ref[0] = jnp.where(head == 0, o[:rows], o[rows:])
    lse_ref[0] = jnp.where(head == 0, lse[:rows], lse[rows:])


def dil_step(q, buf, new, bias, layer):
    nseq, rows, _ = q.shape
    win = buf.shape[4]
    return pl.pallas_call(
        _dil_step_kernel, grid=(nseq,),
        in_specs=[pl.BlockSpec((1, rows, LANES), lambda n: (n, 0, 0)),
                  pl.BlockSpec((1, 1, 2, LANES, win), lambda n: (layer, n, 0, 0, 0)),
                  pl.BlockSpec((1,) + new.shape[1:], lambda n: (n, 0, 0, 0)),
                  pl.BlockSpec(bias.shape, lambda n: (0, 0))],
        out_specs=[pl.BlockSpec((1, rows, LANES), lambda n: (n, 0, 0))] * 2,
        out_shape=[jax.ShapeDtypeStruct((nseq, rows, LANES), F32)] * 2,
        compiler_params=_cparams("parallel"), name="dil_step")(q, buf, new, bias)


def _retention_kernel(bqk_ref, cos_ref, sin_ref, bv_ref, bg_ref, decay_ref, win_ref, wend_ref, gch_ref, gain_ref,
                      s0_ref, out_ref, s_ref):
    c = pl.program_id(1)

    @pl.when(c == 0)
    def _():
        s_ref[...] = s0_ref[...]

    bqk = bqk_ref[...]
    cosv, sinv = cos_ref[...], sin_ref[...]
    w = B_QK_PAD
    q = (bqk[:, 0:w] * cosv + bqk[:, w:2 * w] * sinv) * (B_KEY_DIM ** -0.5)
    k = bqk[:, 2 * w:3 * w] * cosv + bqk[:, 3 * w:4 * w] * sinv
    q_in = q * win_ref[...]
    k_end = k * wend_ref[...]
    bv, bg, gain = bv_ref[...], bg_ref[...], gain_ref[...]
    outs = []
    for h in range(B_HEADS):
        ks = slice(h * B_KEY_DIM, (h + 1) * B_KEY_DIM)
        vs = slice(h * B_VAL_DIM, (h + 1) * B_VAL_DIM)
        vh = bv[:, vs].astype(BF16)
        state = s_ref[0, h]
        intra = _dot_nt(q[:, ks].astype(BF16), k[:, ks].astype(BF16)) * decay_ref[h]
        o = _dot(intra.astype(BF16), vh) + _dot(q_in[:, ks].astype(BF16), state.astype(BF16))
        s_ref[0, h] = gch_ref[h] * state + _dot_tn(k_end[:, ks].astype(BF16), vh)
        mu = jnp.mean(o, axis=1, keepdims=True)
        var = jnp.mean((o - mu) ** 2, axis=1, keepdims=True)
        on = (o - mu) * lax.rsqrt(var + EPS) * gain[:, vs]
        g = bg[:, vs]
        outs.append(g * _sigmoid(g) * on)
    out_ref[...] = jnp.concatenate(outs, axis=1)


def retention(bqk, cosv, sinv, bv, bg, tabs, gain, s0, chunk, nchunk):
    nseq = s0.shape[0]
    rows = bqk.shape[0]
    decay, w_in, w_end, g_chunk = tabs
    row = lambda b, c: (b * nchunk + c, 0)
    pos = lambda b, c: (c, 0)
    fix2 = lambda b, c: (0, 0)
    fix3 = lambda b, c: (0, 0, 0)
    return pl.pallas_call(
        _retention_kernel, grid=(nseq, nchunk),
        in_specs=[pl.BlockSpec((chunk, 4 * B_QK_PAD), row),
                  pl.BlockSpec((chunk, B_QK_PAD), pos), pl.BlockSpec((chunk, B_QK_PAD), pos),
                  pl.BlockSpec((chunk, B_V_WIDTH), row), pl.BlockSpec((chunk, B_V_WIDTH), row),
                  pl.BlockSpec(decay.shape, fix3), pl.BlockSpec(w_in.shape, fix2), pl.BlockSpec(w_end.shape, fix2),
                  pl.BlockSpec(g_chunk.shape, fix3), pl.BlockSpec((1, B_V_WIDTH), fix2),
                  pl.BlockSpec((1,) + s0.shape[1:], lambda b, c: (b, 0, 0, 0))],
        out_specs=[pl.BlockSpec((chunk, B_V_WIDTH), row),
                   pl.BlockSpec((1,) + s0.shape[1:], lambda b, c: (b, 0, 0, 0))],
        out_shape=[jax.ShapeDtypeStruct((rows, B_V_WIDTH), F32), jax.ShapeDtypeStruct(s0.shape, F32)],
        compiler_params=_cparams("parallel", "arbitrary"), name="retention")(
            bqk, cosv, sinv, bv, bg, decay, w_in, w_end, g_chunk, gain.reshape(1, B_V_WIDTH), s0)


def _retention_tables(chunk, pad):
    lg = np.log(1.0 - 2.0 ** (-5.0 - np.arange(B_HEADS, dtype=np.float32))).astype(np.float32)
    lg = jnp.asarray(lg)
    i = jnp.arange(pad, dtype=F32)
    diff = i[:, None] - i[None, :]
    decay = jnp.where(diff >= 0, jnp.exp(jnp.maximum(diff, 0.0)[None] * lg[:, None, None]), 0.0)
    w_end = jnp.exp((chunk - 1.0 - i)[None, :] * lg[:, None])
    w_in = jnp.exp((i + 1.0)[None, :] * lg[:, None])
    g_chunk = jnp.exp(chunk * lg)

    def widen(t):
        t = jnp.repeat(t.T, B_KEY_DIM, axis=1)
        return jnp.pad(t, ((0, 0), (0, B_QK_PAD - B_QK_WIDTH)))

    return decay, widen(w_in), widen(w_end), jnp.broadcast_to(g_chunk[:, None, None], (B_HEADS, 1, B_VAL_DIM))


def _rope_tables(pos):
    half = B_KEY_DIM // 2
    freq = ROPE_BASE ** (-jnp.arange(half, dtype=F32) / half)
    ang = pos.astype(F32)[:, None] * freq[None, :]
    cos, sin = jnp.cos(ang), jnp.sin(ang)
    cos_h = jnp.concatenate([cos, cos], axis=1)
    sin_h = jnp.concatenate([-sin, sin], axis=1)
    padw = ((0, 0), (0, B_QK_PAD - B_QK_WIDTH))
    return jnp.pad(jnp.tile(cos_h, (1, B_HEADS)), padw), jnp.pad(jnp.tile(sin_h, (1, B_HEADS)), padw)


def _ab_out_kernel(h_ref, o0_ref, o1_ref, o2_ref, l0_ref, l1_ref, l2_ref, b_ref, w_ref, out_ref):
    ls = [l0_ref[...], l1_ref[...], l2_ref[...]]
    os_ = [o0_ref[...], o1_ref[...], o2_ref[...]]
    mx = jnp.maximum(jnp.maximum(ls[0], ls[1]), ls[2])
    es = [jnp.exp(l - mx) for l in ls]
    den = es[0] + es[1] + es[2]
    acc = h_ref[...]
    for g in range(3):
        acc = acc + _dot((os_[g] * (es[g] / den)).astype(BF16), w_ref[g * LANES:(g + 1) * LANES, :])
    acc = acc + _dot(b_ref[...].astype(BF16), w_ref[A_WIDTH:, :])
    out_ref[...] = acc


def ab_out(h, os_, ls, b_out, w, tm):
    t, d = h.shape
    row = lambda i: (i, 0)
    small = pl.BlockSpec((tm, LANES), row)
    return pl.pallas_call(
        _ab_out_kernel, grid=(t // tm,),
        in_specs=[pl.BlockSpec((tm, d), row)] + [small] * 6 +
                 [pl.BlockSpec((tm, B_V_WIDTH), row), pl.BlockSpec(w.shape, lambda i: (0, 0))],
        out_specs=pl.BlockSpec((tm, d), row),
        out_shape=jax.ShapeDtypeStruct((t, d), F32),
        compiler_params=_cparams("parallel"), name="ab_out")(h, *os_, *ls, b_out, w)


def _cd_out_kernel(h_ref, c_ref, d_ref, w_ref, out_ref):
    acc = h_ref[...] + _dot(c_ref[...].astype(BF16), w_ref[:C_WIDTH, :])
    out_ref[...] = acc + _dot(d_ref[...].astype(BF16), w_ref[C_WIDTH:, :])


def cd_out(h, c_out, d_out, w, tm):
    t, d = h.shape
    row = lambda i: (i, 0)
    return pl.pallas_call(
        _cd_out_kernel, grid=(t // tm,),
        in_specs=[pl.BlockSpec((tm, d), row), pl.BlockSpec((tm, C_WIDTH), row), pl.BlockSpec((tm, D_WIDTH), row),
                  pl.BlockSpec(w.shape, lambda i: (0, 0))],
        out_specs=pl.BlockSpec((tm, d), row),
        out_shape=jax.ShapeDtypeStruct((t, d), F32),
        compiler_params=_cparams("parallel"), name="cd_out")(h, c_out, d_out, w)


def _route(logits):
    lane = lax.broadcasted_iota(I32, logits.shape, 1).astype(F32)
    gmask = lane < MOE_GROUPS
    gl = jnp.where(gmask, logits, NEG_INF)
    gmax = jnp.max(gl, axis=1, keepdims=True)
    gsum = jnp.sum(jnp.where(gmask, jnp.exp(gl - gmax), 0.0), axis=1, keepdims=True)
    g_w = 1.0 / gsum
    g_idx = jnp.min(jnp.where(gl == gmax, lane, LANES), axis=1, keepdims=True)
    lo = MOE_GROUPS + MOE_GROUP_EXPERTS * g_idx
    emask = (lane >= lo) & (lane < lo + MOE_GROUP_EXPERTS)
    el = jnp.where(emask, logits, NEG_INF)
    emax = jnp.max(el, axis=1, keepdims=True)
    eexp = jnp.where(emask, jnp.exp(el - emax), 0.0)
    p = eexp / jnp.sum(eexp, axis=1, keepdims=True)
    pm = jnp.where(emask, p, -1.0)
    p1 = jnp.max(pm, axis=1, keepdims=True)
    i1 = jnp.min(jnp.where(pm == p1, lane, LANES), axis=1, keepdims=True)
    pm2 = jnp.where(lane == i1, -1.0, pm)
    p2 = jnp.max(pm2, axis=1, keepdims=True)
    i2 = jnp.min(jnp.where(pm2 == p2, lane, LANES), axis=1, keepdims=True)
    den = p1 + p2
    return jnp.where(lane == i1, g_w * (p1 / den), jnp.where(lane == i2, g_w * (p2 / den), 0.0))


def _moe_kernel(h_ref, g_ref, wr_ref, br_ref, wg_ref, wu_ref, wd_ref, out_ref, xn_s, comb_s, acc_s):
    e = pl.program_id(1)

    @pl.when(e == 0)
    def _():
        xn = _rms(h_ref[...], g_ref[...]).astype(BF16)
        xn_s[...] = xn
        comb_s[...] = _route(_dot(xn, wr_ref[...]) + br_ref[...])
        acc_s[...] = jnp.zeros_like(acc_s)

    xb = xn_s[...]
    comb = comb_s[...]
    lane = lax.broadcasted_iota(I32, comb.shape, 1)
    acc = acc_s[...]
    for j in range(MOE_EXPERTS_PER_STEP):
        hg = _dot(xb, wg_ref[j])
        hu = _dot(xb, wu_ref[j])
        hdn = (hg * _sigmoid(hg)) * hu
        y = _dot(hdn.astype(BF16), wd_ref[j])
        expert_lane = MOE_GROUPS + e * MOE_EXPERTS_PER_STEP + j
        cw = jnp.sum(jnp.where(lane == expert_lane, comb, 0.0), axis=1, keepdims=True)
        acc = acc + cw * y
    acc_s[...] = acc

    @pl.when(e == pl.num_programs(1) - 1)
    def _():
        out_ref[...] = h_ref[...] + acc_s[...]


def moe(h, gain, wr, br, wg, wu, wd, tm):
    t, d = h.shape
    row = lambda i, e: (i, 0)
    fix = lambda i, e: (0, 0)
    ex = lambda i, e: (e, 0, 0)
    per = MOE_EXPERTS_PER_STEP
    return pl.pallas_call(
        _moe_kernel, grid=(t // tm, MOE_EXPERTS // per),
        in_specs=[pl.BlockSpec((tm, d), row), pl.BlockSpec((1, d), fix),
                  pl.BlockSpec((d, LANES), fix), pl.BlockSpec((1, LANES), fix),
                  pl.BlockSpec((per, d, D_EXPERT), ex), pl.BlockSpec((per, d, D_EXPERT), ex),
                  pl.BlockSpec((per, D_EXPERT, d), ex)],
        out_specs=pl.BlockSpec((tm, d), row),
        out_shape=jax.ShapeDtypeStruct((t, d), F32),
        scratch_shapes=[pltpu.VMEM((tm, d), BF16), pltpu.VMEM((tm, LANES), F32), pltpu.VMEM((tm, d), F32)],
        compiler_params=_cparams("parallel", "arbitrary"), name="moe")(
            h, gain.reshape(1, d), wr, br, wg, wu, wd)


def _ple_kernel(h_ref, p_ref, g_ref, wg_ref, wp_ref, *rest, final):
    h = h_ref[...]
    gate = _sigmoid(_dot(_rms(h, g_ref[...]).astype(BF16), wg_ref[...]))
    hn = h + gate * _dot(p_ref[...].astype(BF16), wp_ref[...])
    if final:
        gf_ref, out_ref = rest
        out_ref[...] = _rms(hn, gf_ref[...])
    else:
        rest[0][...] = hn


def ple(h, p, gain, wg, wp, gain_final, tm):
    t, d = h.shape
    row = lambda i: (i, 0)
    fix = lambda i: (0, 0)
    final = gain_final is not None
    in_specs = [pl.BlockSpec((tm, d), row), pl.BlockSpec((tm, p.shape[1]), row), pl.BlockSpec((1, d), fix),
                pl.BlockSpec(wg.shape, fix), pl.BlockSpec(wp.shape, fix)]
    args = [h, p, gain.reshape(1, d), wg, wp]
    if final:
        in_specs.append(pl.BlockSpec((1, d), fix))
        args.append(gain_final.reshape(1, d))
    return pl.pallas_call(
        functools.partial(_ple_kernel, final=final), grid=(t // tm,),
        in_specs=in_specs, out_specs=pl.BlockSpec((tm, d), row),
        out_shape=jax.ShapeDtypeStruct((t, d), F32),
        compiler_params=_cparams("parallel"), name="ple")(*args)


def _s5_kernel(u_ref, a_ref, bre_ref, bim_ref, cre_ref, cim_ref, d_ref, wglu_ref, bglu_ref, x0r_ref, x0i_ref,
               out_ref, xr_ref, xi_ref, bur_s, bui_s, *, nbatch, steps):
    c = pl.program_id(0)

    @pl.when(c == 0)
    def _():
        xr_ref[...] = x0r_ref[...]
        xi_ref[...] = x0i_ref[...]

    u = u_ref[...]
    ub = u.astype(BF16)
    nsb = bre_ref.shape[0]
    wi, ws = C_WIDTH // nsb, C_FLAT // nsb
    for j in range(nsb):
        ubj = ub[:, j * wi:(j + 1) * wi]
        bur_s[:, j * ws:(j + 1) * ws] = _dot(ubj, bre_ref[j])
        bui_s[:, j * ws:(j + 1) * ws] = _dot(ubj, bim_ref[j])
    ar = a_ref[0:1, :]
    ai = a_ref[1:2, :]

    def step(t, carry):
        xr, xi = carry
        r0 = pl.multiple_of(t * nbatch, nbatch)
        nr = ar * xr - ai * xi + bur_s[pl.ds(r0, nbatch), :]
        ni = ar * xi + ai * xr + bui_s[pl.ds(r0, nbatch), :]
        bur_s[pl.ds(r0, nbatch), :] = nr
        bui_s[pl.ds(r0, nbatch), :] = ni
        return nr, ni

    xr, xi = lax.fori_loop(0, steps, step, (xr_ref[...], xi_ref[...]), unroll=min(steps, 8))
    xr_ref[...] = xr
    xi_ref[...] = xi
    y = jnp.concatenate(
        [_dot(bur_s[:, j * ws:(j + 1) * ws].astype(BF16), cre_ref[j]) -
         _dot(bui_s[:, j * ws:(j + 1) * ws].astype(BF16), cim_ref[j]) for j in range(nsb)], axis=1) + d_ref[...] * u
    z = 0.5 * y * (1.0 + jnp.tanh(math.sqrt(2.0 / math.pi) * (y + 0.044715 * (y * y * y))))
    out_ref[...] = z * _sigmoid(_dot(z.astype(BF16), wglu_ref[...]) + bglu_ref[...])


def s5(u, prm, x0r, x0i, nbatch, steps):
    rows = u.shape[0]
    a, bre, bim, cre, cim, dskip, wglu, bglu = prm
    nchunk = rows // (nbatch * steps)
    tr = nbatch * steps
    full = lambda arr: pl.BlockSpec(arr.shape, lambda c: (0,) * arr.ndim)
    return pl.pallas_call(
        functools.partial(_s5_kernel, nbatch=nbatch, steps=steps), grid=(nchunk,),
        in_specs=[pl.BlockSpec((tr, C_WIDTH), lambda c: (c, 0))] +
                 [full(t) for t in (a, bre, bim, cre, cim, dskip, wglu, bglu, x0r, x0i)],
        out_specs=[pl.BlockSpec((tr, C_WIDTH), lambda c: (c, 0)), full(x0r), full(x0i)],
        out_shape=[jax.ShapeDtypeStruct((rows, C_WIDTH), F32), jax.ShapeDtypeStruct(x0r.shape, F32),
                   jax.ShapeDtypeStruct(x0i.shape, F32)],
        scratch_shapes=[pltpu.VMEM((tr, C_FLAT), F32), pltpu.VMEM((tr, C_FLAT), F32)],
        compiler_params=_cparams("arbitrary"), name="s5")(u, a, bre, bim, cre, cim, dskip, wglu, bglu, x0r, x0i)


def _s5_prep(lam_re, lam_im, log_dt, b_re, b_im, c_re, c_im, d_skip, w_glu, b_glu):
    dt = jnp.exp(log_dt)[:, None]
    mag = jnp.exp(lam_re * dt)
    ab_re, ab_im = mag * jnp.cos(lam_im * dt), mag * jnp.sin(lam_im * dt)
    den = lam_re * lam_re + lam_im * lam_im
    zr = ((ab_re - 1.0) * lam_re + ab_im * lam_im) / den
    zi = (ab_im * lam_re - (ab_re - 1.0) * lam_im) / den
    bb_re = zr[..., None] * b_re - zi[..., None] * b_im
    bb_im = zr[..., None] * b_im + zi[..., None] * b_re
    eye = jnp.eye(C_GROUPS, dtype=F32)
    nsb = C_GROUPS // C_SUPER
    wi, ws = C_WIDTH // nsb, C_FLAT // nsb

    def diag_blocks(m, rows, cols):
        return jnp.stack([m[j * rows:(j + 1) * rows, j * cols:(j + 1) * cols] for j in range(nsb)], axis=0)

    b_dense = lambda bb: diag_blocks(jnp.einsum('gnc,gh->gchn', bb, eye).reshape(C_WIDTH, C_FLAT), wi, ws).astype(BF16)
    c_dense = lambda cc: diag_blocks(jnp.einsum('gcn,gh->gnhc', cc, eye).reshape(C_FLAT, C_WIDTH), ws, wi).astype(BF16)
    a = jnp.stack([ab_re.reshape(C_FLAT), ab_im.reshape(C_FLAT)], axis=0)
    return (a, b_dense(bb_re), b_dense(bb_im), c_dense(c_re), c_dense(c_im), d_skip.reshape(1, C_WIDTH),
            w_glu.astype(BF16), b_glu.reshape(1, C_WIDTH))


def _sortable(x):
    bits = lax.bitcast_convert_type(x, I32)
    return bits ^ ((bits >> 31) & jnp.int32(0x7FFFFFFF))


def _kth_largest(keys_ref, ntiles, tw, n_sel):
    total_rows = keys_ref.shape[0]
    rows = min(total_rows, KTH_ROWS)
    thrs, rems = [], []
    for r0 in range(0, total_rows, rows):
        def count_ge(cand, r0=r0):
            def body(t, cv):
                off = pl.multiple_of(t * tw, LANES)
                kk = keys_ref[r0:r0 + rows, pl.ds(off, tw)]
                for j in range(tw // LANES):
                    cv = cv + jnp.where(kk[:, j * LANES:(j + 1) * LANES] >= cand, 1.0, 0.0)
                return cv
            cv = lax.fori_loop(0, ntiles, body, jnp.zeros((rows, LANES), F32))
            return jnp.sum(cv, axis=1, keepdims=True)

        def bit(i, carry, count_ge=count_ge):
            thr, c_fail = carry
            cand = thr + lax.shift_left(jnp.int32(1), 31 - i)
            c = count_ge(cand)
            ok = c >= n_sel
            return jnp.where(ok, cand, thr), jnp.where(ok, c_fail, c)

        thr, c_gt = lax.fori_loop(0, 32, bit, (jnp.full((rows, 1), INT_MIN, I32), jnp.zeros((rows, 1), F32)))
        thrs.append(thr)
        rems.append(jnp.where(thr == INT_MIN, 0.0, n_sel - c_gt))
    if len(thrs) == 1:
        return thrs[0], rems[0]
    return jnp.concatenate(thrs, axis=0), jnp.concatenate(rems, axis=0)


def _select(key, thr, rem, cnt, tri):
    eq = key == thr
    eqb = jnp.where(eq, 1.0, 0.0).astype(BF16)
    w = tri.shape[0]
    ranks = []
    for c in range(key.shape[1] // w):
        pc = _dot(eqb[:, c * w:(c + 1) * w], tri)
        ranks.append(cnt + pc)
        cnt = cnt + pc[:, -1:]
    rank = ranks[0] if len(ranks) == 1 else jnp.concatenate(ranks, axis=1)
    sel = (key > thr) | (eq & (rank <= rem))
    return sel, cnt


def _dsa_bias_kernel(tbr_ref, out_ref):
    tq, tk = DSA_BQ, DSA_BK
    start = pl.multiple_of(DSA_DCAP - pl.program_id(0) * LANES, LANES)
    wtab = tbr_ref[:, pl.ds(start, tk + 2 * LANES)]
    for h in range(D_HEADS):
        wb = jnp.broadcast_to(wtab[h:h + 1, :], (tq, tk + 2 * LANES))
        out_ref[0, h] = pltpu.roll(wb, 1, 1, stride=1, stride_axis=0)[:, LANES:LANES + tk]


def dsa_bias_tiles(tbr):
    nd = DSA_DCAP // LANES + 1
    return pl.pallas_call(
        _dsa_bias_kernel, grid=(nd,),
        in_specs=[pl.BlockSpec(tbr.shape, lambda d: (0, 0))],
        out_specs=pl.BlockSpec((1, D_HEADS, DSA_BQ, DSA_BK), lambda d: (d, 0, 0, 0)),
        out_shape=jax.ShapeDtypeStruct((nd, D_HEADS, DSA_BQ, DSA_BK), F32),
        compiler_params=_cparams("parallel"), name="dsa_bias_tiles")(tbr)


def _dsa_prompt_kernel(qb_ref, ktile_ref, q_ref, iq_ref, misc_ref, ikt_ref, kt_ref, vt_ref, *rest, n_sel):
    tq, tk = DSA_TQ, DSA_TK
    nqa, nkb = tq // DSA_BQ, tk // DSA_BK
    bias_refs = rest[:nqa * nkb]
    tri_ref, o_ref, keys_s, thr_s, rem_s, cnt_s, neg_s, m_s, l_s, acc_s, s_s, p_s, alpha_s = rest[nqa * nkb:]
    qb = qb_ref[pl.program_id(0)]
    kt = ktile_ref[pl.program_id(0)]
    kl = (qb * tq + tq - 1) // tk

    @pl.when(kt == 0)
    def _():
        iq = iq_ref[...]
        iw = misc_ref[:, IDX_DIM:IDX_DIM + IDX_HEADS]
        qpos = qb * tq + lax.broadcasted_iota(I32, (tq, tk), 0)

        def scores(t, carry):
            off = pl.multiple_of(t * tk, tk)
            ikt = ikt_ref[:, pl.ds(off, tk)]
            sc = jnp.zeros((tq, tk), F32)
            for h in range(IDX_HEADS):
                s = _dot(iq[:, h * IDX_DIM:(h + 1) * IDX_DIM], ikt)
                sc = sc + jnp.maximum(s, 0.0) * iw[:, h:h + 1]
            kpos = off + lax.broadcasted_iota(I32, (tq, tk), 1)
            keys_s[:, pl.ds(off, tk)] = jnp.where(kpos <= qpos, _sortable(sc * IDX_SCALE), INT_MIN)
            return carry

        lax.fori_loop(0, kl + 1, scores, 0)
        thr, rem = _kth_largest(keys_s, kl + 1, tk, n_sel)
        thr_s[...] = thr
        rem_s[...] = rem
        cnt_s[...] = jnp.zeros_like(cnt_s)
        m_s[...] = jnp.full_like(m_s, NEG_INF)
        l_s[...] = jnp.zeros_like(l_s)
        acc_s[...] = jnp.zeros_like(acc_s)

    def attend():
        off = pl.multiple_of(kt * tk, tk)
        sel, cnt = _select(keys_s[:, pl.ds(off, tk)], thr_s[...], rem_s[...], cnt_s[...], tri_ref[...])
        cnt_s[...] = cnt
        neg_s[...] = jnp.where(sel, 0.0, NEG_INF)
        for h in range(D_HEADS):
            sb, pb = s_s.at[h % 2], p_s.at[h % 2]
            sb[...] = _dot(q_ref[h], kt_ref[h * HEAD_DIM:(h + 1) * HEAD_DIM, :])
            for r in range(0, tq, DSA_RC):
                rs = slice(r, r + DSA_RC)
                a, ro = r // DSA_BQ, r % DSA_BQ
                bias = jnp.concatenate([bias_refs[a * nkb + b][0, h, ro:ro + DSA_RC, :] for b in range(nkb)], axis=1)
                s = sb[rs, :] * (LOG2E * HEAD_DIM ** -0.5) + bias + neg_s[rs, :]
                m_old = m_s[h, rs]
                m_new = jnp.maximum(m_old, jnp.max(s, axis=1, keepdims=True))
                m_safe = jnp.where(m_new == NEG_INF, 0.0, m_new)
                p = jnp.exp2(s - m_safe)
                alpha = jnp.exp2(m_old - m_safe)
                l_s[h, rs] = alpha * l_s[h, rs] + jnp.sum(p, axis=1, keepdims=True)
                alpha_s[h % 2, rs] = alpha
                pb[rs, :] = p.astype(BF16)
                m_s[h, rs] = m_new
            acc_s[h] = alpha_s[h % 2] * acc_s[h] + _dot_nt(pb[...], vt_ref[h * HEAD_DIM:(h + 1) * HEAD_DIM, :])

    attend()

    @pl.when(kt == kl)
    def _():
        o_ref[...] = jnp.concatenate([acc_s[h] / l_s[h] for h in range(D_HEADS)], axis=1)


def dsa_prompt(q, iq, misc, ikt, kt_, v, bias_tiles, tri, n_sel):
    seq = q.shape[1]
    tq, tk = DSA_TQ, DSA_TK
    nb = seq // tq
    pairs = [(b, t) for b in range(nb) for t in range((b * tq + tq - 1) // tk + 1)]
    qb_tab = jnp.asarray(np.array([p[0] for p in pairs], np.int32))
    kt_tab = jnp.asarray(np.array([p[1] for p in pairs], np.int32))
    qrow = lambda i, qbt, ktt: (qbt[i], 0)
    fix = lambda i, qbt, ktt: (0, 0)
    ktile = lambda i, qbt, ktt: (0, ktt[i])

    def bias_spec(a, b):
        def idx(i, qbt, ktt):
            d = (qbt[i] * tq + a * DSA_BQ - ktt[i] * tk - b * DSA_BK) // LANES
            return (jnp.clip(d, 0, DSA_DCAP // LANES), 0, 0, 0)
        return pl.BlockSpec((1, D_HEADS, DSA_BQ, DSA_BK), idx)

    bias_specs = [bias_spec(a, b) for a in range(tq // DSA_BQ) for b in range(tk // DSA_BK)]
    grid_spec = pltpu.PrefetchScalarGridSpec(
        num_scalar_prefetch=2, grid=(len(pairs),),
        in_specs=[pl.BlockSpec((D_HEADS, tq, HEAD_DIM), lambda i, qbt, ktt: (0, qbt[i], 0)),
                  pl.BlockSpec((tq, IDX_HEADS * IDX_DIM), qrow),
                  pl.BlockSpec((tq, LANES), qrow), pl.BlockSpec(ikt.shape, fix),
                  pl.BlockSpec((D_WIDTH, tk), ktile), pl.BlockSpec((D_WIDTH, tk), ktile)] +
                 bias_specs + [pl.BlockSpec(tri.shape, fix)],
        out_specs=pl.BlockSpec((tq, D_WIDTH), qrow),
        scratch_shapes=[pltpu.VMEM((tq, seq), I32), pltpu.VMEM((tq, 1), I32), pltpu.VMEM((tq, 1), F32),
                        pltpu.VMEM((tq, 1), F32), pltpu.VMEM((tq, tk), F32), pltpu.VMEM((D_HEADS, tq, 1), F32),
                        pltpu.VMEM((D_HEADS, tq, 1), F32), pltpu.VMEM((D_HEADS, tq, HEAD_DIM), F32),
                        pltpu.VMEM((2, tq, tk), F32), pltpu.VMEM((2, tq, tk), BF16), pltpu.VMEM((2, tq, 1), F32)])
    return pl.pallas_call(
        functools.partial(_dsa_prompt_kernel, n_sel=n_sel), grid_spec=grid_spec,
        out_shape=jax.ShapeDtypeStruct((seq, D_WIDTH), F32),
        compiler_params=_cparams("arbitrary"), name="dsa_prompt")(
            qb_tab, kt_tab, q, iq, misc, ikt, kt_, v, *([bias_tiles] * len(bias_specs)), tri)


def _dsa_step_keys_kernel(pt_ref, iq_ref, iw_ref, *rest, n_new, pg):
    page_refs, new_ref, keys_ref = rest[:pg], rest[pg], rest[pg + 1]
    p = pl.program_id(1)
    npage = pl.num_programs(1) * pg
    psz = new_ref.shape[2]
    rows = keys_ref.shape[1]

    def score(kt):
        r = jnp.maximum(_dot(iq_ref[0], kt.astype(BF16)), 0.0) * iw_ref[0][:, 0:1]
        sc = r[0:rows]
        for h in range(1, IDX_HEADS):
            sc = sc + r[h * rows:(h + 1) * rows]
        return _sortable(sc * IDX_SCALE)

    kt = jnp.concatenate([page_ref[0, 0] for page_ref in page_refs], axis=1)
    keys_ref[0, :, pl.ds(pl.multiple_of(p * (pg * psz), psz), pg * psz)] = score(kt)

    @pl.when(p == 0)
    def _():
        s_i = lax.broadcasted_iota(I32, (rows, psz), 0)
        j_i = lax.broadcasted_iota(I32, (rows, psz), 1)
        ok = (j_i <= s_i) & (j_i < n_new)
        keys_ref[0, :, pl.ds(pl.multiple_of(npage * psz, psz), psz)] = jnp.where(ok, score(new_ref[0]), INT_MIN)


def dsa_step_keys(page_table, iqm, iwm, cache_kidx, ik_new, layer, n_new, pg):
    nseq, npage = page_table.shape
    psz = cache_kidx.shape[3]
    rows = iqm.shape[1] // IDX_HEADS
    ncol = (npage + 1) * psz
    seqb = lambda n, p, pt: (n, 0, 0)
    page_spec = lambda i: pl.BlockSpec((1, 1, IDX_DIM, psz),
                                       lambda n, p, pt: (layer, pt[n * npage + p * pg + i], 0, 0))
    grid_spec = pltpu.PrefetchScalarGridSpec(
        num_scalar_prefetch=1, grid=(nseq, npage // pg),
        in_specs=[pl.BlockSpec((1,) + iqm.shape[1:], seqb), pl.BlockSpec((1,) + iwm.shape[1:], seqb)] +
                 [page_spec(i) for i in range(pg)] + [pl.BlockSpec((1,) + ik_new.shape[1:], seqb)],
        out_specs=pl.BlockSpec((1, rows, ncol), seqb))
    return pl.pallas_call(
        functools.partial(_dsa_step_keys_kernel, n_new=n_new, pg=pg), grid_spec=grid_spec,
        out_shape=jax.ShapeDtypeStruct((nseq, rows, ncol), I32),
        compiler_params=_cparams("parallel", "arbitrary"), name="dsa_step_keys")(
            page_table.reshape(-1), iqm, iwm, *([cache_kidx] * pg), ik_new)


def _topk_mask_kernel(keys_ref, tri_ref, neg_ref, *, tw, n_sel):
    thr, rem = _kth_largest(keys_ref, keys_ref.shape[1] // tw, tw, n_sel)
    w = tri_ref.shape[0]

    def chunk(c, cnt):
        off = pl.multiple_of(c * w, w)
        sel, cnt = _select(keys_ref[:, pl.ds(off, w)], thr, rem, cnt, tri_ref[...])
        neg_ref[:, pl.ds(off, w)] = jnp.where(sel, 0.0, NEG_INF)
        return cnt

    lax.fori_loop(0, keys_ref.shape[1] // w, chunk, jnp.zeros(thr.shape, F32))


def topk_mask_rows(keys, tri, tw, n_sel, tr):
    rows, ncol = keys.shape
    return pl.pallas_call(
        functools.partial(_topk_mask_kernel, tw=tw, n_sel=n_sel), grid=(rows // tr,),
        in_specs=[pl.BlockSpec((tr, ncol), lambda i: (i, 0)), pl.BlockSpec(tri.shape, lambda i: (0, 0))],
        out_specs=pl.BlockSpec((tr, ncol), lambda i: (i, 0)),
        out_shape=jax.ShapeDtypeStruct((rows, ncol), F32),
        compiler_params=_cparams("parallel"), name="topk_mask_rows")(keys, tri)


def _dsa_step_attn_kernel(pt_ref, q_ref, neg_ref, neg_new_ref, bias_ref, bias_new_ref, *rest, pg):
    k_refs, v_refs = rest[:pg], rest[pg:2 * pg]
    kn_ref, vn_ref, o_ref, m_s, l_s, acc_s = rest[2 * pg:]
    p = pl.program_id(1)
    nstep = pl.num_programs(1) - 1
    rows = neg_ref.shape[1]

    @pl.when(p == 0)
    def _():
        m_s[...] = jnp.full_like(m_s, NEG_INF)
        l_s[...] = jnp.zeros_like(l_s)
        acc_s[...] = jnp.zeros_like(acc_s)

    def process(kt, vt, neg, bias):
        s = _dot(q_ref[0], kt.astype(BF16)) * (LOG2E * HEAD_DIM ** -0.5) + bias
        s = s + jnp.concatenate([neg] * D_HEADS, axis=0)
        m_old = m_s[...]
        m_new = jnp.maximum(m_old, jnp.max(s, axis=1, keepdims=True))
        m_safe = jnp.where(m_new == NEG_INF, 0.0, m_new)
        pr = jnp.exp2(s - m_safe)
        alpha = jnp.exp2(m_old - m_safe)
        l_s[...] = alpha * l_s[...] + jnp.sum(pr, axis=1, keepdims=True)
        acc_s[...] = alpha * acc_s[...] + _dot_nt(pr.astype(BF16), vt.astype(BF16))
        m_s[...] = m_new

    @pl.when(p < nstep)
    def _():
        process(jnp.concatenate([r[0, 0, 0] for r in k_refs], axis=1),
                jnp.concatenate([r[0, 0, 0] for r in v_refs], axis=1), neg_ref[0], bias_ref[...])

    @pl.when(p == nstep)
    def _():
        process(kn_ref[0, 0], vn_ref[0, 0], neg_new_ref[0], bias_new_ref[...])
        accn = acc_s[...] / l_s[...]
        head = lax.broadcasted_iota(I32, (rows, D_WIDTH), 1) // HEAD_DIM
        out = jnp.zeros((rows, D_WIDTH), F32)
        for h in range(D_HEADS):
            out = out + jnp.where(head == h, accn[h * rows:(h + 1) * rows], 0.0)
        o_ref[0] = out


def dsa_step_attn(page_table, qbd, neg, bias, cache_kvt, kvt_new, layer, pg):
    nseq, npage = page_table.shape
    psz = cache_kvt.shape[4]
    rows = neg.shape[1]
    qr = qbd.shape[1]
    nstep = npage // pg
    seqb = lambda n, p, pt: (n, 0, 0)
    stp = lambda p: jnp.minimum(p, nstep - 1)

    def page_spec(i, part):
        return pl.BlockSpec((1, 1, 1, D_WIDTH, psz),
                            lambda n, p, pt: (layer, pt[n * npage + stp(p) * pg + i], part, 0, 0))

    def new_spec(part):
        return pl.BlockSpec((1, 1, D_WIDTH, psz), lambda n, p, pt: (n, part, 0, 0))

    grid_spec = pltpu.PrefetchScalarGridSpec(
        num_scalar_prefetch=1, grid=(nseq, nstep + 1),
        in_specs=[pl.BlockSpec((1, qr, D_WIDTH), seqb),
                  pl.BlockSpec((1, rows, pg * psz), lambda n, p, pt: (n, 0, stp(p))),
                  pl.BlockSpec((1, rows, psz), lambda n, p, pt: (n, 0, npage)),
                  pl.BlockSpec((qr, pg * psz), lambda n, p, pt: (0, stp(p))),
                  pl.BlockSpec((qr, psz), lambda n, p, pt: (0, npage))] +
                 [page_spec(i, 0) for i in range(pg)] + [page_spec(i, 1) for i in range(pg)] +
                 [new_spec(0), new_spec(1)],
        out_specs=pl.BlockSpec((1, rows, D_WIDTH), seqb),
        scratch_shapes=[pltpu.VMEM((qr, 1), F32), pltpu.VMEM((qr, 1), F32), pltpu.VMEM((qr, D_WIDTH), F32)])
    return pl.pallas_call(
        functools.partial(_dsa_step_attn_kernel, pg=pg), grid_spec=grid_spec,
        out_shape=jax.ShapeDtypeStruct((nseq, rows, D_WIDTH), F32),
        compiler_params=_cparams("parallel", "arbitrary"), name="dsa_step_attn")(
            page_table.reshape(-1), qbd, neg, neg, bias, bias, *([cache_kvt] * (2 * pg)), kvt_new, kvt_new)


def _prep_ab_w(w):
    d = w.shape[0]
    aq, ak, av, bq, bk, bv, bg = _split_cols(w, [A_WIDTH] * 3 + [B_QK_WIDTH] * 2 + [B_V_WIDTH] * 2)
    half = B_KEY_DIM // 2

    def rot(m):
        return m.reshape(d, B_HEADS, 2, half)[:, :, ::-1, :].reshape(d, B_QK_WIDTH)

    pad = jnp.zeros((d, B_QK_PAD - B_QK_WIDTH), w.dtype)
    cols = [aq, ak, av, bq, pad, rot(bq), pad, bk, pad, rot(bk), pad, bv, bg]
    return jnp.concatenate(cols, axis=1).astype(BF16)


AB_SEGS = ((0, 3 * A_WIDTH, F32), (3 * A_WIDTH, 4 * B_QK_PAD, F32),
           (3 * A_WIDTH + 4 * B_QK_PAD, B_V_WIDTH, F32), (3 * A_WIDTH + 4 * B_QK_PAD + B_V_WIDTH, B_V_WIDTH, F32))


def _split_cols(x, widths):
    parts, start = [], 0
    for wd in widths:
        parts.append(x[..., start:start + wd])
        start += wd
    return parts


def _prep_cd_w(w):
    d = w.shape[0]
    cu, dq, dk, dv, iq, iw, ik = _split_cols(w, [C_WIDTH, D_WIDTH, D_WIDTH, D_WIDTH, IDX_HEADS * IDX_DIM, IDX_HEADS, IDX_DIM])
    pad = jnp.zeros((d, LANES - IDX_DIM - IDX_HEADS), w.dtype)
    return jnp.concatenate([cu, dq, dk, dv, iq, ik, iw, pad], axis=1).astype(BF16)


CD_SEGS = ((0, C_WIDTH, F32), (C_WIDTH, D_WIDTH, BF16), (C_WIDTH + D_WIDTH, 2 * D_WIDTH, F32),
           (C_WIDTH + 3 * D_WIDTH, IDX_HEADS * IDX_DIM, BF16), (C_WIDTH + 3 * D_WIDTH + IDX_HEADS * IDX_DIM, LANES, F32))


def _bias_by_distance(t5_table, dist):
    onehot = (_t5_bucket(dist)[..., None] == jnp.arange(T5_BUCKETS)).astype(F32)
    return jnp.einsum('...b,bh->...h', onehot, t5_table.astype(F32), precision=lax.Precision.HIGHEST)


def _dil_group_bias(t5_table, g, dil, nkey):
    return _bias_by_distance(t5_table, jnp.arange(nkey + 1) * dil)[:, g * A_GROUP_HEADS:(g + 1) * A_GROUP_HEADS]


def _dil_prompt_bias(t5_table, g, dil, nkey):
    bias = _dil_group_bias(t5_table, g, dil, nkey)
    ext = jnp.concatenate([jnp.broadcast_to(bias[:1], (nkey - 1, A_GROUP_HEADS)), bias,
                           jnp.broadcast_to(bias[-1:], (nkey - 1, A_GROUP_HEADS))], axis=0)[::-1]
    rows = [lax.slice_in_dim(ext, nkey - 1 - qi, nkey - 1 - qi + 2 * nkey, axis=0) for qi in range(nkey)]
    return jnp.moveaxis(jnp.stack(rows, axis=0), -1, 0)


def _dil_step_bias(t5_table, g, dil, nkey, n_buf, n_new, rows, pad_new):
    assert n_buf == nkey * dil
    rev = _dil_group_bias(t5_table, g, dil, nkey)[::-1]
    width = n_buf + pad_new
    out = []
    for s in range(rows):
        if s < n_new:
            blockc = jnp.full((nkey + 1, dil, A_GROUP_HEADS), NEG_INF, F32).at[:, s % dil, :].set(rev)
            flat = blockc.reshape((nkey + 1) * dil, A_GROUP_HEADS)
            front = s - s % dil
            row = jnp.pad(flat, ((front, width - front - flat.shape[0]), (0, 0)), constant_values=NEG_INF)
        else:
            row = jnp.full((width, A_GROUP_HEADS), NEG_INF, F32).at[0].set(0.0)
        out.append(row)
    return jnp.stack(out, axis=0).transpose(2, 0, 1).reshape(A_GROUP_HEADS * rows, width)


def _dsa_prompt_table(t5_table):
    d = (DSA_DCAP + LANES - 1) - jnp.arange(DSA_TAB)
    return _bias_by_distance(t5_table, d)[:, :D_HEADS].T * LOG2E


def _tri(n):
    r = np.arange(n)
    return jnp.asarray((r[:, None] <= r[None, :]).astype(np.float32)).astype(BF16)


def _ffn_and_ple(h, p, i, prm, tm, final):
    h = moe(h, prm['norm_ffn'][i], prm['moe_wr'][i], prm['moe_br'][i], prm['moe_wg'][i], prm['moe_wu'][i],
            prm['moe_wd'][i], min(4 * tm, h.shape[0]))
    return ple(h, p, prm['norm_ple'][i], prm['ple_wg'][i], prm['ple_wp'][i], prm['norm_final'] if final else None, tm)


def _layer_ab_prompt(h, prm, j, tm):
    seq = h.shape[0]
    aqkv, bqk, bv, bg = norm_matmul(h, prm['norm_mix'][2 * j], prm['ab_w'][j], AB_SEGS, tm)
    os_, ls, bufs = [], [], []
    for g, (win, dil) in enumerate(A_PATTERNS):
        nkey = win // dil
        assert seq % (dil * nkey) == 0
        o, lse = dil_prompt(aqkv, _dil_prompt_bias(prm['t5_table'], g, dil, nkey), g, dil)
        os_.append(o)
        ls.append(lse)
        kcols = aqkv[seq - min(win, seq):, A_WIDTH + g * LANES:A_WIDTH + (g + 1) * LANES]
        vcols = aqkv[seq - min(win, seq):, 2 * A_WIDTH + g * LANES:2 * A_WIDTH + (g + 1) * LANES]
        bufs.append(jnp.stack([kcols, vcols], axis=1).reshape(1, -1, 2, A_GROUP_HEADS, HEAD_DIM))
    chunk = B_CHUNK if seq % B_CHUNK == 0 else seq
    cosv, sinv = _rope_tables(jnp.arange(seq))
    s0 = jnp.zeros((1, B_HEADS, B_KEY_DIM, B_VAL_DIM), F32)
    b_out, s_new = retention(bqk, cosv, sinv, bv, bg, _retention_tables(chunk, chunk), prm['ab_gn_gain'][j], s0,
                             chunk, seq // chunk)
    h = ab_out(h, os_, ls, b_out, prm['ab_w_out'][j], tm)
    return h, bufs, s_new


def _layer_ab_sample(h, prm, j, past, nseq, n_new, pos0):
    rows = 8
    aqkv, bqk, bv, bg = norm_matmul(h, prm['norm_mix'][2 * j], prm['ab_w'][j], AB_SEGS, h.shape[0])
    a3 = aqkv.reshape(nseq, n_new, 3 * A_WIDTH)
    pad_rows = lambda x, r: jnp.pad(x, ((0, 0), (0, r - x.shape[1]), (0, 0)))
    os_, ls, bufs = [], [], []
    for g, (win, dil) in enumerate(A_PATTERNS):
        nkey = win // dil
        buf = past['a'][g]
        n_buf = buf.shape[2]
        cs = lambda part: slice(part * A_WIDTH + g * LANES, part * A_WIDTH + (g + 1) * LANES)
        q = pad_rows(a3[:, :, cs(0)], rows)
        kv_new = jnp.concatenate([a3[:, :, cs(1)], a3[:, :, cs(2)]], axis=2)
        bias = _dil_step_bias(prm['t5_table'], g, dil, nkey, n_buf, n_new, rows, LANES)
        buf_t = buf.transpose(0, 1, 3, 4, 5, 2).reshape(buf.shape[0], nseq, 2, LANES, n_buf)
        new_t = jnp.pad(kv_new.reshape(nseq, n_new, 2, LANES).transpose(0, 2, 3, 1),
                        ((0, 0), (0, 0), (0, 0), (0, LANES - n_new)))
        o, lse = dil_step(q, buf_t, new_t, bias, j)
        os_.append(o[:, :n_new].reshape(nseq * n_new, LANES))
        ls.append(lse[:, :n_new].reshape(nseq * n_new, LANES))
        kv_all = jnp.concatenate([buf[j].reshape(nseq, n_buf, 2 * LANES), kv_new], axis=1)
        bufs.append(kv_all[:, n_new:].reshape(nseq, n_buf, 2, A_GROUP_HEADS, HEAD_DIM))
    chunk = B_CHUNK if n_new % B_CHUNK == 0 else n_new
    assert chunk == n_new
    cpad = 16
    cosv, sinv = _rope_tables(pos0 + jnp.arange(cpad))
    padc = lambda x: jnp.pad(x.reshape(nseq, n_new, -1), ((0, 0), (0, cpad - n_new), (0, 0))).reshape(nseq * cpad, -1)
    b_out, s_new = retention(padc(bqk), cosv, sinv, padc(bv), padc(bg), _retention_tables(chunk, cpad),
                             prm['ab_gn_gain'][j], past['b'][j].astype(F32), cpad, 1)
    b_out = b_out.reshape(nseq, cpad, -1)[:, :n_new].reshape(nseq * n_new, -1)
    h = ab_out(h, os_, ls, b_out, prm['ab_w_out'][j], h.shape[0])
    return h, bufs, s_new


def _cd_project(h, prm, j, tm):
    return norm_matmul(h, prm['norm_mix'][2 * j + 1], prm['cd_w'][j], CD_SEGS, tm)


def _layer_cd_prompt(h, prm, j, tm):
    seq = h.shape[0]
    cu, dq, kv, iq, misc = _cd_project(h, prm, j, tm)
    x0 = jnp.zeros((1, C_FLAT), F32)
    c_out, xr, xi = s5(cu, prm['s5'][j], x0, x0, 1, min(256, seq))
    sc = jnp.stack([xr.reshape(C_GROUPS, C_STATE), xi.reshape(C_GROUPS, C_STATE)], axis=-1)[None]
    n_sel = min(IDX_TOPK, seq // 4)
    heads_first = lambda x: x.reshape(seq, D_HEADS, HEAD_DIM).transpose(1, 0, 2)
    kb = kv[:, :D_WIDTH].astype(BF16).T
    vb = kv[:, D_WIDTH:].astype(BF16).T
    ikt = misc[:, :IDX_DIM].astype(BF16).T
    bias_tiles = dsa_bias_tiles(_dsa_prompt_table(prm['t5_table']))
    d_out = dsa_prompt(heads_first(dq), iq, misc, ikt, kb, vb, bias_tiles, _tri(DSA_TRI), n_sel)
    h = cd_out(h, c_out, d_out, prm['cd_w_out'][j], tm)
    return h, sc, kv, misc[:, :IDX_DIM]


def _layer_cd_sample(h, prm, j, past, nseq, n_new):
    rows = 8
    cu, dq, kv, iq, misc = _cd_project(h, prm, j, h.shape[0])
    tmaj = lambda x: x.reshape(nseq, n_new, -1).transpose(1, 0, 2).reshape(nseq * n_new, -1)
    x0 = past['c'][j].astype(F32)
    c_out, xr, xi = s5(tmaj(cu), prm['s5'][j], x0[..., 0].reshape(nseq, C_FLAT), x0[..., 1].reshape(nseq, C_FLAT),
                       nseq, n_new)
    c_out = c_out.reshape(n_new, nseq, -1).transpose(1, 0, 2).reshape(nseq * n_new, -1)
    sc = jnp.stack([xr.reshape(nseq, C_GROUPS, C_STATE), xi.reshape(nseq, C_GROUPS, C_STATE)], axis=-1)

    page_table = past['page_table']
    cache_kv, cache_kidx = past['d_kv'], past['d_kidx']
    npage = page_table.shape[1]
    psz = cache_kv.shape[2]
    n_past = npage * psz
    n_sel = min(IDX_TOPK, (n_past + n_new) // 4)
    pad_rows = lambda x, r: jnp.pad(x, ((0, 0), (0, r - x.shape[1]), (0, 0)))
    ik_new = misc[:, :IDX_DIM]
    iqm = pad_rows(iq.reshape(nseq, n_new, IDX_HEADS * IDX_DIM), rows).reshape(nseq, rows, IDX_HEADS, IDX_DIM)
    iqm = iqm.transpose(0, 2, 1, 3).reshape(nseq, IDX_HEADS * rows, IDX_DIM)
    iwm = pad_rows(misc[:, IDX_DIM:IDX_DIM + IDX_HEADS].reshape(nseq, n_new, IDX_HEADS), rows)
    iwm = jnp.broadcast_to(iwm.transpose(0, 2, 1).reshape(nseq, IDX_HEADS * rows, 1), (nseq, IDX_HEADS * rows, LANES))
    kidx_t = cache_kidx.transpose(0, 1, 3, 2)
    kv_t = cache_kv.transpose(0, 1, 3, 4, 5, 2).reshape(cache_kv.shape[0], cache_kv.shape[1], 2, D_WIDTH, psz)
    lane_pad = lambda x: jnp.pad(x, ((0, 0),) * (x.ndim - 1) + ((0, psz - x.shape[-1]),))
    ik_new_t = lane_pad(ik_new.reshape(nseq, n_new, IDX_DIM).transpose(0, 2, 1))
    keys = dsa_step_keys(page_table, iqm, iwm, kidx_t, ik_new_t, j, n_new, math.gcd(npage, 32))
    ncol = keys.shape[2]
    neg = topk_mask_rows(keys.reshape(nseq * rows, ncol), _tri(LANES),
                         ncol // 3 if (ncol // LANES) % 3 == 0 else LANES, n_sel, min(LANES, nseq * rows))
    qpos = n_past + jnp.minimum(jnp.arange(rows), n_new - 1)
    dist = qpos[:, None] - jnp.arange(ncol)[None, :]
    bias = _bias_by_distance(prm['t5_table'], dist)[..., :D_HEADS] * LOG2E
    bias = bias.transpose(2, 0, 1).reshape(D_HEADS * rows, ncol)
    q = pad_rows(dq.reshape(nseq, n_new, D_WIDTH), rows)
    col_head = jnp.arange(D_WIDTH) // HEAD_DIM
    qbd = jnp.where(col_head[None, None, None, :] == jnp.arange(D_HEADS)[None, :, None, None], q[:, None], 0)
    qbd = qbd.reshape(nseq, D_HEADS * rows, D_WIDTH).astype(dq.dtype)
    kvt_new = lane_pad(kv.reshape(nseq, n_new, 2, D_WIDTH).transpose(0, 2, 3, 1))
    d_out = dsa_step_attn(page_table, qbd, neg.reshape(nseq, rows, ncol), bias, kv_t, kvt_new, j, math.gcd(npage, 16))
    d_out = d_out[:, :n_new].reshape(nseq * n_new, D_WIDTH)
    h = cd_out(h, c_out, d_out, prm['cd_w_out'][j], h.shape[0])
    return h, sc, kv, ik_new


def _trunk(x, p, prm, past, nseq, n_new, pos0, tm):
    depth = p.shape[0]
    h = x
    a_new = [[] for _ in A_PATTERNS]
    b_new, c_new, kv_new, kidx_new = [], [], [], []
    for i in range(depth):
        j = i // 2
        if i % 2 == 0:
            if past is None:
                h, bufs, sb = _layer_ab_prompt(h, prm, j, tm)
            else:
                h, bufs, sb = _layer_ab_sample(h, prm, j, past, nseq, n_new, pos0)
            for g in range(len(A_PATTERNS)):
                a_new[g].append(bufs[g])
            b_new.append(sb)
        else:
            if past is None:
                h, sc, kv, kidx = _layer_cd_prompt(h, prm, j, tm)
            else:
                h, sc, kv, kidx = _layer_cd_sample(h, prm, j, past, nseq, n_new)
            c_new.append(sc)
            kv_new.append(kv.reshape(nseq, n_new, 2, D_HEADS, HEAD_DIM))
            kidx_new.append(kidx.reshape(nseq, n_new, IDX_DIM))
        h = _ffn_and_ple(h, p[i], i, prm, tm, i == depth - 1)
    stack = lambda t: jnp.stack(t, axis=0)
    return h, tuple(stack(t) for t in a_new), stack(b_new), stack(c_new), stack(kv_new), stack(kidx_new)


def kernel(x_prompt, x_sample, p_prompt, p_sample, cache_a_kv0, cache_a_kv1, cache_a_kv2, state_b, state_c, cache_d_kv, cache_d_kidx, page_table, t5_table, norm_mix, norm_ffn, norm_ple, norm_final, ab_w_in, ab_w_out, ab_gn_gain, cd_w_in, cd_w_out, c_lambda_re, c_lambda_im, c_log_dt, c_b_re, c_b_im, c_c_re, c_c_im, c_d, c_w_glu, c_b_glu, moe_w_group, moe_b_group, moe_w_expert, moe_b_expert, moe_w_gate, moe_w_up, moe_w_down, ple_w_gate, ple_w_proj):
    depth, d_model = norm_mix.shape
    n_ab, n_cd = ab_w_in.shape[0], cd_w_in.shape[0]
    rpad = jnp.zeros((depth, d_model, LANES - MOE_GROUPS - MOE_EXPERTS), F32)
    prm = dict(
        t5_table=t5_table, norm_mix=norm_mix, norm_ffn=norm_ffn, norm_ple=norm_ple, norm_final=norm_final,
        ab_w=[_prep_ab_w(ab_w_in[j]) for j in range(n_ab)], ab_w_out=ab_w_out.astype(BF16), ab_gn_gain=ab_gn_gain,
        cd_w=[_prep_cd_w(cd_w_in[j]) for j in range(n_cd)], cd_w_out=cd_w_out.astype(BF16),
        s5=[_s5_prep(c_lambda_re[j], c_lambda_im[j], c_log_dt[j], c_b_re[j], c_b_im[j], c_c_re[j], c_c_im[j],
                     c_d[j], c_w_glu[j], c_b_glu[j]) for j in range(n_cd)],
        moe_wr=jnp.concatenate([moe_w_group, moe_w_expert.reshape(depth, d_model, MOE_EXPERTS), rpad],
                               axis=2).astype(BF16),
        moe_br=jnp.concatenate([moe_b_group, moe_b_expert.reshape(depth, MOE_EXPERTS),
                                jnp.zeros((depth, LANES - MOE_GROUPS - MOE_EXPERTS), F32)], axis=1)[:, None, :],
        moe_wg=moe_w_gate.astype(BF16), moe_wu=moe_w_up.astype(BF16), moe_wd=moe_w_down.astype(BF16),
        ple_wg=ple_w_gate.astype(BF16), ple_wp=ple_w_proj.astype(BF16))

    bsz, seq, _ = x_prompt.shape
    assert bsz == 1
    y_p, a_p, sb_p, sc_p, dkv_p, dki_p = _trunk(x_prompt[0], p_prompt[:, 0], prm, None, 1, seq, 0, min(256, seq))
    y_p = y_p[None]

    nseq, n_new, _ = x_sample.shape
    past = dict(a=(cache_a_kv0, cache_a_kv1, cache_a_kv2), b=state_b, c=state_c, d_kv=cache_d_kv, d_kidx=cache_d_kidx,
                page_table=page_table)
    n_past = page_table.shape[1] * cache_d_kv.shape[2]
    rows_s = nseq * n_new
    y_s, a_s, sb_s, sc_s, dkv_s, dki_s = _trunk(x_sample.reshape(rows_s, d_model),
                                                p_sample.reshape(depth, rows_s, -1), prm, past, nseq, n_new, n_past,
                                                rows_s)
    y_s = y_s.reshape(nseq, n_new, d_model)
    return (y_p, y_s, a_p[0], a_s[0], a_p[1], a_s[1], a_p[2], a_s[2], sb_p, sb_s, sc_p, sc_s, dkv_p, dkv_s, dki_p, dki_s)
```

```python
import functools
import math

import numpy as np
import jax
import jax.numpy as jnp
from jax import lax
from jax.experimental import pallas as pl
from jax.experimental.pallas import tpu as pltpu

F32 = jnp.float32
BF16 = jnp.bfloat16
I32 = jnp.int32

EPS = 1e-6
HEAD_DIM = 64
A_PATTERNS = ((128, 1), (512, 4), (2048, 16))
A_GROUP_HEADS = 2
A_WIDTH = 384
B_HEADS = 5
B_KEY_DIM = 64
B_VAL_DIM = 128
B_QK_WIDTH = 320
B_QK_PAD = 384
B_V_WIDTH = 640
B_CHUNK = 128
ROPE_BASE = 10000.0
C_GROUPS = 32
C_GROUP_SIZE = 16
C_WIDTH = 512
C_STATE = 64
C_FLAT = C_GROUPS * C_STATE
C_SUPER = 8
D_HEADS = 8
D_WIDTH = 512
IDX_HEADS = 4
IDX_DIM = 64
IDX_TOPK = 256
IDX_SCALE = (IDX_HEADS * IDX_DIM) ** -0.5
T5_BUCKETS = 32
T5_MAX_DIST = 2048
MOE_GROUPS = 4
MOE_GROUP_EXPERTS = 4
MOE_EXPERTS = 16
D_EXPERT = 512
LANES = 128
INT_MIN = -(2 ** 31)
NEG_INF = float("-inf")
LOG2E = math.log2(math.e)

VMEM_LIMIT_BYTES = 52 * 1024 * 1024

DSA_TQ = 256
DSA_TK = 1024
DSA_BQ = 128
DSA_BK = 512
DSA_DCAP = T5_MAX_DIST + DSA_BK
DSA_TAB = DSA_DCAP + DSA_BK + 2 * LANES
DSA_TRI = 256
DSA_RC = 32
KTH_ROWS = 128
MOE_EXPERTS_PER_STEP = 2
DIL_BLOCKS_PER_STEP = 8


def _cparams(*sem):
    return pltpu.CompilerParams(dimension_semantics=sem, vmem_limit_bytes=VMEM_LIMIT_BYTES)


def _rms(x, g):
    return x * lax.rsqrt(jnp.mean(x * x, axis=-1, keepdims=True) + EPS) * g


def _dot(a, b):
    return jnp.dot(a, b, preferred_element_type=F32)


def _dot_nt(a, b):
    return lax.dot_general(a, b, (((1,), (1,)), ((), ())), preferred_element_type=F32)


def _dot_tn(a, b):
    return lax.dot_general(a, b, (((0,), (0,)), ((), ())), preferred_element_type=F32)


def _sigmoid(x):
    return 1.0 / (1.0 + jnp.exp(-x))


def _t5_bucket(dist):
    max_exact = T5_BUCKETS // 2
    d = jnp.maximum(dist, 0)
    scaled = jnp.log(jnp.maximum(d, 1).astype(F32) / max_exact) / math.log(T5_MAX_DIST / max_exact)
    large = jnp.minimum(max_exact + (scaled * (T5_BUCKETS - max_exact)).astype(I32), T5_BUCKETS - 1)
    return jnp.where(d < max_exact, d, large)


def _norm_matmul_kernel(x_ref, g_ref, w_ref, *out_refs, segs):
    xn = _rms(x_ref[...], g_ref[...]).astype(BF16)
    for o_ref, (start, width) in zip(out_refs, segs):
        o_ref[...] = _dot(xn, w_ref[:, start:start + width]).astype(o_ref.dtype)


def norm_matmul(x, gain, w, segs, tm):
    t, d = x.shape
    n = w.shape[1]
    kern = functools.partial(_norm_matmul_kernel, segs=tuple((s, wd) for s, wd, _ in segs))
    return pl.pallas_call(
        kern, grid=(t // tm,),
        in_specs=[pl.BlockSpec((tm, d), lambda i: (i, 0)),
                  pl.BlockSpec((1, d), lambda i: (0, 0)),
                  pl.BlockSpec((d, n), lambda i: (0, 0))],
        out_specs=[pl.BlockSpec((tm, wd), lambda i: (i, 0)) for _, wd, _ in segs],
        out_shape=[jax.ShapeDtypeStruct((t, wd), dt) for _, wd, dt in segs],
        compiler_params=_cparams("parallel"), name="norm_matmul")(x, gain.reshape(1, d), w)


def _dil_prompt_kernel(q_ref, kc_ref, kp_ref, vc_ref, vp_ref, bias_ref, o_ref, lse_ref):
    n = pl.program_id(1)
    blk = kp_ref.shape[0]
    nblk = q_ref.shape[0] // blk
    qi = lax.broadcasted_iota(I32, (blk, 2 * blk), 0)
    kj = lax.broadcasted_iota(I32, (blk, 2 * blk), 1)
    m = blk + qi - kj
    band = (m >= 0) & (m <= blk)
    for i in range(nblk):
        rows = slice(i * blk, (i + 1) * blk)
        q = q_ref[rows, :]
        if i == 0:
            k_prev, v_prev = kp_ref[...], vp_ref[...]
            mask = band & (kj >= jnp.where(n > 0, 0, blk))
        else:
            prev_rows = slice((i - 1) * blk, i * blk)
            k_prev, v_prev = kc_ref[prev_rows, :], vc_ref[prev_rows, :]
            mask = band
        kw = jnp.concatenate([k_prev, kc_ref[rows, :]], axis=0)
        vw = jnp.concatenate([v_prev, vc_ref[rows, :]], axis=0)
        outs, lses = [], []
        for h in range(A_GROUP_HEADS):
            cs = slice(h * HEAD_DIM, (h + 1) * HEAD_DIM)
            s = _dot_nt(q[:, cs].astype(BF16), kw[:, cs].astype(BF16)) * (HEAD_DIM ** -0.5) + bias_ref[h]
            s = jnp.where(mask, s, NEG_INF)
            mx = jnp.max(s, axis=1, keepdims=True)
            p = jnp.exp(s - mx)
            l = jnp.sum(p, axis=1, keepdims=True)
            outs.append(_dot((p / l).astype(BF16), vw[:, cs].astype(BF16)))
            lses.append(jnp.broadcast_to(mx + jnp.log(l), (blk, HEAD_DIM)))
        o_ref[rows, :] = jnp.concatenate(outs, axis=1)
        lse_ref[rows, :] = jnp.concatenate(lses, axis=1)


def dil_prompt(aqkv, bias_qk, g, dil):
    seq = aqkv.shape[0]
    blk = bias_qk.shape[1]
    nb = seq // (blk * dil)
    per = math.gcd(nb, DIL_BLOCKS_PER_STEP)
    ncol = aqkv.shape[1] // LANES
    a = aqkv.reshape(seq // dil, dil * aqkv.shape[1])
    ng = len(A_PATTERNS)

    def cur(off):
        return pl.BlockSpec((per * blk, LANES), lambda r, n: (n, r * ncol + off + g))

    def prev(off):
        return pl.BlockSpec((blk, LANES), lambda r, n: (jnp.maximum(n * per - 1, 0), r * ncol + off + g))

    o, lse = pl.pallas_call(
        _dil_prompt_kernel, grid=(dil, nb // per),
        in_specs=[cur(0), cur(ng), prev(ng), cur(2 * ng), prev(2 * ng),
                  pl.BlockSpec(bias_qk.shape, lambda r, n: (0, 0, 0))],
        out_specs=[pl.BlockSpec((per * blk, LANES), lambda r, n: (n, r))] * 2,
        out_shape=[jax.ShapeDtypeStruct((seq // dil, dil * LANES), F32)] * 2,
        compiler_params=_cparams("parallel", "parallel"), name="dil_prompt")(a, a, a, a, a, bias_qk)
    return o.reshape(seq, LANES), lse.reshape(seq, LANES)


def _dil_step_kernel(q_ref, buf_ref, new_ref, bias_ref, o_ref, lse_ref):
    q = q_ref[0]
    rows = q.shape[0]
    head = lax.broadcasted_iota(I32, q.shape, 1) // HEAD_DIM
    qbd = jnp.concatenate([jnp.where(head == h, q, 0.0) for h in range(A_GROUP_HEADS)], axis=0).astype(BF16)
    kt = jnp.concatenate([buf_ref[0, 0, 0], new_ref[0, 0]], axis=1).astype(BF16)
    vt = jnp.concatenate([buf_ref[0, 0, 1], new_ref[0, 1]], axis=1).astype(BF16)
    s = _dot(qbd, kt) * (HEAD_DIM ** -0.5) + bias_ref[...]
    mx = jnp.max(s, axis=1, keepdims=True)
    p = jnp.exp(s - mx)
    l = jnp.sum(p, axis=1, keepdims=True)
    o = _dot_nt((p / l).astype(BF16), vt)
    lse = jnp.broadcast_to(mx + jnp.log(l), o.shape)
    o_ref[0] = jnp.where(head == 0, o[:rows], o[rows:])
    lse_ref[0] = jnp.where(head == 0, lse[:rows], lse[rows:])


def dil_step(q, buf, new, bias, layer):
    nseq, rows, _ = q.shape
    win = buf.shape[4]
    return pl.pallas_call(
        _dil_step_kernel, grid=(nseq,),
        in_specs=[pl.BlockSpec((1, rows, LANES), lambda n: (n, 0, 0)),
                  pl.BlockSpec((1, 1, 2, LANES, win), lambda n: (layer, n, 0, 0, 0)),
                  pl.BlockSpec((1,) + new.shape[1:], lambda n: (n, 0, 0, 0)),
                  pl.BlockSpec(bias.shape, lambda n: (0, 0))],
        out_specs=[pl.BlockSpec((1, rows, LANES), lambda n: (n, 0, 0))] * 2,
        out_shape=[jax.ShapeDtypeStruct((nseq, rows, LANES), F32)] * 2,
        compiler_params=_cparams("parallel"), name="dil_step")(q, buf, new, bias)


def _retention_kernel(bqk_ref, cos_ref, sin_ref, bv_ref, bg_ref, decay_ref, win_ref, wend_ref, gch_ref, gain_ref,
                      s0_ref, out_ref, s_ref):
    c = pl.program_id(1)

    @pl.when(c == 0)
    def _():
        s_ref[...] = s0_ref[...]

    bqk = bqk_ref[...]
    cosv, sinv = cos_ref[...], sin_ref[...]
    w = B_QK_PAD
    q = (bqk[:, 0:w] * cosv + bqk[:, w:2 * w] * sinv) * (B_KEY_DIM ** -0.5)
    k = bqk[:, 2 * w:3 * w] * cosv + bqk[:, 3 * w:4 * w] * sinv
    q_in = q * win_ref[...]
    w_end = wend_ref[...]
    bv, bg, gain = bv_ref[...], bg_ref[...], gain_ref[...]
    outs = []
    for h in range(B_HEADS):
        ks = slice(h * B_KEY_DIM, (h + 1) * B_KEY_DIM)
        vs = slice(h * B_VAL_DIM, (h + 1) * B_VAL_DIM)
        vh = bv[:, vs].astype(BF16)
        state = s_ref[0, h]
        intra = _dot_nt(q[:, ks].astype(BF16), k[:, ks].astype(BF16)) * decay_ref[h]
        o = _dot(intra.astype(BF16), vh) + _dot(q_in[:, ks].astype(BF16), state.astype(BF16))
        v_end = (bv[:, vs] * w_end[:, h * B_KEY_DIM:h * B_KEY_DIM + 1]).astype(BF16)
        s_ref[0, h] = gch_ref[h] * state + _dot_tn(k[:, ks].astype(BF16), v_end)
        mu = jnp.mean(o, axis=1, keepdims=True)
        var = jnp.mean((o - mu) ** 2, axis=1, keepdims=True)
        on = (o - mu) * lax.rsqrt(var + EPS) * gain[:, vs]
        g = bg[:, vs]
        outs.append(g * _sigmoid(g) * on)
    out_ref[...] = jnp.concatenate(outs, axis=1)


def retention(bqk, cosv, sinv, bv, bg, tabs, gain, s0, chunk, nchunk):
    nseq = s0.shape[0]
    rows = bqk.shape[0]
    decay, w_in, w_end, g_chunk = tabs
    row = lambda b, c: (b * nchunk + c, 0)
    pos = lambda b, c: (c, 0)
    fix2 = lambda b, c: (0, 0)
    fix3 = lambda b, c: (0, 0, 0)
    return pl.pallas_call(
        _retention_kernel, grid=(nseq, nchunk),
        in_specs=[pl.BlockSpec((chunk, 4 * B_QK_PAD), row),
                  pl.BlockSpec((chunk, B_QK_PAD), pos), pl.BlockSpec((chunk, B_QK_PAD), pos),
                  pl.BlockSpec((chunk, B_V_WIDTH), row), pl.BlockSpec((chunk, B_V_WIDTH), row),
                  pl.BlockSpec(decay.shape, fix3), pl.BlockSpec(w_in.shape, fix2), pl.BlockSpec(w_end.shape, fix2),
                  pl.BlockSpec(g_chunk.shape, fix3), pl.BlockSpec((1, B_V_WIDTH), fix2),
                  pl.BlockSpec((1,) + s0.shape[1:], lambda b, c: (b, 0, 0, 0))],
        out_specs=[pl.BlockSpec((chunk, B_V_WIDTH), row),
                   pl.BlockSpec((1,) + s0.shape[1:], lambda b, c: (b, 0, 0, 0))],
        out_shape=[jax.ShapeDtypeStruct((rows, B_V_WIDTH), F32), jax.ShapeDtypeStruct(s0.shape, F32)],
        compiler_params=_cparams("parallel", "arbitrary"), name="retention")(
            bqk, cosv, sinv, bv, bg, decay, w_in, w_end, g_chunk, gain.reshape(1, B_V_WIDTH), s0)


def _retention_tables(chunk, pad):
    lg = np.log(1.0 - 2.0 ** (-5.0 - np.arange(B_HEADS, dtype=np.float32))).astype(np.float32)
    lg = jnp.asarray(lg)
    i = jnp.arange(pad, dtype=F32)
    diff = i[:, None] - i[None, :]
    decay = jnp.where(diff >= 0, jnp.exp(jnp.maximum(diff, 0.0)[None] * lg[:, None, None]), 0.0)
    w_end = jnp.exp((chunk - 1.0 - i)[None, :] * lg[:, None])
    w_in = jnp.exp((i + 1.0)[None, :] * lg[:, None])
    g_chunk = jnp.exp(chunk * lg)

    def widen(t):
        t = jnp.repeat(t.T, B_KEY_DIM, axis=1)
        return jnp.pad(t, ((0, 0), (0, B_QK_PAD - B_QK_WIDTH)))

    return decay, widen(w_in), widen(w_end), jnp.broadcast_to(g_chunk[:, None, None], (B_HEADS, 1, B_VAL_DIM))


def _rope_tables(pos):
    half = B_KEY_DIM // 2
    freq = ROPE_BASE ** (-jnp.arange(half, dtype=F32) / half)
    ang = pos.astype(F32)[:, None] * freq[None, :]
    cos, sin = jnp.cos(ang), jnp.sin(ang)
    cos_h = jnp.concatenate([cos, cos], axis=1)
    sin_h = jnp.concatenate([-sin, sin], axis=1)
    padw = ((0, 0), (0, B_QK_PAD - B_QK_WIDTH))
    return jnp.pad(jnp.tile(cos_h, (1, B_HEADS)), padw), jnp.pad(jnp.tile(sin_h, (1, B_HEADS)), padw)


def _ab_out_kernel(h_ref, o0_ref, o1_ref, o2_ref, l0_ref, l1_ref, l2_ref, b_ref, w_ref, out_ref):
    ls = [l0_ref[...], l1_ref[...], l2_ref[...]]
    os_ = [o0_ref[...], o1_ref[...], o2_ref[...]]
    mx = jnp.maximum(jnp.maximum(ls[0], ls[1]), ls[2])
    es = [jnp.exp(l - mx) for l in ls]
    den = es[0] + es[1] + es[2]
    acc = h_ref[...]
    for g in range(3):
        acc = acc + _dot((os_[g] * (es[g] / den)).astype(BF16), w_ref[g * LANES:(g + 1) * LANES, :])
    acc = acc + _dot(b_ref[...].astype(BF16), w_ref[A_WIDTH:, :])
    out_ref[...] = acc


def ab_out(h, os_, ls, b_out, w, tm):
    t, d = h.shape
    row = lambda i: (i, 0)
    small = pl.BlockSpec((tm, LANES), row)
    return pl.pallas_call(
        _ab_out_kernel, grid=(t // tm,),
        in_specs=[pl.BlockSpec((tm, d), row)] + [small] * 6 +
                 [pl.BlockSpec((tm, B_V_WIDTH), row), pl.BlockSpec(w.shape, lambda i: (0, 0))],
        out_specs=pl.BlockSpec((tm, d), row),
        out_shape=jax.ShapeDtypeStruct((t, d), F32),
        compiler_params=_cparams("parallel"), name="ab_out")(h, *os_, *ls, b_out, w)


def _cd_out_kernel(h_ref, c_ref, d_ref, w_ref, out_ref):
    acc = h_ref[...] + _dot(c_ref[...].astype(BF16), w_ref[:C_WIDTH, :])
    out_ref[...] = acc + _dot(d_ref[...].astype(BF16), w_ref[C_WIDTH:, :])


def cd_out(h, c_out, d_out, w, tm):
    t, d = h.shape
    row = lambda i: (i, 0)
    return pl.pallas_call(
        _cd_out_kernel, grid=(t // tm,),
        in_specs=[pl.BlockSpec((tm, d), row), pl.BlockSpec((tm, C_WIDTH), row), pl.BlockSpec((tm, D_WIDTH), row),
                  pl.BlockSpec(w.shape, lambda i: (0, 0))],
        out_specs=pl.BlockSpec((tm, d), row),
        out_shape=jax.ShapeDtypeStruct((t, d), F32),
        compiler_params=_cparams("parallel"), name="cd_out")(h, c_out, d_out, w)


def _route(logits):
    lane = lax.broadcasted_iota(I32, logits.shape, 1).astype(F32)
    gmask = lane < MOE_GROUPS
    gl = jnp.where(gmask, logits, NEG_INF)
    gmax = jnp.max(gl, axis=1, keepdims=True)
    gsum = jnp.sum(jnp.where(gmask, jnp.exp(gl - gmax), 0.0), axis=1, keepdims=True)
    g_w = 1.0 / gsum
    g_idx = jnp.min(jnp.where(gl == gmax, lane, LANES), axis=1, keepdims=True)
    lo = MOE_GROUPS + MOE_GROUP_EXPERTS * g_idx
    emask = (lane >= lo) & (lane < lo + MOE_GROUP_EXPERTS)
    el = jnp.where(emask, logits, NEG_INF)
    emax = jnp.max(el, axis=1, keepdims=True)
    eexp = jnp.where(emask, jnp.exp(el - emax), 0.0)
    p = eexp / jnp.sum(eexp, axis=1, keepdims=True)
    pm = jnp.where(emask, p, -1.0)
    p1 = jnp.max(pm, axis=1, keepdims=True)
    i1 = jnp.min(jnp.where(pm == p1, lane, LANES), axis=1, keepdims=True)
    pm2 = jnp.where(lane == i1, -1.0, pm)
    p2 = jnp.max(pm2, axis=1, keepdims=True)
    i2 = jnp.min(jnp.where(pm2 == p2, lane, LANES), axis=1, keepdims=True)
    den = p1 + p2
    return jnp.where(lane == i1, g_w * (p1 / den), jnp.where(lane == i2, g_w * (p2 / den), 0.0))


def _moe_kernel(h_ref, g_ref, wr_ref, br_ref, wg_ref, wu_ref, wd_ref, out_ref, xn_s, comb_s, acc_s):
    e = pl.program_id(1)

    @pl.when(e == 0)
    def _():
        xn = _rms(h_ref[...], g_ref[...]).astype(BF16)
        xn_s[...] = xn
        comb_s[...] = _route(_dot(xn, wr_ref[...]) + br_ref[...])
        acc_s[...] = jnp.zeros_like(acc_s)

    xb = xn_s[...]
    comb = comb_s[...]
    lane = lax.broadcasted_iota(I32, comb.shape, 1)
    acc = acc_s[...]
    for j in range(MOE_EXPERTS_PER_STEP):
        hg = _dot(xb, wg_ref[j])
        hu = _dot(xb, wu_ref[j])
        hdn = (hg * _sigmoid(hg)) * hu
        y = _dot(hdn.astype(BF16), wd_ref[j])
        expert_lane = MOE_GROUPS + e * MOE_EXPERTS_PER_STEP + j
        cw = jnp.sum(jnp.where(lane == expert_lane, comb, 0.0), axis=1, keepdims=True)
        acc = acc + cw * y
    acc_s[...] = acc

    @pl.when(e == pl.num_programs(1) - 1)
    def _():
        out_ref[...] = h_ref[...] + acc_s[...]


def moe(h, gain, wr, br, wg, wu, wd, tm):
    t, d = h.shape
    row = lambda i, e: (i, 0)
    fix = lambda i, e: (0, 0)
    ex = lambda i, e: (e, 0, 0)
    per = MOE_EXPERTS_PER_STEP
    return pl.pallas_call(
        _moe_kernel, grid=(t // tm, MOE_EXPERTS // per),
        in_specs=[pl.BlockSpec((tm, d), row), pl.BlockSpec((1, d), fix),
                  pl.BlockSpec((d, LANES), fix), pl.BlockSpec((1, LANES), fix),
                  pl.BlockSpec((per, d, D_EXPERT), ex), pl.BlockSpec((per, d, D_EXPERT), ex),
                  pl.BlockSpec((per, D_EXPERT, d), ex)],
        out_specs=pl.BlockSpec((tm, d), row),
        out_shape=jax.ShapeDtypeStruct((t, d), F32),
        scratch_shapes=[pltpu.VMEM((tm, d), BF16), pltpu.VMEM((tm, LANES), F32), pltpu.VMEM((tm, d), F32)],
        compiler_params=_cparams("parallel", "arbitrary"), name="moe")(
            h, gain.reshape(1, d), wr, br, wg, wu, wd)


def _ple_kernel(h_ref, p_ref, g_ref, wg_ref, wp_ref, *rest, final):
    h = h_ref[...]
    gate = _sigmoid(_dot(_rms(h, g_ref[...]).astype(BF16), wg_ref[...]))
    hn = h + gate * _dot(p_ref[...].astype(BF16), wp_ref[...])
    if final:
        gf_ref, out_ref = rest
        out_ref[...] = _rms(hn, gf_ref[...])
    else:
        rest[0][...] = hn


def ple(h, p, gain, wg, wp, gain_final, tm):
    t, d = h.shape
    row = lambda i: (i, 0)
    fix = lambda i: (0, 0)
    final = gain_final is not None
    in_specs = [pl.BlockSpec((tm, d), row), pl.BlockSpec((tm, p.shape[1]), row), pl.BlockSpec((1, d), fix),
                pl.BlockSpec(wg.shape, fix), pl.BlockSpec(wp.shape, fix)]
    args = [h, p, gain.reshape(1, d), wg, wp]
    if final:
        in_specs.append(pl.BlockSpec((1, d), fix))
        args.append(gain_final.reshape(1, d))
    return pl.pallas_call(
        functools.partial(_ple_kernel, final=final), grid=(t // tm,),
        in_specs=in_specs, out_specs=pl.BlockSpec((tm, d), row),
        out_shape=jax.ShapeDtypeStruct((t, d), F32),
        compiler_params=_cparams("parallel"), name="ple")(*args)


def _s5_kernel(u_ref, a_ref, bre_ref, bim_ref, cre_ref, cim_ref, d_ref, wglu_ref, bglu_ref, x0r_ref, x0i_ref,
               out_ref, xr_ref, xi_ref, bur_s, bui_s, *, nbatch, steps):
    c = pl.program_id(0)

    @pl.when(c == 0)
    def _():
        xr_ref[...] = x0r_ref[...]
        xi_ref[...] = x0i_ref[...]

    u = u_ref[...]
    ub = u.astype(BF16)
    nsb = bre_ref.shape[0]
    wi, ws = C_WIDTH // nsb, C_FLAT // nsb
    for j in range(nsb):
        ubj = ub[:, j * wi:(j + 1) * wi]
        bur_s[:, j * ws:(j + 1) * ws] = _dot(ubj, bre_ref[j])
        bui_s[:, j * ws:(j + 1) * ws] = _dot(ubj, bim_ref[j])
    ar = a_ref[0:1, :]
    ai = a_ref[1:2, :]

    def step(t, carry):
        xr, xi = carry
        r0 = pl.multiple_of(t * nbatch, nbatch)
        nr = ar * xr - ai * xi + bur_s[pl.ds(r0, nbatch), :]
        ni = ar * xi + ai * xr + bui_s[pl.ds(r0, nbatch), :]
        bur_s[pl.ds(r0, nbatch), :] = nr
        bui_s[pl.ds(r0, nbatch), :] = ni
        return nr, ni

    xr, xi = lax.fori_loop(0, steps, step, (xr_ref[...], xi_ref[...]), unroll=min(steps, 8))
    xr_ref[...] = xr
    xi_ref[...] = xi
    y = jnp.concatenate(
        [_dot(bur_s[:, j * ws:(j + 1) * ws].astype(BF16), cre_ref[j]) -
         _dot(bui_s[:, j * ws:(j + 1) * ws].astype(BF16), cim_ref[j]) for j in range(nsb)], axis=1) + d_ref[...] * u
    z = 0.5 * y * (1.0 + jnp.tanh(math.sqrt(2.0 / math.pi) * (y + 0.044715 * (y * y * y))))
    out_ref[...] = z * _sigmoid(_dot(z.astype(BF16), wglu_ref[...]) + bglu_ref[...])


def s5(u, prm, x0r, x0i, nbatch, steps):
    rows = u.shape[0]
    a, bre, bim, cre, cim, dskip, wglu, bglu = prm
    nchunk = rows // (nbatch * steps)
    tr = nbatch * steps
    full = lambda arr: pl.BlockSpec(arr.shape, lambda c: (0,) * arr.ndim)
    return pl.pallas_call(
        functools.partial(_s5_kernel, nbatch=nbatch, steps=steps), grid=(nchunk,),
        in_specs=[pl.BlockSpec((tr, C_WIDTH), lambda c: (c, 0))] +
                 [full(t) for t in (a, bre, bim, cre, cim, dskip, wglu, bglu, x0r, x0i)],
        out_specs=[pl.BlockSpec((tr, C_WIDTH), lambda c: (c, 0)), full(x0r), full(x0i)],
        out_shape=[jax.ShapeDtypeStruct((rows, C_WIDTH), F32), jax.ShapeDtypeStruct(x0r.shape, F32),
                   jax.ShapeDtypeStruct(x0i.shape, F32)],
        scratch_shapes=[pltpu.VMEM((tr, C_FLAT), F32), pltpu.VMEM((tr, C_FLAT), F32)],
        compiler_params=_cparams("arbitrary"), name="s5")(u, a, bre, bim, cre, cim, dskip, wglu, bglu, x0r, x0i)


def _s5_prep(lam_re, lam_im, log_dt, b_re, b_im, c_re, c_im, d_skip, w_glu, b_glu):
    dt = jnp.exp(log_dt)[:, None]
    mag = jnp.exp(lam_re * dt)
    ab_re, ab_im = mag * jnp.cos(lam_im * dt), mag * jnp.sin(lam_im * dt)
    den = lam_re * lam_re + lam_im * lam_im
    zr = ((ab_re - 1.0) * lam_re + ab_im * lam_im) / den
    zi = (ab_im * lam_re - (ab_re - 1.0) * lam_im) / den
    bb_re = zr[..., None] * b_re - zi[..., None] * b_im
    bb_im = zr[..., None] * b_im + zi[..., None] * b_re
    eye = jnp.eye(C_GROUPS, dtype=F32)
    nsb = C_GROUPS // C_SUPER
    wi, ws = C_WIDTH // nsb, C_FLAT // nsb

    def diag_blocks(m, rows, cols):
        return jnp.stack([m[j * rows:(j + 1) * rows, j * cols:(j + 1) * cols] for j in range(nsb)], axis=0)

    b_dense = lambda bb: diag_blocks(jnp.einsum('gnc,gh->gchn', bb, eye).reshape(C_WIDTH, C_FLAT), wi, ws).astype(BF16)
    c_dense = lambda cc: diag_blocks(jnp.einsum('gcn,gh->gnhc', cc, eye).reshape(C_FLAT, C_WIDTH), ws, wi).astype(BF16)
    a = jnp.stack([ab_re.reshape(C_FLAT), ab_im.reshape(C_FLAT)], axis=0)
    return (a, b_dense(bb_re), b_dense(bb_im), c_dense(c_re), c_dense(c_im), d_skip.reshape(1, C_WIDTH),
            w_glu.astype(BF16), b_glu.reshape(1, C_WIDTH))


def _sortable(x):
    bits = lax.bitcast_convert_type(x, I32)
    return bits ^ ((bits >> 31) & jnp.int32(0x7FFFFFFF))


def _kth_largest(keys_ref, ntiles, tw, n_sel):
    total_rows = keys_ref.shape[0]
    rows = min(total_rows, KTH_ROWS)
    thrs, rems = [], []
    for r0 in range(0, total_rows, rows):
        def count_ge(cand, r0=r0):
            def body(t, cv):
                off = pl.multiple_of(t * tw, LANES)
                kk = keys_ref[r0:r0 + rows, pl.ds(off, tw)]
                for j in range(tw // LANES):
                    cv = cv + jnp.where(kk[:, j * LANES:(j + 1) * LANES] >= cand, 1.0, 0.0)
                return cv
            cv = lax.fori_loop(0, ntiles, body, jnp.zeros((rows, LANES), F32))
            return jnp.sum(cv, axis=1, keepdims=True)

        def bit(i, carry, count_ge=count_ge):
            thr, c_fail = carry
            cand = thr + lax.shift_left(jnp.int32(1), 31 - i)
            c = count_ge(cand)
            ok = c >= n_sel
            return jnp.where(ok, cand, thr), jnp.where(ok, c_fail, c)

        thr, c_gt = lax.fori_loop(0, 32, bit, (jnp.full((rows, 1), INT_MIN, I32), jnp.zeros((rows, 1), F32)))
        thrs.append(thr)
        rems.append(jnp.where(thr == INT_MIN, 0.0, n_sel - c_gt))
    if len(thrs) == 1:
        return thrs[0], rems[0]
    return jnp.concatenate(thrs, axis=0), jnp.concatenate(rems, axis=0)


def _select(key, thr, rem, cnt, tri):
    eq = key == thr
    eqb = jnp.where(eq, 1.0, 0.0).astype(BF16)
    w = tri.shape[0]
    ranks = []
    for c in range(key.shape[1] // w):
        pc = _dot(eqb[:, c * w:(c + 1) * w], tri)
        ranks.append(cnt + pc)
        cnt = cnt + pc[:, -1:]
    rank = ranks[0] if len(ranks) == 1 else jnp.concatenate(ranks, axis=1)
    sel = (key > thr) | (eq & (rank <= rem))
    return sel, cnt


def _dsa_bias_kernel(tbr_ref, out_ref):
    tq, tk = DSA_BQ, DSA_BK
    start = pl.multiple_of(DSA_DCAP - pl.program_id(0) * LANES, LANES)
    wtab = tbr_ref[:, pl.ds(start, tk + 2 * LANES)]
    for h in range(D_HEADS):
        wb = jnp.broadcast_to(wtab[h:h + 1, :], (tq, tk + 2 * LANES))
        out_ref[0, h] = pltpu.roll(wb, 1, 1, stride=1, stride_axis=0)[:, LANES:LANES + tk]


def dsa_bias_tiles(tbr):
    nd = DSA_DCAP // LANES + 1
    return pl.pallas_call(
        _dsa_bias_kernel, grid=(nd,),
        in_specs=[pl.BlockSpec(tbr.shape, lambda d: (0, 0))],
        out_specs=pl.BlockSpec((1, D_HEADS, DSA_BQ, DSA_BK), lambda d: (d, 0, 0, 0)),
        out_shape=jax.ShapeDtypeStruct((nd, D_HEADS, DSA_BQ, DSA_BK), F32),
        compiler_params=_cparams("parallel"), name="dsa_bias_tiles")(tbr)


def _dsa_prompt_kernel(qb_ref, ktile_ref, q_ref, iq_ref, misc_ref, ikt_ref, kt_ref, vt_ref, *rest, n_sel):
    tq, tk = DSA_TQ, DSA_TK
    nqa, nkb = tq // DSA_BQ, tk // DSA_BK
    bias_refs = rest[:nqa * nkb]
    tri_ref, o_ref, keys_s, thr_s, rem_s, cnt_s, neg_s, m_s, l_s, acc_s, s_s, p_s, alpha_s = rest[nqa * nkb:]
    qb = qb_ref[pl.program_id(0)]
    kt = ktile_ref[pl.program_id(0)]
    kl = (qb * tq + tq - 1) // tk

    @pl.when(kt == 0)
    def _():
        iq = iq_ref[...]
        iw = misc_ref[:, IDX_DIM:IDX_DIM + IDX_HEADS]
        qpos = qb * tq + lax.broadcasted_iota(I32, (tq, tk), 0)

        def scores(t, carry):
            off = pl.multiple_of(t * tk, tk)
            ikt = ikt_ref[:, pl.ds(off, tk)]
            sc = jnp.zeros((tq, tk), F32)
            for h in range(IDX_HEADS):
                s = _dot(iq[:, h * IDX_DIM:(h + 1) * IDX_DIM], ikt)
                sc = sc + jnp.maximum(s, 0.0).astype(BF16).astype(F32) * iw[:, h:h + 1].astype(BF16).astype(F32)
            kpos = off + lax.broadcasted_iota(I32, (tq, tk), 1)
            keys_s[:, pl.ds(off, tk)] = jnp.where(kpos <= qpos, _sortable(sc * IDX_SCALE), INT_MIN)
            return carry

        lax.fori_loop(0, kl + 1, scores, 0)
        thr, rem = _kth_largest(keys_s, kl + 1, tk, n_sel)
        thr_s[...] = thr
        rem_s[...] = rem
        cnt_s[...] = jnp.zeros_like(cnt_s)
        m_s[...] = jnp.full_like(m_s, NEG_INF)
        l_s[...] = jnp.zeros_like(l_s)
        acc_s[...] = jnp.zeros_like(acc_s)

    def attend():
        off = pl.multiple_of(kt * tk, tk)
        sel, cnt = _select(keys_s[:, pl.ds(off, tk)], thr_s[...], rem_s[...], cnt_s[...], tri_ref[...])
        cnt_s[...] = cnt
        neg_s[...] = jnp.where(sel, 0.0, NEG_INF)
        for h in range(D_HEADS):
            sb, pb = s_s.at[h % 2], p_s.at[h % 2]
            sb[...] = _dot(q_ref[h], kt_ref[h * HEAD_DIM:(h + 1) * HEAD_DIM, :])
            for r in range(0, tq, DSA_RC):
                rs = slice(r, r + DSA_RC)
                a, ro = r // DSA_BQ, r % DSA_BQ
                bias = jnp.concatenate([bias_refs[a * nkb + b][0, h, ro:ro + DSA_RC, :] for b in range(nkb)], axis=1)
                s = sb[rs, :] * (LOG2E * HEAD_DIM ** -0.5) + bias + neg_s[rs, :]
                m_old = m_s[h, rs]
                m_new = jnp.maximum(m_old, jnp.max(s, axis=1, keepdims=True))
                m_safe = jnp.where(m_new == NEG_INF, 0.0, m_new)
                p = jnp.exp2(s - m_safe)
                alpha = jnp.exp2(m_old - m_safe)
                l_s[h, rs] = alpha * l_s[h, rs] + jnp.sum(p, axis=1, keepdims=True)
                alpha_s[h % 2, rs] = alpha
                pb[rs, :] = p.astype(BF16)
                m_s[h, rs] = m_new
            acc_s[h] = alpha_s[h % 2] * acc_s[h] + _dot_nt(pb[...], vt_ref[h * HEAD_DIM:(h + 1) * HEAD_DIM, :])

    attend()

    @pl.when(kt == kl)
    def _():
        o_ref[...] = jnp.concatenate([acc_s[h] / l_s[h] for h in range(D_HEADS)], axis=1)


def dsa_prompt(q, iq, misc, ikt, kt_, v, bias_tiles, tri, n_sel):
    seq = q.shape[1]
    tq, tk = DSA_TQ, DSA_TK
    nb = seq // tq
    pairs = [(b, t) for b in range(nb) for t in range((b * tq + tq - 1) // tk + 1)]
    qb_tab = jnp.asarray(np.array([p[0] for p in pairs], np.int32))
    kt_tab = jnp.asarray(np.array([p[1] for p in pairs], np.int32))
    qrow = lambda i, qbt, ktt: (qbt[i], 0)
    fix = lambda i, qbt, ktt: (0, 0)
    ktile = lambda i, qbt, ktt: (0, ktt[i])

    def bias_spec(a, b):
        def idx(i, qbt, ktt):
            d = (qbt[i] * tq + a * DSA_BQ - ktt[i] * tk - b * DSA_BK) // LANES
            return (jnp.clip(d, 0, DSA_DCAP // LANES), 0, 0, 0)
        return pl.BlockSpec((1, D_HEADS, DSA_BQ, DSA_BK), idx)

    bias_specs = [bias_spec(a, b) for a in range(tq // DSA_BQ) for b in range(tk // DSA_BK)]
    grid_spec = pltpu.PrefetchScalarGridSpec(
        num_scalar_prefetch=2, grid=(len(pairs),),
        in_specs=[pl.BlockSpec((D_HEADS, tq, HEAD_DIM), lambda i, qbt, ktt: (0, qbt[i], 0)),
                  pl.BlockSpec((tq, IDX_HEADS * IDX_DIM), qrow),
                  pl.BlockSpec((tq, LANES), qrow), pl.BlockSpec(ikt.shape, fix),
                  pl.BlockSpec((D_WIDTH, tk), ktile), pl.BlockSpec((D_WIDTH, tk), ktile)] +
                 bias_specs + [pl.BlockSpec(tri.shape, fix)],
        out_specs=pl.BlockSpec((tq, D_WIDTH), qrow),
        scratch_shapes=[pltpu.VMEM((tq, seq), I32), pltpu.VMEM((tq, 1), I32), pltpu.VMEM((tq, 1), F32),
                        pltpu.VMEM((tq, 1), F32), pltpu.VMEM((tq, tk), F32), pltpu.VMEM((D_HEADS, tq, 1), F32),
                        pltpu.VMEM((D_HEADS, tq, 1), F32), pltpu.VMEM((D_HEADS, tq, HEAD_DIM), F32),
                        pltpu.VMEM((2, tq, tk), F32), pltpu.VMEM((2, tq, tk), BF16), pltpu.VMEM((2, tq, 1), F32)])
    return pl.pallas_call(
        functools.partial(_dsa_prompt_kernel, n_sel=n_sel), grid_spec=grid_spec,
        out_shape=jax.ShapeDtypeStruct((seq, D_WIDTH), F32),
        compiler_params=_cparams("arbitrary"), name="dsa_prompt")(
            qb_tab, kt_tab, q, iq, misc, ikt, kt_, v, *([bias_tiles] * len(bias_specs)), tri)


def _dsa_step_keys_kernel(pt_ref, iq_ref, iw_ref, *rest, n_new, pg):
    page_refs, new_ref, keys_ref = rest[:pg], rest[pg], rest[pg + 1]
    p = pl.program_id(1)
    npage = pl.num_programs(1) * pg
    psz = new_ref.shape[2]
    rows = keys_ref.shape[1]

    def score(kt):
        r = (jnp.maximum(_dot(iq_ref[0], kt.astype(BF16)), 0.0).astype(BF16).astype(F32) *
             iw_ref[0][:, 0:1].astype(BF16).astype(F32))
        sc = r[0:rows]
        for h in range(1, IDX_HEADS):
            sc = sc + r[h * rows:(h + 1) * rows]
        return _sortable(sc * IDX_SCALE)

    kt = jnp.concatenate([page_ref[0, 0] for page_ref in page_refs], axis=1)
    keys_ref[0, :, pl.ds(pl.multiple_of(p * (pg * psz), psz), pg * psz)] = score(kt)

    @pl.when(p == 0)
    def _():
        s_i = lax.broadcasted_iota(I32, (rows, psz), 0)
        j_i = lax.broadcasted_iota(I32, (rows, psz), 1)
        ok = (j_i <= s_i) & (j_i < n_new)
        keys_ref[0, :, pl.ds(pl.multiple_of(npage * psz, psz), psz)] = jnp.where(ok, score(new_ref[0]), INT_MIN)


def dsa_step_keys(page_table, iqm, iwm, cache_kidx, ik_new, layer, n_new, pg):
    nseq, npage = page_table.shape
    psz = cache_kidx.shape[3]
    rows = iqm.shape[1] // IDX_HEADS
    ncol = (npage + 1) * psz
    seqb = lambda n, p, pt: (n, 0, 0)
    page_spec = lambda i: pl.BlockSpec((1, 1, IDX_DIM, psz),
                                       lambda n, p, pt: (layer, pt[n * npage + p * pg + i], 0, 0))
    grid_spec = pltpu.PrefetchScalarGridSpec(
        num_scalar_prefetch=1, grid=(nseq, npage // pg),
        in_specs=[pl.BlockSpec((1,) + iqm.shape[1:], seqb), pl.BlockSpec((1,) + iwm.shape[1:], seqb)] +
                 [page_spec(i) for i in range(pg)] + [pl.BlockSpec((1,) + ik_new.shape[1:], seqb)],
        out_specs=pl.BlockSpec((1, rows, ncol), seqb))
    return pl.pallas_call(
        functools.partial(_dsa_step_keys_kernel, n_new=n_new, pg=pg), grid_spec=grid_spec,
        out_shape=jax.ShapeDtypeStruct((nseq, rows, ncol), I32),
        compiler_params=_cparams("parallel", "arbitrary"), name="dsa_step_keys")(
            page_table.reshape(-1), iqm, iwm, *([cache_kidx] * pg), ik_new)


def _topk_mask_kernel(keys_ref, tri_ref, neg_ref, *, tw, n_sel):
    thr, rem = _kth_largest(keys_ref, keys_ref.shape[1] // tw, tw, n_sel)
    w = tri_ref.shape[0]

    def chunk(c, cnt):
        off = pl.multiple_of(c * w, w)
        sel, cnt = _select(keys_ref[:, pl.ds(off, w)], thr, rem, cnt, tri_ref[...])
        neg_ref[:, pl.ds(off, w)] = jnp.where(sel, 0.0, NEG_INF)
        return cnt

    lax.fori_loop(0, keys_ref.shape[1] // w, chunk, jnp.zeros(thr.shape, F32))


def topk_mask_rows(keys, tri, tw, n_sel, tr):
    rows, ncol = keys.shape
    return pl.pallas_call(
        functools.partial(_topk_mask_kernel, tw=tw, n_sel=n_sel), grid=(rows // tr,),
        in_specs=[pl.BlockSpec((tr, ncol), lambda i: (i, 0)), pl.BlockSpec(tri.shape, lambda i: (0, 0))],
        out_specs=pl.BlockSpec((tr, ncol), lambda i: (i, 0)),
        out_shape=jax.ShapeDtypeStruct((rows, ncol), F32),
        compiler_params=_cparams("parallel"), name="topk_mask_rows")(keys, tri)


def _dsa_step_attn_kernel(pt_ref, q_ref, neg_ref, neg_new_ref, bias_ref, bias_new_ref, *rest, pg):
    k_refs, v_refs = rest[:pg], rest[pg:2 * pg]
    kn_ref, vn_ref, o_ref, m_s, l_s, acc_s = rest[2 * pg:]
    p = pl.program_id(1)
    nstep = pl.num_programs(1) - 1
    rows = neg_ref.shape[1]

    @pl.when(p == 0)
    def _():
        m_s[...] = jnp.full_like(m_s, NEG_INF)
        l_s[...] = jnp.zeros_like(l_s)
        acc_s[...] = jnp.zeros_like(acc_s)

    def process(kt, vt, neg, bias):
        s = _dot(q_ref[0], kt.astype(BF16)) * (LOG2E * HEAD_DIM ** -0.5) + bias
        s = s + jnp.concatenate([neg] * D_HEADS, axis=0)
        m_old = m_s[...]
        m_new = jnp.maximum(m_old, jnp.max(s, axis=1, keepdims=True))
        m_safe = jnp.where(m_new == NEG_INF, 0.0, m_new)
        pr = jnp.exp2(s - m_safe)
        alpha = jnp.exp2(m_old - m_safe)
        l_s[...] = alpha * l_s[...] + jnp.sum(pr, axis=1, keepdims=True)
        acc_s[...] = alpha * acc_s[...] + _dot_nt(pr.astype(BF16), vt.astype(BF16))
        m_s[...] = m_new

    @pl.when(p < nstep)
    def _():
        process(jnp.concatenate([r[0, 0, 0] for r in k_refs], axis=1),
                jnp.concatenate([r[0, 0, 0] for r in v_refs], axis=1), neg_ref[0], bias_ref[...])

    @pl.when(p == nstep)
    def _():
        process(kn_ref[0, 0], vn_ref[0, 0], neg_new_ref[0], bias_new_ref[...])
        accn = acc_s[...] / l_s[...]
        head = lax.broadcasted_iota(I32, (rows, D_WIDTH), 1) // HEAD_DIM
        out = jnp.zeros((rows, D_WIDTH), F32)
        for h in range(D_HEADS):
            out = out + jnp.where(head == h, accn[h * rows:(h + 1) * rows], 0.0)
        o_ref[0] = out


def dsa_step_attn(page_table, qbd, neg, bias, cache_kvt, kvt_new, layer, pg):
    nseq, npage = page_table.shape
    psz = cache_kvt.shape[4]
    rows = neg.shape[1]
    qr = qbd.shape[1]
    nstep = npage // pg
    seqb = lambda n, p, pt: (n, 0, 0)
    stp = lambda p: jnp.minimum(p, nstep - 1)

    def page_spec(i, part):
        return pl.BlockSpec((1, 1, 1, D_WIDTH, psz),
                            lambda n, p, pt: (layer, pt[n * npage + stp(p) * pg + i], part, 0, 0))

    def new_spec(part):
        return pl.BlockSpec((1, 1, D_WIDTH, psz), lambda n, p, pt: (n, part, 0, 0))

    grid_spec = pltpu.PrefetchScalarGridSpec(
        num_scalar_prefetch=1, grid=(nseq, nstep + 1),
        in_specs=[pl.BlockSpec((1, qr, D_WIDTH), seqb),
                  pl.BlockSpec((1, rows, pg * psz), lambda n, p, pt: (n, 0, stp(p))),
                  pl.BlockSpec((1, rows, psz), lambda n, p, pt: (n, 0, npage)),
                  pl.BlockSpec((qr, pg * psz), lambda n, p, pt: (0, stp(p))),
                  pl.BlockSpec((qr, psz), lambda n, p, pt: (0, npage))] +
                 [page_spec(i, 0) for i in range(pg)] + [page_spec(i, 1) for i in range(pg)] +
                 [new_spec(0), new_spec(1)],
        out_specs=pl.BlockSpec((1, rows, D_WIDTH), seqb),
        scratch_shapes=[pltpu.VMEM((qr, 1), F32), pltpu.VMEM((qr, 1), F32), pltpu.VMEM((qr, D_WIDTH), F32)])
    return pl.pallas_call(
        functools.partial(_dsa_step_attn_kernel, pg=pg), grid_spec=grid_spec,
        out_shape=jax.ShapeDtypeStruct((nseq, rows, D_WIDTH), F32),
        compiler_params=_cparams("parallel", "arbitrary"), name="dsa_step_attn")(
            page_table.reshape(-1), qbd, neg, neg, bias, bias, *([cache_kvt] * (2 * pg)), kvt_new, kvt_new)


def _prep_ab_w(w):
    d = w.shape[0]
    aq, ak, av, bq, bk, bv, bg = _split_cols(w, [A_WIDTH] * 3 + [B_QK_WIDTH] * 2 + [B_V_WIDTH] * 2)
    half = B_KEY_DIM // 2

    def rot(m):
        return m.reshape(d, B_HEADS, 2, half)[:, :, ::-1, :].reshape(d, B_QK_WIDTH)

    pad = jnp.zeros((d, B_QK_PAD - B_QK_WIDTH), w.dtype)
    cols = [aq, ak, av, bq, pad, rot(bq), pad, bk, pad, rot(bk), pad, bv, bg]
    return jnp.concatenate(cols, axis=1).astype(BF16)


AB_SEGS = ((0, 3 * A_WIDTH, F32), (3 * A_WIDTH, 4 * B_QK_PAD, F32),
           (3 * A_WIDTH + 4 * B_QK_PAD, B_V_WIDTH, F32), (3 * A_WIDTH + 4 * B_QK_PAD + B_V_WIDTH, B_V_WIDTH, F32))


def _split_cols(x, widths):
    parts, start = [], 0
    for wd in widths:
        parts.append(x[..., start:start + wd])
        start += wd
    return parts


def _prep_cd_w(w):
    d = w.shape[0]
    cu, dq, dk, dv, iq, iw, ik = _split_cols(w, [C_WIDTH, D_WIDTH, D_WIDTH, D_WIDTH, IDX_HEADS * IDX_DIM, IDX_HEADS, IDX_DIM])
    pad = jnp.zeros((d, LANES - IDX_DIM - IDX_HEADS), w.dtype)
    return jnp.concatenate([cu, dq, dk, dv, iq, ik, iw, pad], axis=1).astype(BF16)


CD_SEGS = ((0, C_WIDTH, F32), (C_WIDTH, D_WIDTH, BF16), (C_WIDTH + D_WIDTH, 2 * D_WIDTH, F32),
           (C_WIDTH + 3 * D_WIDTH, IDX_HEADS * IDX_DIM, BF16), (C_WIDTH + 3 * D_WIDTH + IDX_HEADS * IDX_DIM, LANES, F32))


def _bias_by_distance(t5_table, dist):
    onehot = (_t5_bucket(dist)[..., None] == jnp.arange(T5_BUCKETS)).astype(F32)
    return jnp.einsum('...b,bh->...h', onehot, t5_table.astype(F32), precision=lax.Precision.HIGHEST)


def _dil_group_bias(t5_table, g, dil, nkey):
    return _bias_by_distance(t5_table, jnp.arange(nkey + 1) * dil)[:, g * A_GROUP_HEADS:(g + 1) * A_GROUP_HEADS]


def _dil_prompt_bias(t5_table, g, dil, nkey):
    bias = _dil_group_bias(t5_table, g, dil, nkey)
    ext = jnp.concatenate([jnp.broadcast_to(bias[:1], (nkey - 1, A_GROUP_HEADS)), bias,
                           jnp.broadcast_to(bias[-1:], (nkey - 1, A_GROUP_HEADS))], axis=0)[::-1]
    rows = [lax.slice_in_dim(ext, nkey - 1 - qi, nkey - 1 - qi + 2 * nkey, axis=0) for qi in range(nkey)]
    return jnp.moveaxis(jnp.stack(rows, axis=0), -1, 0)


def _dil_step_bias(t5_table, g, dil, nkey, n_buf, n_new, rows, pad_new):
    assert n_buf == nkey * dil
    rev = _dil_group_bias(t5_table, g, dil, nkey)[::-1]
    width = n_buf + pad_new
    out = []
    for s in range(rows):
        if s < n_new:
            blockc = jnp.full((nkey + 1, dil, A_GROUP_HEADS), NEG_INF, F32).at[:, s % dil, :].set(rev)
            flat = blockc.reshape((nkey + 1) * dil, A_GROUP_HEADS)
            front = s - s % dil
            row = jnp.pad(flat, ((front, width - front - flat.shape[0]), (0, 0)), constant_values=NEG_INF)
        else:
            row = jnp.full((width, A_GROUP_HEADS), NEG_INF, F32).at[0].set(0.0)
        out.append(row)
    return jnp.stack(out, axis=0).transpose(2, 0, 1).reshape(A_GROUP_HEADS * rows, width)


def _dsa_prompt_table(t5_table):
    d = (DSA_DCAP + LANES - 1) - jnp.arange(DSA_TAB)
    return _bias_by_distance(t5_table, d)[:, :D_HEADS].T * LOG2E


def _tri(n):
    r = np.arange(n)
    return jnp.asarray((r[:, None] <= r[None, :]).astype(np.float32)).astype(BF16)


def _ffn_and_ple(h, p, i, prm, tm, final):
    h = moe(h, prm['norm_ffn'][i], prm['moe_wr'][i], prm['moe_br'][i], prm['moe_wg'][i], prm['moe_wu'][i],
            prm['moe_wd'][i], min(4 * tm, h.shape[0]))
    return ple(h, p, prm['norm_ple'][i], prm['ple_wg'][i], prm['ple_wp'][i], prm['norm_final'] if final else None, tm)


def _layer_ab_prompt(h, prm, j, tm):
    seq = h.shape[0]
    aqkv, bqk, bv, bg = norm_matmul(h, prm['norm_mix'][2 * j], prm['ab_w'][j], AB_SEGS, tm)
    os_, ls, bufs = [], [], []
    for g, (win, dil) in enumerate(A_PATTERNS):
        nkey = win // dil
        assert seq % (dil * nkey) == 0
        o, lse = dil_prompt(aqkv, _dil_prompt_bias(prm['t5_table'], g, dil, nkey), g, dil)
        os_.append(o)
        ls.append(lse)
        kcols = aqkv[seq - min(win, seq):, A_WIDTH + g * LANES:A_WIDTH + (g + 1) * LANES]
        vcols = aqkv[seq - min(win, seq):, 2 * A_WIDTH + g * LANES:2 * A_WIDTH + (g + 1) * LANES]
        bufs.append(jnp.stack([kcols, vcols], axis=1).reshape(1, -1, 2, A_GROUP_HEADS, HEAD_DIM))
    chunk = B_CHUNK if seq % B_CHUNK == 0 else seq
    cosv, sinv = _rope_tables(jnp.arange(seq))
    s0 = jnp.zeros((1, B_HEADS, B_KEY_DIM, B_VAL_DIM), F32)
    b_out, s_new = retention(bqk, cosv, sinv, bv, bg, _retention_tables(chunk, chunk), prm['ab_gn_gain'][j], s0,
                             chunk, seq // chunk)
    h = ab_out(h, os_, ls, b_out, prm['ab_w_out'][j], tm)
    return h, bufs, s_new


def _layer_ab_sample(h, prm, j, past, nseq, n_new, pos0):
    rows = 8
    aqkv, bqk, bv, bg = norm_matmul(h, prm['norm_mix'][2 * j], prm['ab_w'][j], AB_SEGS, h.shape[0])
    a3 = aqkv.reshape(nseq, n_new, 3 * A_WIDTH)
    pad_rows = lambda x, r: jnp.pad(x, ((0, 0), (0, r - x.shape[1]), (0, 0)))
    os_, ls, bufs = [], [], []
    for g, (win, dil) in enumerate(A_PATTERNS):
        nkey = win // dil
        buf = past['a'][g]
        n_buf = buf.shape[2]
        cs = lambda part: slice(part * A_WIDTH + g * LANES, part * A_WIDTH + (g + 1) * LANES)
        q = pad_rows(a3[:, :, cs(0)], rows)
        kv_new = jnp.concatenate([a3[:, :, cs(1)], a3[:, :, cs(2)]], axis=2)
        bias = _dil_step_bias(prm['t5_table'], g, dil, nkey, n_buf, n_new, rows, LANES)
        buf_t = buf.transpose(0, 1, 3, 4, 5, 2).reshape(buf.shape[0], nseq, 2, LANES, n_buf)
        new_t = jnp.pad(kv_new.reshape(nseq, n_new, 2, LANES).transpose(0, 2, 3, 1),
                        ((0, 0), (0, 0), (0, 0), (0, LANES - n_new)))
        o, lse = dil_step(q, buf_t, new_t, bias, j)
        os_.append(o[:, :n_new].reshape(nseq * n_new, LANES))
        ls.append(lse[:, :n_new].reshape(nseq * n_new, LANES))
        kv_all = jnp.concatenate([buf[j].reshape(nseq, n_buf, 2 * LANES), kv_new], axis=1)
        bufs.append(kv_all[:, n_new:].reshape(nseq, n_buf, 2, A_GROUP_HEADS, HEAD_DIM))
    chunk = B_CHUNK if n_new % B_CHUNK == 0 else n_new
    assert chunk == n_new
    cpad = 16
    cosv, sinv = _rope_tables(pos0 + jnp.arange(cpad))
    padc = lambda x: jnp.pad(x.reshape(nseq, n_new, -1), ((0, 0), (0, cpad - n_new), (0, 0))).reshape(nseq * cpad, -1)
    b_out, s_new = retention(padc(bqk), cosv, sinv, padc(bv), padc(bg), _retention_tables(chunk, cpad),
                             prm['ab_gn_gain'][j], past['b'][j].astype(F32), cpad, 1)
    b_out = b_out.reshape(nseq, cpad, -1)[:, :n_new].reshape(nseq * n_new, -1)
    h = ab_out(h, os_, ls, b_out, prm['ab_w_out'][j], h.shape[0])
    return h, bufs, s_new


def _cd_project(h, prm, j, tm):
    return norm_matmul(h, prm['norm_mix'][2 * j + 1], prm['cd_w'][j], CD_SEGS, tm)


def _layer_cd_prompt(h, prm, j, tm):
    seq = h.shape[0]
    cu, dq, kv, iq, misc = _cd_project(h, prm, j, tm)
    x0 = jnp.zeros((1, C_FLAT), F32)
    c_out, xr, xi = s5(cu, prm['s5'][j], x0, x0, 1, min(256, seq))
    sc = jnp.stack([xr.reshape(C_GROUPS, C_STATE), xi.reshape(C_GROUPS, C_STATE)], axis=-1)[None]
    n_sel = min(IDX_TOPK, seq // 4)
    heads_first = lambda x: x.reshape(seq, D_HEADS, HEAD_DIM).transpose(1, 0, 2)
    kb = kv[:, :D_WIDTH].astype(BF16).T
    vb = kv[:, D_WIDTH:].astype(BF16).T
    ikt = misc[:, :IDX_DIM].astype(BF16).T
    bias_tiles = dsa_bias_tiles(_dsa_prompt_table(prm['t5_table']))
    d_out = dsa_prompt(heads_first(dq), iq, misc, ikt, kb, vb, bias_tiles, _tri(DSA_TRI), n_sel)
    h = cd_out(h, c_out, d_out, prm['cd_w_out'][j], tm)
    return h, sc, kv, misc[:, :IDX_DIM]


def _layer_cd_sample(h, prm, j, past, nseq, n_new):
    rows = 8
    cu, dq, kv, iq, misc = _cd_project(h, prm, j, h.shape[0])
    tmaj = lambda x: x.reshape(nseq, n_new, -1).transpose(1, 0, 2).reshape(nseq * n_new, -1)
    x0 = past['c'][j].astype(F32)
    c_out, xr, xi = s5(tmaj(cu), prm['s5'][j], x0[..., 0].reshape(nseq, C_FLAT), x0[..., 1].reshape(nseq, C_FLAT),
                       nseq, n_new)
    c_out = c_out.reshape(n_new, nseq, -1).transpose(1, 0, 2).reshape(nseq * n_new, -1)
    sc = jnp.stack([xr.reshape(nseq, C_GROUPS, C_STATE), xi.reshape(nseq, C_GROUPS, C_STATE)], axis=-1)

    page_table = past['page_table']
    cache_kv, cache_kidx = past['d_kv'], past['d_kidx']
    npage = page_table.shape[1]
    psz = cache_kv.shape[2]
    n_past = npage * psz
    n_sel = min(IDX_TOPK, (n_past + n_new) // 4)
    pad_rows = lambda x, r: jnp.pad(x, ((0, 0), (0, r - x.shape[1]), (0, 0)))
    ik_new = misc[:, :IDX_DIM]
    iqm = pad_rows(iq.reshape(nseq, n_new, IDX_HEADS * IDX_DIM), rows).reshape(nseq, rows, IDX_HEADS, IDX_DIM)
    iqm = iqm.transpose(0, 2, 1, 3).reshape(nseq, IDX_HEADS * rows, IDX_DIM)
    iwm = pad_rows(misc[:, IDX_DIM:IDX_DIM + IDX_HEADS].reshape(nseq, n_new, IDX_HEADS), rows)
    iwm = jnp.broadcast_to(iwm.transpose(0, 2, 1).reshape(nseq, IDX_HEADS * rows, 1), (nseq, IDX_HEADS * rows, LANES))
    kidx_t = cache_kidx.transpose(0, 1, 3, 2)
    kv_t = cache_kv.transpose(0, 1, 3, 4, 5, 2).reshape(cache_kv.shape[0], cache_kv.shape[1], 2, D_WIDTH, psz)
    lane_pad = lambda x: jnp.pad(x, ((0, 0),) * (x.ndim - 1) + ((0, psz - x.shape[-1]),))
    ik_new_t = lane_pad(ik_new.reshape(nseq, n_new, IDX_DIM).transpose(0, 2, 1))
    keys = dsa_step_keys(page_table, iqm, iwm, kidx_t, ik_new_t, j, n_new, math.gcd(npage, 32))
    ncol = keys.shape[2]
    neg = topk_mask_rows(keys.reshape(nseq * rows, ncol), _tri(LANES),
                         ncol // 3 if (ncol // LANES) % 3 == 0 else LANES, n_sel, min(LANES, nseq * rows))
    qpos = n_past + jnp.minimum(jnp.arange(rows), n_new - 1)
    dist = qpos[:, None] - jnp.arange(ncol)[None, :]
    bias = _bias_by_distance(prm['t5_table'], dist)[..., :D_HEADS] * LOG2E
    bias = bias.transpose(2, 0, 1).reshape(D_HEADS * rows, ncol)
    q = pad_rows(dq.reshape(nseq, n_new, D_WIDTH), rows)
    col_head = jnp.arange(D_WIDTH) // HEAD_DIM
    qbd = jnp.where(col_head[None, None, None, :] == jnp.arange(D_HEADS)[None, :, None, None], q[:, None], 0)
    qbd = qbd.reshape(nseq, D_HEADS * rows, D_WIDTH).astype(dq.dtype)
    kvt_new = lane_pad(kv.reshape(nseq, n_new, 2, D_WIDTH).transpose(0, 2, 3, 1))
    d_out = dsa_step_attn(page_table, qbd, neg.reshape(nseq, rows, ncol), bias, kv_t, kvt_new, j, math.gcd(npage, 16))
    d_out = d_out[:, :n_new].reshape(nseq * n_new, D_WIDTH)
    h = cd_out(h, c_out, d_out, prm['cd_w_out'][j], h.shape[0])
    return h, sc, kv, ik_new


def _trunk(x, p, prm, past, nseq, n_new, pos0, tm):
    depth = p.shape[0]
    h = x
    a_new = [[] for _ in A_PATTERNS]
    b_new, c_new, kv_new, kidx_new = [], [], [], []
    for i in range(depth):
        j = i // 2
        if i % 2 == 0:
            if past is None:
                h, bufs, sb = _layer_ab_prompt(h, prm, j, tm)
            else:
                h, bufs, sb = _layer_ab_sample(h, prm, j, past, nseq, n_new, pos0)
            for g in range(len(A_PATTERNS)):
                a_new[g].append(bufs[g])
            b_new.append(sb)
        else:
            if past is None:
                h, sc, kv, kidx = _layer_cd_prompt(h, prm, j, tm)
            else:
                h, sc, kv, kidx = _layer_cd_sample(h, prm, j, past, nseq, n_new)
            c_new.append(sc)
            kv_new.append(kv.reshape(nseq, n_new, 2, D_HEADS, HEAD_DIM))
            kidx_new.append(kidx.reshape(nseq, n_new, IDX_DIM))
        h = _ffn_and_ple(h, p[i], i, prm, tm, i == depth - 1)
    stack = lambda t: jnp.stack(t, axis=0)
    return h, tuple(stack(t) for t in a_new), stack(b_new), stack(c_new), stack(kv_new), stack(kidx_new)


def kernel(x_prompt, x_sample, p_prompt, p_sample, cache_a_kv0, cache_a_kv1, cache_a_kv2, state_b, state_c, cache_d_kv, cache_d_kidx, page_table, t5_table, norm_mix, norm_ffn, norm_ple, norm_final, ab_w_in, ab_w_out, ab_gn_gain, cd_w_in, cd_w_out, c_lambda_re, c_lambda_im, c_log_dt, c_b_re, c_b_im, c_c_re, c_c_im, c_d, c_w_glu, c_b_glu, moe_w_group, moe_b_group, moe_w_expert, moe_b_expert, moe_w_gate, moe_w_up, moe_w_down, ple_w_gate, ple_w_proj):
    depth, d_model = norm_mix.shape
    n_ab, n_cd = ab_w_in.shape[0], cd_w_in.shape[0]
    rpad = jnp.zeros((depth, d_model, LANES - MOE_GROUPS - MOE_EXPERTS), F32)
    prm = dict(
        t5_table=t5_table, norm_mix=norm_mix, norm_ffn=norm_ffn, norm_ple=norm_ple, norm_final=norm_final,
        ab_w=[_prep_ab_w(ab_w_in[j]) for j in range(n_ab)], ab_w_out=ab_w_out.astype(BF16), ab_gn_gain=ab_gn_gain,
        cd_w=[_prep_cd_w(cd_w_in[j]) for j in range(n_cd)], cd_w_out=cd_w_out.astype(BF16),
        s5=[_s5_prep(c_lambda_re[j], c_lambda_im[j], c_log_dt[j], c_b_re[j], c_b_im[j], c_c_re[j], c_c_im[j],
                     c_d[j], c_w_glu[j], c_b_glu[j]) for j in range(n_cd)],
        moe_wr=jnp.concatenate([moe_w_group, moe_w_expert.reshape(depth, d_model, MOE_EXPERTS), rpad],
                               axis=2).astype(BF16),
        moe_br=jnp.concatenate([moe_b_group, moe_b_expert.reshape(depth, MOE_EXPERTS),
                                jnp.zeros((depth, LANES - MOE_GROUPS - MOE_EXPERTS), F32)], axis=1)[:, None, :],
        moe_wg=moe_w_gate.astype(BF16), moe_wu=moe_w_up.astype(BF16), moe_wd=moe_w_down.astype(BF16),
        ple_wg=ple_w_gate.astype(BF16), ple_wp=ple_w_proj.astype(BF16))

    bsz, seq, _ = x_prompt.shape
    assert bsz == 1
    y_p, a_p, sb_p, sc_p, dkv_p, dki_p = _trunk(x_prompt[0], p_prompt[:, 0], prm, None, 1, seq, 0, min(256, seq))
    y_p = y_p[None]

    nseq, n_new, _ = x_sample.shape
    past = dict(a=(cache_a_kv0, cache_a_kv1, cache_a_kv2), b=state_b, c=state_c, d_kv=cache_d_kv, d_kidx=cache_d_kidx,
                page_table=page_table)
    n_past = page_table.shape[1] * cache_d_kv.shape[2]
    rows_s = nseq * n_new
    y_s, a_s, sb_s, sc_s, dkv_s, dki_s = _trunk(x_sample.reshape(rows_s, d_model),
                                                p_sample.reshape(depth, rows_s, -1), prm, past, nseq, n_new, n_past,
                                                rows_s)
    y_s = y_s.reshape(nseq, n_new, d_model)
    return (y_p, y_s, a_p[0], a_s[0], a_p[1], a_s[1], a_p[2], a_s[2], sb_p, sb_s, sc_p, sc_s, dkv_p, dkv_s, dki_p, dki_s)
```
